```python
import jax
import jax.numpy as jnp
from jax import lax
import numpy as np

D_MODEL = 4096
BATCH = 4
SEQ = 4096
DEPTH = 2

GRID_W = 64
CTX_LEN = 256
MIX_W = D_MODEL
POOL_W = MIX_W // 4
POOL_WINDOWS = (2, 4, 8, 16)
POOL_GROUP = POOL_W // len(POOL_WINDOWS)
RWKV_W = MIX_W - POOL_W
HEAD_SIZE = 64
N_HEADS = RWKV_W // HEAD_SIZE
DECAY_LORA = 96
ICLR_LORA = 96
VRES_LORA = 64
GATE_LORA = 384
N_ADA = 6
D_FF = 11008
N_EXPERTS = 8
TOP_K = 2
D_EXPERT = 4096
ALPHA = (2 * DEPTH) ** 0.25
BETA = (8 * DEPTH) ** -0.25
LN_EPS = 1e-5
GN_EPS = 64e-5
NORM_EPS = 1e-12

kernel_name = 'hybrid_pool_rwkv7_moe_dit'


def _rwkv_sizes(has_vres):
    sizes = [RWKV_W, RWKV_W, RWKV_W, DECAY_LORA, DECAY_LORA, ICLR_LORA, ICLR_LORA, GATE_LORA]
    return sizes + [VRES_LORA] if has_vres else sizes


def _split(z, sizes):
    cuts = [int(i) for i in np.cumsum(sizes)[:-1]]
    return jnp.split(z, cuts, axis=-1)


def _layer_norm(x, gain=None, bias=None):
    xf = x.astype(jnp.float32)
    mu = jnp.mean(xf, axis=-1, keepdims=True)
    var = jnp.mean(jnp.square(xf - mu), axis=-1, keepdims=True)
    y = (xf - mu) * lax.rsqrt(var + LN_EPS)
    if gain is not None:
        y = y * gain + bias
    return y.astype(x.dtype)


def _modulation(cond, w_ada, b_ada):
    return jnp.split(jax.nn.silu(cond) @ w_ada + b_ada, N_ADA, axis=-1)


def _modulate(x, shift, scale):
    return _layer_norm(x) * (1 + scale) + shift


def _centred_shift(f, mu_prev, mu_next):
    prev = jnp.pad(f, ((0, 0), (1, 0), (0, 0)))[:, :-1]
    nxt = jnp.pad(f, ((0, 0), (0, 1), (0, 0)))[:, 1:]
    return f + mu_prev * (prev - f) + mu_next * (nxt - f)


def _centred_mean(x, win):
    n = x.shape[1]
    xf = x.astype(jnp.float32)
    cs = jnp.concatenate([jnp.zeros_like(xf[:, :1]), jnp.cumsum(xf, axis=1)], axis=1)
    t = jnp.arange(n)
    hi = jnp.minimum(t + win // 2, n)
    lo = jnp.maximum(t - win // 2, 0)
    cnt = (hi - lo).astype(jnp.float32).reshape((1, n) + (1,) * (x.ndim - 2))
    return ((cs[:, hi] - cs[:, lo]) / cnt).astype(x.dtype)


def _pool_mixer(u, pool_w, pool_scale, on_grid):
    B, T, _ = u.shape
    grp = u.reshape(B, T, len(POOL_WINDOWS), POOL_GROUP)
    diffs = []
    for i, win in enumerate(POOL_WINDOWS):
        g = grp[:, :, i]
        if on_grid:
            rows = T // GRID_W
            m = _centred_mean(g.reshape(B, rows, GRID_W, POOL_GROUP), win)
            m = jnp.swapaxes(_centred_mean(jnp.swapaxes(m, 1, 2), win), 1, 2)
            m = m.reshape(B, T, POOL_GROUP)
        else:
            m = _centred_mean(g, win)
        diffs.append(m - g)
    d = jnp.stack(diffs, axis=2)
    y = jnp.einsum('btgc,gcd->btgd', d, pool_w).reshape(B, T, POOL_W)
    return y * pool_scale


def _decay(zw, w0, w2):
    wl = (w0 + jnp.tanh(zw) @ w2).astype(jnp.float32)
    return jnp.exp(-jnp.exp(-jax.nn.softplus(-wl) - 0.5))


def _rwkv_terms(z, p, v_first):
    B, T, _ = z.shape
    parts = _split(z, _rwkv_sizes(v_first is not None))
    r, k, v, zwf, zwb, zaf, zab, zg = parts[:8]
    if v_first is not None:
        v = v + (v_first - v) * jax.nn.sigmoid(p['v0'] + parts[8] @ p['v2'])
    w_f = _decay(zwf, p['w0_f'], p['w2_f'])
    w_b = _decay(zwb, p['w0_b'], p['w2_b'])
    a_f = jax.nn.sigmoid(p['a0_f'] + zaf @ p['a2_f'])
    a_b = jax.nn.sigmoid(p['a0_b'] + zab @ p['a2_b'])
    heads = lambda t: t.reshape(B, T, N_HEADS, HEAD_SIZE)
    kk = heads(k * p['k_k']).astype(jnp.float32)
    kk = kk / jnp.maximum(jnp.linalg.norm(kk, axis=-1, keepdims=True), NORM_EPS)
    k_f = k * (1 + (a_f - 1) * p['k_a'])
    k_b = k * (1 + (a_b - 1) * p['k_a'])
    return {'r': heads(r), 'v': heads(v), 'v_flat': v, 'k_f': heads(k_f), 'k_b': heads(k_b),
            'w_f': heads(w_f), 'w_b': heads(w_b), 'kk': kk, 'a_f': heads(a_f), 'a_b': heads(a_b),
            'zg': zg}


def _wkv7(r, w, k, v, aa, bb, reverse):
    B, T, H, N = r.shape

    def step(state, inp):
        r_t, w_t, k_t, v_t, a_t, b_t = inp
        sa = jnp.einsum('bhvk,bhk->bhv', state, a_t)
        state = (state * w_t[:, :, None, :] + sa[..., None] * b_t[:, :, None, :]
                 + v_t[..., None] * k_t[:, :, None, :])
        return state, jnp.einsum('bhvk,bhk->bhv', state, r_t)

    seq = tuple(jnp.moveaxis(t.astype(jnp.float32), 1, 0) for t in (r, w, k, v, aa, bb))
    s0 = jnp.zeros((B, H, N, N), jnp.float32)
    _, y = lax.scan(step, s0, seq, reverse=reverse)
    return jnp.moveaxis(y, 0, 1)


def _bidir_wkv(tc, tl):
    fwd = lambda n: jnp.concatenate([tc[n], tl[n]], axis=1)
    bwd = lambda n: jnp.concatenate([tl[n], tc[n]], axis=1)
    y_f = _wkv7(fwd('r'), fwd('w_f'), fwd('k_f'), fwd('v'), -fwd('kk'), fwd('kk') * fwd('a_f'), False)
    y_b = _wkv7(bwd('r'), bwd('w_b'), bwd('k_b'), bwd('v'), -bwd('kk'), bwd('kk') * bwd('a_b'), True)
    return y_f, y_b


def _rwkv_out(y, t, p):
    B, T = y.shape[:2]
    mu = jnp.mean(y, axis=-1, keepdims=True)
    var = jnp.mean(jnp.square(y - mu), axis=-1, keepdims=True)
    yn = ((y - mu) * lax.rsqrt(var + GN_EPS)).reshape(B, T, RWKV_W).astype(t['zg'].dtype)
    yn = yn * p['lnx_g'] + p['lnx_b']
    bonus = jnp.sum(t['r'] * (t['k_f'] + t['k_b']) * p['r_k'], axis=-1, keepdims=True) * t['v']
    gate = jax.nn.sigmoid(t['zg']) @ p['g_up']
    return (yn + bonus.reshape(B, T, RWKV_W)) * gate


def _mixer(h_ctx, h_lat, p, v_first, need_ctx):
    vf_ctx, vf_lat = v_first if v_first is not None else (None, None)
    Lc, T = h_ctx.shape[1], h_lat.shape[1]
    u_lat = h_lat @ p['w_in']
    pool_lat = _pool_mixer(u_lat[..., :POOL_W], p['pool_w'], p['pool_scale'], True)
    if need_ctx:
        u_ctx = h_ctx @ p['w_in']
        z_ctx = u_ctx[..., POOL_W:]
    else:
        z_ctx = h_ctx @ p['w_in'][:, POOL_W:]
    tl = _rwkv_terms(_centred_shift(u_lat[..., POOL_W:], p['mu_prev'], p['mu_next']), p, vf_lat)
    tc = _rwkv_terms(_centred_shift(z_ctx, p['mu_prev'], p['mu_next']), p, vf_ctx)
    y_f, y_b = _bidir_wkv(tc, tl)
    out_lat = _rwkv_out(y_f[:, Lc:] + y_b[:, :T], tl, p)
    mix_lat = jnp.concatenate([pool_lat, out_lat], axis=-1) @ p['w_out']
    mix_ctx = None
    if need_ctx:
        pool_ctx = _pool_mixer(u_ctx[..., :POOL_W], p['pool_w'], p['pool_scale'], False)
        out_ctx = _rwkv_out(y_f[:, :Lc] + y_b[:, T:], tc, p)
        mix_ctx = jnp.concatenate([pool_ctx, out_ctx], axis=-1) @ p['w_out']
    return mix_ctx, mix_lat, (tc['v_flat'], tl['v_flat'])


def _swiglu(h, w1, w3, w2):
    return (jax.nn.silu(h @ w1) * (h @ w3)) @ w2


def _moe(h, router, w1, w3, w2):
    logits = (h @ router).astype(jnp.float32)
    top_v, top_i = lax.top_k(logits, TOP_K)
    gates = jax.nn.softmax(top_v, axis=-1)
    out = jnp.zeros_like(h)
    for e in range(N_EXPERTS):
        g_e = jnp.sum(jnp.where(top_i == e, gates, 0.0), axis=-1, keepdims=True).astype(h.dtype)
        out = out + g_e * _swiglu(h, w1[e], w3[e], w2[e])
    return out


def _channel_mixer(h, p, layer):
    if layer % 2 == 0:
        return _swiglu(h, p['ffn_w1'], p['ffn_w3'], p['ffn_w2'])
    return _moe(h, p['router'], p['exp_w1'], p['exp_w3'], p['exp_w2'])


def _normal(key, shape, scale):
    return jax.random.normal(key, shape, jnp.float32) * scale


def _layer_params(key, layer):
    cols = POOL_W + sum(_rwkv_sizes(layer > 0))
    rcols = cols - POOL_W
    ks = iter(jax.random.split(key, 40))
    nk = lambda: next(ks)
    p = {}
    p['w_ada'] = _normal(nk(), (D_MODEL, N_ADA * D_MODEL), D_MODEL ** -0.5)
    p['b_ada'] = _normal(nk(), (N_ADA * D_MODEL,), 0.02)
    p['w_in'] = _normal(nk(), (D_MODEL, cols), D_MODEL ** -0.5)
    p['mu_prev'] = jax.random.uniform(nk(), (rcols,), jnp.float32, 0.0, 0.5)
    p['mu_next'] = jax.random.uniform(nk(), (rcols,), jnp.float32, 0.0, 0.5)
    p['pool_w'] = _normal(nk(), (len(POOL_WINDOWS), POOL_GROUP, POOL_GROUP), POOL_GROUP ** -0.5)
    p['pool_scale'] = 1.0 + _normal(nk(), (POOL_W,), 0.02)
    for d in ('f', 'b'):
        p['w0_' + d] = jax.random.uniform(nk(), (RWKV_W,), jnp.float32, -6.0, -1.0)
        p['w2_' + d] = _normal(nk(), (DECAY_LORA, RWKV_W), 0.5 * DECAY_LORA ** -0.5)
    for d in ('f', 'b'):
        p['a0_' + d] = _normal(nk(), (RWKV_W,), 0.1)
        p['a2_' + d] = _normal(nk(), (ICLR_LORA, RWKV_W), ICLR_LORA ** -0.5)
    p['g_up'] = _normal(nk(), (GATE_LORA, RWKV_W), GATE_LORA ** -0.5)
    p['k_k'] = 0.85 + _normal(nk(), (RWKV_W,), 0.02)
    p['k_a'] = 1.0 + _normal(nk(), (RWKV_W,), 0.02)
    p['r_k'] = _normal(nk(), (N_HEADS, HEAD_SIZE), 0.1)
    p['lnx_g'] = 1.0 + _normal(nk(), (RWKV_W,), 0.02)
    p['lnx_b'] = _normal(nk(), (RWKV_W,), 0.02)
    p['w_out'] = _normal(nk(), (MIX_W, D_MODEL), BETA * MIX_W ** -0.5)
    p['ln1_g'] = 1.0 + _normal(nk(), (D_MODEL,), 0.02)
    p['ln1_b'] = _normal(nk(), (D_MODEL,), 0.02)
    p['ln2_g'] = 1.0 + _normal(nk(), (D_MODEL,), 0.02)
    p['ln2_b'] = _normal(nk(), (D_MODEL,), 0.02)
    if layer > 0:
        p['v0'] = _normal(nk(), (RWKV_W,), 0.1)
        p['v2'] = _normal(nk(), (VRES_LORA, RWKV_W), 0.5 * VRES_LORA ** -0.5)
    if layer % 2 == 0:
        p['ffn_w1'] = _normal(nk(), (D_MODEL, D_FF), D_MODEL ** -0.5)
        p['ffn_w3'] = _normal(nk(), (D_MODEL, D_FF), D_MODEL ** -0.5)
        p['ffn_w2'] = _normal(nk(), (D_FF, D_MODEL), BETA * D_FF ** -0.5)
    else:
        p['router'] = _normal(nk(), (D_MODEL, N_EXPERTS), D_MODEL ** -0.5)
        p['exp_w1'] = _normal(nk(), (N_EXPERTS, D_MODEL, D_EXPERT), D_MODEL ** -0.5)
        p['exp_w3'] = _normal(nk(), (N_EXPERTS, D_MODEL, D_EXPERT), D_MODEL ** -0.5)
        p['exp_w2'] = _normal(nk(), (N_EXPERTS, D_EXPERT, D_MODEL), BETA * D_EXPERT ** -0.5)
    return p


def setup_inputs(seed: int = 0) -> dict:
    key = jax.random.key(seed)
    kx, kc, kctx, kcc, kl0, kl1 = jax.random.split(key, 6)
    inputs = {
        'x': jax.random.normal(kx, (BATCH, SEQ, D_MODEL), jnp.float32),
        'c': jax.random.normal(kc, (BATCH, D_MODEL), jnp.float32),
        'ctx': jax.random.normal(kctx, (BATCH, CTX_LEN, D_MODEL), jnp.float32),
        'c_ctx': jax.random.normal(kcc, (D_MODEL,), jnp.float32),
    }
    for i, kl in enumerate((kl0, kl1)):
        for name, val in _layer_params(kl, i).items():
            inputs['l%d_%s' % (i, name)] = val
    return inputs


def reference(x, c, ctx, c_ctx,
              l0_w_ada, l0_b_ada, l0_w_in, l0_mu_prev, l0_mu_next, l0_pool_w, l0_pool_scale,
              l0_w0_f, l0_w2_f, l0_w0_b, l0_w2_b, l0_a0_f, l0_a2_f, l0_a0_b, l0_a2_b,
              l0_g_up, l0_k_k, l0_k_a, l0_r_k, l0_lnx_g, l0_lnx_b, l0_w_out,
              l0_ln1_g, l0_ln1_b, l0_ln2_g, l0_ln2_b, l0_ffn_w1, l0_ffn_w3, l0_ffn_w2,
              l1_w_ada, l1_b_ada, l1_w_in, l1_mu_prev, l1_mu_next, l1_pool_w, l1_pool_scale,
              l1_w0_f, l1_w2_f, l1_w0_b, l1_w2_b, l1_a0_f, l1_a2_f, l1_a0_b, l1_a2_b,
              l1_g_up, l1_k_k, l1_k_a, l1_r_k, l1_lnx_g, l1_lnx_b, l1_w_out,
              l1_ln1_g, l1_ln1_b, l1_ln2_g, l1_ln2_b, l1_v0, l1_v2,
              l1_router, l1_exp_w1, l1_exp_w3, l1_exp_w2):
    layers = [
        dict(w_ada=l0_w_ada, b_ada=l0_b_ada, w_in=l0_w_in, mu_prev=l0_mu_prev, mu_next=l0_mu_next,
             pool_w=l0_pool_w, pool_scale=l0_pool_scale, w0_f=l0_w0_f, w2_f=l0_w2_f,
             w0_b=l0_w0_b, w2_b=l0_w2_b, a0_f=l0_a0_f, a2_f=l0_a2_f, a0_b=l0_a0_b, a2_b=l0_a2_b,
             g_up=l0_g_up, k_k=l0_k_k, k_a=l0_k_a, r_k=l0_r_k, lnx_g=l0_lnx_g, lnx_b=l0_lnx_b,
             w_out=l0_w_out, ln1_g=l0_ln1_g, ln1_b=l0_ln1_b, ln2_g=l0_ln2_g, ln2_b=l0_ln2_b,
             ffn_w1=l0_ffn_w1, ffn_w3=l0_ffn_w3, ffn_w2=l0_ffn_w2),
        dict(w_ada=l1_w_ada, b_ada=l1_b_ada, w_in=l1_w_in, mu_prev=l1_mu_prev, mu_next=l1_mu_next,
             pool_w=l1_pool_w, pool_scale=l1_pool_scale, w0_f=l1_w0_f, w2_f=l1_w2_f,
             w0_b=l1_w0_b, w2_b=l1_w2_b, a0_f=l1_a0_f, a2_f=l1_a2_f, a0_b=l1_a0_b, a2_b=l1_a2_b,
             g_up=l1_g_up, k_k=l1_k_k, k_a=l1_k_a, r_k=l1_r_k, lnx_g=l1_lnx_g, lnx_b=l1_lnx_b,
             w_out=l1_w_out, ln1_g=l1_ln1_g, ln1_b=l1_ln1_b, ln2_g=l1_ln2_g, ln2_b=l1_ln2_b,
             v0=l1_v0, v2=l1_v2, router=l1_router, exp_w1=l1_exp_w1, exp_w3=l1_exp_w3,
             exp_w2=l1_exp_w2),
    ]
    xc = ctx
    v_first = None
    for i in range(DEPTH):
        p = layers[i]
        last = i == DEPTH - 1
        sh1, sc1, gate1, sh2, sc2, gate2 = [m[:, None, :] for m in _modulation(c, p['w_ada'], p['b_ada'])]
        csh1, csc1, cgate1, csh2, csc2, cgate2 = _modulation(c_ctx, p['w_ada'], p['b_ada'])
        mix_ctx, mix_lat, vf = _mixer(_modulate(xc, csh1, csc1), _modulate(x, sh1, sc1), p, v_first, not last)
        if v_first is None:
            v_first = vf
        x = _layer_norm(ALPHA * x + gate1 * mix_lat, p['ln1_g'], p['ln1_b'])
        x = _layer_norm(ALPHA * x + gate2 * _channel_mixer(_modulate(x, sh2, sc2), p, i),
                        p['ln2_g'], p['ln2_b'])
        if not last:
            xc = _layer_norm(ALPHA * xc + cgate1 * mix_ctx, p['ln1_g'], p['ln1_b'])
            xc = _layer_norm(ALPHA * xc + cgate2 * _channel_mixer(_modulate(xc, csh2, csc2), p, i),
                             p['ln2_g'], p['ln2_b'])
    return x
```

```python
import functools
import math

import numpy as np
import jax
import jax.numpy as jnp
from jax import lax
from jax.experimental import pallas as pl
from jax.experimental.pallas import tpu as pltpu

F32 = jnp.float32
BF16 = jnp.bfloat16

GRID_W = 64
POOL_WINDOWS = (2, 4, 8, 16)
HEAD = 64
LANES = 128
SUBLANES = 8
N_ADA = 6
LN_EPS = 1e-5
GN_EPS = 64e-5
NORM_EPS = 1e-12
CHUNK = 64
SCAN_PAIRS = 2
INV_PASSES = 1
VMEM_LIMIT = 56 * 1024 * 1024
COND_ROWS = 16


def _cparams(sem):
    return pltpu.CompilerParams(dimension_semantics=sem, vmem_limit_bytes=VMEM_LIMIT)


def _pick(n, target, mult):
    best = None
    for d in range(mult, min(n, target) + 1, mult):
        if n % d == 0:
            best = d
    return best if best is not None else n


def _dot(a, b):
    return jnp.dot(a, b, preferred_element_type=F32)


def _dot_nt(a, b):
    return lax.dot_general(a, b, (((1,), (1,)), ((), ())), preferred_element_type=F32)


def _split2(x):
    hi = x.astype(BF16)
    lo = (x - hi.astype(F32)).astype(BF16)
    return hi, lo


def _split3(x):
    hi = x.astype(BF16)
    r1 = x - hi.astype(F32)
    mid = r1.astype(BF16)
    lo = (r1 - mid.astype(F32)).astype(BF16)
    return hi, mid, lo


def _mm(a, b, passes=1):
    if passes == 1:
        return _dot(a.astype(BF16), b.astype(BF16))
    a_hi, a_lo = _split2(a)
    b_hi, b_lo = _split2(b)
    return _dot(a_hi, b_hi) + _dot(a_lo, b_hi) + _dot(a_hi, b_lo)


def _sigmoid(x):
    return 1.0 / (1.0 + jnp.exp(-x))


def _layer_norm_rows(x):
    mu = jnp.mean(x, axis=-1, keepdims=True)
    xc = x - mu
    var = jnp.mean(xc * xc, axis=-1, keepdims=True)
    return xc * lax.rsqrt(var + LN_EPS)


def _mod_kernel(c_ref, w_ref, b_ref, o_ref):
    c = c_ref[...]
    s = c * _sigmoid(c)
    o_ref[...] = _mm(s, w_ref[...], passes=3) + b_ref[...]


def _modulation(cond, w_ada, b_ada):
    d, n = w_ada.shape
    tn = _pick(n, 512, LANES)
    return pl.pallas_call(
        _mod_kernel,
        grid=(n // tn,),
        in_specs=[pl.BlockSpec((COND_ROWS, d), lambda j: (0, 0)),
                  pl.BlockSpec((d, tn), lambda j: (0, j)),
                  pl.BlockSpec((1, tn), lambda j: (0, j))],
        out_specs=pl.BlockSpec((COND_ROWS, tn), lambda j: (0, j)),
        out_shape=jax.ShapeDtypeStruct((COND_ROWS, n), F32),
        compiler_params=_cparams(("parallel",)),
        name="modulation",
    )(cond, w_ada, b_ada.reshape(1, n))


def _mod_row(mod_ref, row, chunk, d):
    return mod_ref[pl.ds(row, 1), chunk * d:(chunk + 1) * d]


def _ln_mod_kernel(x_ref, mod_ref, h_ref, *, nctx_blk, ctx_row, k_shift, k_scale):
    b, j = pl.program_id(0), pl.program_id(1)
    d = x_ref.shape[-1]
    row = jnp.where(j < nctx_blk, ctx_row, b)
    shift = _mod_row(mod_ref, row, k_shift, d)
    scale = _mod_row(mod_ref, row, k_scale, d)
    h_ref[...] = (_layer_norm_rows(x_ref[...]) * (1.0 + scale) + shift).astype(h_ref.dtype)


def _ln_mod(x_all, mod, *, lc, tr, ctx_row, k_shift, k_scale):
    bsz, t, d = x_all.shape
    kern = functools.partial(_ln_mod_kernel, nctx_blk=lc // tr, ctx_row=ctx_row,
                             k_shift=k_shift, k_scale=k_scale)
    return pl.pallas_call(
        kern,
        grid=(bsz, t // tr),
        in_specs=[pl.BlockSpec((None, tr, d), lambda b, j: (b, j, 0)),
                  pl.BlockSpec(mod.shape, lambda b, j: (0, 0))],
        out_specs=pl.BlockSpec((None, tr, d), lambda b, j: (b, j, 0)),
        out_shape=jax.ShapeDtypeStruct((bsz, t, d), BF16),
        compiler_params=_cparams(("parallel", "parallel")),
        name="ln_modulate",
    )(x_all, mod)


def _mm_kernel(a_ref, w_ref, o_ref, *scratch, nk):
    if nk == 1:
        o_ref[...] = _dot(a_ref[...], w_ref[...]).astype(o_ref.dtype)
        return
    acc_ref, = scratch
    k = pl.program_id(2)

    @pl.when(k == 0)
    def _():
        acc_ref[...] = jnp.zeros_like(acc_ref)

    acc_ref[...] += _dot(a_ref[...], w_ref[...])

    @pl.when(k == nk - 1)
    def _():
        o_ref[...] = acc_ref[...].astype(o_ref.dtype)


def _matmul(a, w, *, out_dtype, tm, tn, tk, name):
    m, kdim = a.shape
    n = w.shape[1]
    tm, tn, tk = _pick(m, tm, SUBLANES), _pick(n, tn, LANES), _pick(kdim, tk, LANES)
    nk = kdim // tk
    scratch = [pltpu.VMEM((tm, tn), F32)] if nk > 1 else []
    return pl.pallas_call(
        functools.partial(_mm_kernel, nk=nk),
        grid=(m // tm, n // tn, nk),
        in_specs=[pl.BlockSpec((tm, tk), lambda i, j, k: (i, k)),
                  pl.BlockSpec((tk, tn), lambda i, j, k: (k, j))],
        out_specs=pl.BlockSpec((tm, tn), lambda i, j, k: (i, j)),
        out_shape=jax.ShapeDtypeStruct((m, n), out_dtype),
        scratch_shapes=scratch,
        compiler_params=_cparams(("parallel", "parallel", "arbitrary")),
        name=name,
    )(a, w)


def _glu_kernel(a_ref, w1_ref, w3_ref, *rest, gated):
    a = a_ref[...]
    p1 = _dot(a, w1_ref[...])
    p3 = _dot(a, w3_ref[...])
    hid = p1 * _sigmoid(p1) * p3
    if gated:
        g_ref, o_ref = rest
        hid = hid * pltpu.repeat(g_ref[...], hid.shape[1] // LANES, axis=1)
    else:
        o_ref, = rest
    o_ref[...] = hid.astype(o_ref.dtype)


def _glu(a, w1, w3, gates, *, tm, tn, name):
    m, kdim = a.shape
    e, _, f = w1.shape
    tm, tn = _pick(m, tm, SUBLANES), _pick(f, tn, LANES)
    nj = f // tn
    in_specs = [pl.BlockSpec((tm, kdim), lambda i, ex, j: (i, 0)),
                pl.BlockSpec((None, kdim, tn), lambda i, ex, j: (ex, 0, j)),
                pl.BlockSpec((None, kdim, tn), lambda i, ex, j: (ex, 0, j))]
    args = [a, w1, w3]
    if gates is not None:
        in_specs.append(pl.BlockSpec((None, tm, LANES), lambda i, ex, j: (ex, i, 0)))
        args.append(gates)
    return pl.pallas_call(
        functools.partial(_glu_kernel, gated=gates is not None),
        grid=(m // tm, e, nj),
        in_specs=in_specs,
        out_specs=pl.BlockSpec((tm, tn), lambda i, ex, j: (i, ex * nj + j)),
        out_shape=jax.ShapeDtypeStruct((m, e * f), BF16),
        compiler_params=_cparams(("parallel", "parallel", "parallel")),
        name=name,
    )(*args)


def _pool_consts(lc, tl, grid_w, tb):
    rows = tl // grid_w
    cb, c1, inv = [], [], []
    for win in POOL_WINDOWS:
        half = win // 2
        t = np.arange(tb)
        same_row = (t[:, None] // grid_w) == (t[None, :] // grid_w)
        dc = (t[None, :] % grid_w) - (t[:, None] % grid_w)
        cb.append((same_row & (dc >= -half) & (dc < half)).astype(np.float32))
        tc = np.arange(lc)
        d1 = tc[None, :] - tc[:, None]
        c1.append(((d1 >= -half) & (d1 < half)).astype(np.float32))
        cnt1 = np.minimum(tc + half, lc) - np.maximum(tc - half, 0)
        g = np.arange(grid_w)
        cntc = np.minimum(g + half, grid_w) - np.maximum(g - half, 0)
        r = np.arange(rows)
        cntr = np.minimum(r + half, rows) - np.maximum(r - half, 0)
        cnt2 = (cntr[:, None] * cntc[None, :]).reshape(-1)
        iv = 1.0 / np.concatenate([cnt1, cnt2]).astype(np.float64)
        inv.append(np.broadcast_to(iv[:, None], (lc + tl, LANES)).astype(np.float32))
    return (jnp.asarray(np.stack(cb), BF16), jnp.asarray(np.stack(c1), BF16),
            jnp.asarray(np.stack(inv), F32))


def _pool_kernel(u_ref, cb_ref, c1_ref, inv_ref, pw_ref, ps_ref, o_ref, s1_ref, acc_ref,
                 *, lc, tb, grid_w, half):
    t = u_ref.shape[0]
    tl = t - lc
    gc_hi, gc_lo = _split2(u_ref[0:lc, :])
    c1 = c1_ref[...]
    acc_ref[0:lc, :] = _dot(c1, gc_hi) + _dot(c1, gc_lo)
    cb = cb_ref[...]
    for blk in range(tl // tb):
        lo, hi = lc + blk * tb, lc + (blk + 1) * tb
        g_hi, g_lo = _split2(u_ref[lo:hi, :])
        s1_ref[lo:hi, :] = _dot(cb, g_hi) + _dot(cb, g_lo)
    acc_ref[lc:t, :] = s1_ref[lc:t, :]
    for dr in range(-half, half):
        if dr == 0:
            continue
        sh = abs(dr) * grid_w
        if sh >= tl:
            continue
        if dr > 0:
            acc_ref[lc:t - sh, :] += s1_ref[lc + sh:t, :]
        else:
            acc_ref[lc + sh:t, :] += s1_ref[lc:t - sh, :]
    n_rep = u_ref.shape[1] // LANES
    diff = acc_ref[...] * pltpu.repeat(inv_ref[...], n_rep, axis=1) - u_ref[...]
    o_ref[...] = (_dot(diff.astype(BF16), pw_ref[...]) * ps_ref[...]).astype(o_ref.dtype)


def _pool_group(u_all, consts, pool_w, pool_scale, idx, *, lc, grid_w, tb, pg):
    bsz, t, _ = u_all.shape
    cb, c1, inv = consts
    half = POOL_WINDOWS[idx] // 2
    kern = functools.partial(_pool_kernel, lc=lc, tb=tb, grid_w=grid_w, half=half)
    return pl.pallas_call(
        kern,
        grid=(bsz,),
        in_specs=[pl.BlockSpec((None, t, pg), lambda b: (b, 0, idx)),
                  pl.BlockSpec((None, tb, tb), lambda b: (idx, 0, 0)),
                  pl.BlockSpec((None, lc, lc), lambda b: (idx, 0, 0)),
                  pl.BlockSpec((None, t, LANES), lambda b: (idx, 0, 0)),
                  pl.BlockSpec((None, pg, pg), lambda b: (idx, 0, 0)),
                  pl.BlockSpec((1, pg), lambda b: (0, idx))],
        out_specs=pl.BlockSpec((None, t, pg), lambda b: (b, 0, 0)),
        out_shape=jax.ShapeDtypeStruct((bsz, t, pg), BF16),
        scratch_shapes=[pltpu.VMEM((t, pg), F32), pltpu.VMEM((t, pg), F32)],
        compiler_params=_cparams(("parallel",)),
        name="pool_mixer_%d" % idx,
    )(u_all, cb, c1, inv, pool_w, pool_scale)


def _token_shift(f, prev8, next8, mu_p, mu_n, seg_first, seg_last):
    tr = f.shape[0]
    row = lax.broadcasted_iota(jnp.int32, f.shape, 0)
    prev_row = jnp.where(seg_first, 0.0, prev8[SUBLANES - 1:SUBLANES, :])
    next_row = jnp.where(seg_last, 0.0, next8[0:1, :])
    prev = jnp.where(row == 0, prev_row, pltpu.roll(f, 1, 0))
    nxt = jnp.where(row == tr - 1, next_row, pltpu.roll(f, tr - 1, 0))
    return f + mu_p * (prev - f) + mu_n * (nxt - f)


def _segment_flags(j, nctx_blk, n_blk):
    seg_first = jnp.logical_or(j == 0, j == nctx_blk)
    seg_last = jnp.logical_or(j == nctx_blk - 1, j == n_blk - 1)
    return seg_first, seg_last


def _shift_specs(tr, tc, col_blk, t):
    r8 = tr // SUBLANES
    last8 = t // SUBLANES - 1
    cur = pl.BlockSpec((None, tr, tc), lambda b, j, c: (b, j, col_blk(c)))
    prv = pl.BlockSpec((None, SUBLANES, tc),
                       lambda b, j, c: (b, jnp.maximum(j * r8 - 1, 0), col_blk(c)))
    nxt = pl.BlockSpec((None, SUBLANES, tc),
                       lambda b, j, c: (b, jnp.minimum((j + 1) * r8, last8), col_blk(c)))
    return [cur, prv, nxt]


def _lora_kernel(u_ref, up_ref, un_ref, mu_ref, o_ref, *, nctx_blk, n_blk, lr, gl):
    j = pl.program_id(1)
    seg_first, seg_last = _segment_flags(j, nctx_blk, n_blk)
    z = _token_shift(u_ref[...], up_ref[...], un_ref[...], mu_ref[0:1, :], mu_ref[1:2, :],
                     seg_first, seg_last)
    col = lax.broadcasted_iota(jnp.int32, z.shape, 1)
    act = jnp.where(col < 2 * lr, jnp.tanh(z),
                    jnp.where(jnp.logical_and(col >= 4 * lr, col < 4 * lr + gl), _sigmoid(z), z))
    o_ref[...] = act.astype(o_ref.dtype)


def _lora_act(u_all, mu_lora, *, lc, tr, col0, wl, lr, gl):
    bsz, t, _ = u_all.shape
    specs = _shift_specs(tr, wl, lambda c: col0 // wl, t)
    kern = functools.partial(_lora_kernel, nctx_blk=lc // tr, n_blk=t // tr, lr=lr, gl=gl)
    return pl.pallas_call(
        kern,
        grid=(bsz, t // tr, 1),
        in_specs=specs + [pl.BlockSpec((SUBLANES, wl), lambda b, j, c: (0, 0))],
        out_specs=pl.BlockSpec((None, tr, wl), lambda b, j, c: (b, j, 0)),
        out_shape=jax.ShapeDtypeStruct((bsz, t, wl), BF16),
        compiler_params=_cparams(("parallel", "parallel", "arbitrary")),
        name="lora_act",
    )(u_all, u_all, u_all, mu_lora)


def _head_ones():
    r = lax.broadcasted_iota(jnp.int32, (LANES, LANES), 0)
    c = lax.broadcasted_iota(jnp.int32, (LANES, LANES), 1)
    return jnp.where((r ^ c) < HEAD, 1.0, 0.0).astype(BF16)


def _head_sum(x, ones):
    outs = []
    for s in range(x.shape[1] // LANES):
        hi, mid, lo = _split3(x[:, s * LANES:(s + 1) * LANES])
        outs.append(_dot(hi, ones) + _dot(mid, ones) + _dot(lo, ones))
    return outs[0] if len(outs) == 1 else jnp.concatenate(outs, axis=1)


(_V_MPR, _V_MNR, _V_MPK, _V_MNK, _V_MPV, _V_MNV, _V_W0F, _V_W0B, _V_A0F, _V_A0B,
 _V_V0, _V_KK, _V_KA, _V_RK, _V_LG, _V_LB) = range(16)
_DECAY_SCALE = math.exp(-0.5)


def _terms_kernel(*refs, nctx_blk, n_blk, lr, gl, has_vres):
    (ur, urp, urn, uk, ukp, ukn, uv, uvp, uvn, act_ref, vec_ref,
     w2f_ref, w2b_ref, a2f_ref, a2b_ref, gup_ref) = refs[:16]
    pos = 16
    if has_vres:
        v2_ref, vfirst_ref = refs[pos:pos + 2]
        pos += 2
    (r_o, v_o, kk_o, lwf_o, lwb_o, kf_o, kb_o, af_o, ab_o, gate_o, bonus_o) = refs[pos:]

    j = pl.program_id(1)
    seg_first, seg_last = _segment_flags(j, nctx_blk, n_blk)
    vec = lambda i: vec_ref[i:i + 1, :]
    r = _token_shift(ur[...], urp[...], urn[...], vec(_V_MPR), vec(_V_MNR), seg_first, seg_last)
    k = _token_shift(uk[...], ukp[...], ukn[...], vec(_V_MPK), vec(_V_MNK), seg_first, seg_last)
    v = _token_shift(uv[...], uvp[...], uvn[...], vec(_V_MPV), vec(_V_MNV), seg_first, seg_last)

    act = act_ref[...]
    a_wf, a_wb = act[:, 0:lr], act[:, lr:2 * lr]
    a_af, a_ab = act[:, 2 * lr:3 * lr], act[:, 3 * lr:4 * lr]
    a_g = act[:, 4 * lr:4 * lr + gl]
    if has_vres:
        a_v = act[:, 4 * lr + gl:5 * lr + gl]
        v = v + (vfirst_ref[...] - v) * _sigmoid(vec(_V_V0) + _dot(a_v, v2_ref[...]))

    lwf_o[...] = -_DECAY_SCALE * _sigmoid(vec(_V_W0F) + _dot(a_wf, w2f_ref[...]))
    lwb_o[...] = -_DECAY_SCALE * _sigmoid(vec(_V_W0B) + _dot(a_wb, w2b_ref[...]))
    a_f = _sigmoid(vec(_V_A0F) + _dot(a_af, a2f_ref[...]))
    a_b = _sigmoid(vec(_V_A0B) + _dot(a_ab, a2b_ref[...]))

    ones = _head_ones()
    kkr = k * vec(_V_KK)
    norm = jnp.sqrt(_head_sum(kkr * kkr, ones))
    kk_o[...] = kkr / jnp.maximum(norm, NORM_EPS)
    k_f = k * (1.0 + (a_f - 1.0) * vec(_V_KA))
    k_b = k * (1.0 + (a_b - 1.0) * vec(_V_KA))
    bonus_o[...] = _head_sum(r * (k_f + k_b) * vec(_V_RK), ones) * v
    gate_o[...] = _dot(a_g, gup_ref[...])
    r_o[...] = r
    v_o[...] = v
    kf_o[...] = k_f
    kb_o[...] = k_b
    af_o[...] = a_f
    ab_o[...] = a_b


def _rwkv_terms(u_all, act, vec, lw, v_first, *, lc, tr, tc, pool_w, rw, lr, gl):
    bsz, t, _ = u_all.shape
    has_vres = v_first is not None
    nb = rw // tc
    specs = []
    for part in range(3):
        base = (pool_w + part * rw) // tc
        specs += _shift_specs(tr, tc, (lambda c, base=base: base + c), t)
    specs.append(pl.BlockSpec((None, tr, act.shape[-1]), lambda b, j, c: (b, j, 0)))
    specs.append(pl.BlockSpec((vec.shape[0], tc), lambda b, j, c: (0, c)))
    wspec = lambda rows: pl.BlockSpec((rows, tc), lambda b, j, c: (0, c))
    specs += [wspec(lr), wspec(lr), wspec(lr), wspec(lr), wspec(gl)]
    args = [u_all] * 9 + [act, vec, lw["w2_f"], lw["w2_b"], lw["a2_f"], lw["a2_b"], lw["g_up"]]
    if has_vres:
        specs += [wspec(lr), pl.BlockSpec((None, tr, tc), lambda b, j, c: (b, j, c))]
        args += [lw["v2"], v_first]
    out_spec = pl.BlockSpec((None, tr, tc), lambda b, j, c: (b, j, c))
    out_shape = jax.ShapeDtypeStruct((bsz, t, rw), F32)
    kern = functools.partial(_terms_kernel, nctx_blk=lc // tr, n_blk=t // tr, lr=lr, gl=gl,
                             has_vres=has_vres)
    names = ("r", "v", "kk", "lw_f", "lw_b", "k_f", "k_b", "a_f", "a_b", "gate", "bonus")
    outs = pl.pallas_call(
        kern,
        grid=(bsz, t // tr, nb),
        in_specs=specs,
        out_specs=[out_spec] * len(names),
        out_shape=[out_shape] * len(names),
        compiler_params=_cparams(("parallel", "parallel", "parallel")),
        name="rwkv_terms",
    )(*args)
    return dict(zip(names, outs))


def _cumsum_rows(x, rev):
    n = x.shape[0]
    row = lax.broadcasted_iota(jnp.int32, x.shape, 0)
    s = 1
    while s < n:
        if rev:
            x = x + jnp.where(row < n - s, pltpu.roll(x, n - s, 0), 0.0)
        else:
            x = x + jnp.where(row >= s, pltpu.roll(x, s, 0), 0.0)
        s *= 2
    return x


def _wkv_chunk(r, v, kk, lw, k, a, state, rev):
    n = r.shape[0]
    hp = 2 * n
    lane = lax.broadcasted_iota(jnp.int32, (n, LANES), 1)
    first_head = lane < HEAD
    c = _cumsum_rows(lw, rev)
    ctot = c[0:1, :] if rev else c[n - 1:n, :]
    e_pos, e_neg = jnp.exp(c), jnp.exp(-c)
    e_prev, e_rem = jnp.exp(c - lw), jnp.exp(ctot - c)
    kka = kk * a
    at, bt, kt, rt = -kk * e_prev, kka * e_neg, k * e_neg, r * e_pos
    btw, ktw = kka * e_rem, k * e_rem

    def stack(x):
        return jnp.concatenate([jnp.where(first_head, x, 0.0), jnp.where(first_head, 0.0, x)], axis=0)

    s_at, s_rt, s_v = stack(at), stack(rt), stack(v)
    lhs1 = jnp.concatenate([s_at, s_rt], axis=0).astype(BF16)
    rhs1 = jnp.concatenate([bt, bt, kt, kt], axis=0).astype(BF16)
    s1 = _dot_nt(lhs1, rhs1)

    ri = lax.broadcasted_iota(jnp.int32, (hp, hp), 0)
    ci = lax.broadcasted_iota(jnp.int32, (hp, hp), 1)
    ti, tj = ri & (n - 1), ci & (n - 1)
    dist = jnp.where((ri ^ ci) < n, (ti - tj) if rev else (tj - ti), hp)
    strict, incl = dist < 0, dist <= 0
    eye = ri == ci
    nmat = jnp.where(strict, s1[:hp, :hp], 0.0)
    m_ak = jnp.where(strict, s1[:hp, hp:], 0.0)
    m_rb = jnp.where(incl, s1[hp:, :hp], 0.0)
    m_rk = jnp.where(incl, s1[hp:, hp:], 0.0)

    blk = ri ^ ci
    n8 = jnp.where(blk < SUBLANES, nmat, 0.0)
    tmat = jnp.where(eye, 1.0, n8)
    n2 = _mm(n8, n8, INV_PASSES)
    st = _mm(jnp.concatenate([tmat, n2], axis=0), n2, INV_PASSES)
    tmat = tmat + st[:hp]
    tmat = tmat + _mm(tmat, st[hp:], INV_PASSES)
    size = SUBLANES
    while size < n:
        off = jnp.where(jnp.logical_and(blk >= size, blk < 2 * size), nmat, 0.0)
        tmat = tmat + _mm(tmat, _mm(off, tmat, INV_PASSES), INV_PASSES)
        size *= 2

    z = _mm(m_ak, s_v)
    gu = _mm(tmat, jnp.concatenate([s_at, z], axis=1))
    rhs4 = jnp.concatenate(
        [gu, jnp.concatenate([jnp.zeros((hp, LANES), F32), s_v], axis=1)], axis=0).astype(BF16)
    top = _dot(jnp.concatenate([m_rb, m_rk], axis=1).astype(BF16), rhs4)
    lhs_t = jnp.concatenate([stack(btw), stack(ktw)], axis=0).T
    bot = _dot(lhs_t.astype(BF16), rhs4)

    q = rt + top[:n, :LANES] + top[n:, :LANES]
    y0 = top[:n, LANES:] + top[n:, LANES:]
    amat = jnp.where(eye, jnp.broadcast_to(jnp.exp(ctot), (LANES, LANES)), 0.0) + bot[:, :LANES]
    cmat = bot[:, LANES:]

    qa = jnp.concatenate([q, amat], axis=0).astype(BF16)
    s_hi, s_lo = _split2(state)
    out = _dot(qa, s_hi) + _dot(qa, s_lo)
    return out[:n] + y0, out[n:] + cmat


def _scan_kernel(*refs, pairs):
    ins, (yf_ref, yb_ref, state_ref) = refs[:12], refs[12:]
    s = pl.program_id(1)

    @pl.when(s == 0)
    def _():
        state_ref[...] = jnp.zeros_like(state_ref)

    for d, y_ref in ((0, yf_ref), (1, yb_ref)):
        for p in range(pairs):
            cols = slice(p * LANES, (p + 1) * LANES)
            r, v, kk, lw, k, a = (ref[:, cols] for ref in ins[6 * d:6 * d + 6])
            y, new_state = _wkv_chunk(r, v, kk, lw, k, a, state_ref[d, p], rev=bool(d))
            y_ref[:, cols] = y
            state_ref[d, p] = new_state


def _wkv_scan(tm, *, lc, chunk, pairs):
    bsz, t, rw = tm["r"].shape
    width = pairs * LANES
    ngrp = rw // width
    nctx, ntot = lc // chunk, t // chunk

    def fwd(g, s):
        return (g // ngrp, s, g % ngrp)

    def bwd(g, s):
        return (g // ngrp, jnp.where(s < nctx, nctx - 1 - s, ntot - 1 - (s - nctx)), g % ngrp)

    blk = (None, chunk, width)
    names_f = ("r", "v", "kk", "lw_f", "k_f", "a_f")
    names_b = ("r", "v", "kk", "lw_b", "k_b", "a_b")
    in_specs = [pl.BlockSpec(blk, fwd)] * 6 + [pl.BlockSpec(blk, bwd)] * 6
    args = [tm[n] for n in names_f] + [tm[n] for n in names_b]
    return pl.pallas_call(
        functools.partial(_scan_kernel, pairs=pairs),
        grid=(bsz * ngrp, ntot),
        in_specs=in_specs,
        out_specs=[pl.BlockSpec(blk, fwd), pl.BlockSpec(blk, bwd)],
        out_shape=[jax.ShapeDtypeStruct((bsz, t, rw), F32)] * 2,
        scratch_shapes=[pltpu.VMEM((2, pairs, LANES, LANES), F32)],
        compiler_params=_cparams(("parallel", "arbitrary")),
        name="wkv7_scan",
    )(*args)


def _rwkv_out_kernel(yf_ref, yb_ref, bonus_ref, gate_ref, vec_ref, o_ref):
    y = yf_ref[...] + yb_ref[...]
    ones = _head_ones()
    inv_n = 1.0 / HEAD
    mu = _head_sum(y, ones) * inv_n
    yc = y - mu
    var = _head_sum(yc * yc, ones) * inv_n
    yn = yc * lax.rsqrt(var + GN_EPS)
    yn = yn * vec_ref[_V_LG:_V_LG + 1, :] + vec_ref[_V_LB:_V_LB + 1, :]
    o_ref[...] = ((yn + bonus_ref[...]) * gate_ref[...]).astype(o_ref.dtype)


def _rwkv_out(y_f, y_b, bonus, gate, vec, *, tr, tc):
    bsz, t, rw = y_f.shape
    spec = pl.BlockSpec((None, tr, tc), lambda b, j, c: (b, j, c))
    return pl.pallas_call(
        _rwkv_out_kernel,
        grid=(bsz, t // tr, rw // tc),
        in_specs=[spec] * 4 + [pl.BlockSpec((vec.shape[0], tc), lambda b, j, c: (0, c))],
        out_specs=spec,
        out_shape=jax.ShapeDtypeStruct((bsz, t, rw), BF16),
        compiler_params=_cparams(("parallel", "parallel", "parallel")),
        name="rwkv_out",
    )(y_f, y_b, bonus, gate, vec)


def _res_kernel(*refs, alpha, row_off_blk, nctx_blk, ctx_row, k_gate, k_shift, k_scale,
                with_mod, n_exp):
    x_ref, y_ref, mod_ref, gb_ref = refs[:4]
    pos = 4
    if n_exp:
        router_ref = refs[pos]
        pos += 1
    outs = refs[pos:]
    b, j = pl.program_id(0), pl.program_id(1)
    d = x_ref.shape[-1]
    row = jnp.where(j + row_off_blk < nctx_blk, ctx_row, b)
    gate = _mod_row(mod_ref, row, k_gate, d)
    xn = _layer_norm_rows(alpha * x_ref[...] + gate * y_ref[...]) * gb_ref[0:1, :] + gb_ref[1:2, :]
    outs[0][...] = xn
    if not with_mod:
        return
    h = _layer_norm_rows(xn) * (1.0 + _mod_row(mod_ref, row, k_scale, d)) + _mod_row(mod_ref, row, k_shift, d)
    outs[1][...] = h.astype(outs[1].dtype)
    if not n_exp:
        return
    logits = _mm(h, router_ref[...], passes=3)
    lane = lax.broadcasted_iota(jnp.int32, logits.shape, 1).astype(F32)
    neg = -jnp.inf
    logits = jnp.where(lane < n_exp, logits, neg)
    m1 = jnp.max(logits, axis=-1, keepdims=True)
    i1 = jnp.min(jnp.where(logits == m1, lane, float(LANES)), axis=-1, keepdims=True)
    rest = jnp.where(lane == i1, neg, logits)
    m2 = jnp.max(rest, axis=-1, keepdims=True)
    i2 = jnp.min(jnp.where(rest == m2, lane, float(LANES)), axis=-1, keepdims=True)
    e2 = jnp.exp(m2 - m1)
    g1 = 1.0 / (1.0 + e2)
    g2 = e2 / (1.0 + e2)
    g_ref = outs[2]
    for e in range(n_exp):
        ge = jnp.where(i1 == e, g1, 0.0) + jnp.where(i2 == e, g2, 0.0)
        g_ref[e] = jnp.broadcast_to(ge, (ge.shape[0], LANES))


def _residual_ln(x, y, mod, gain, bias, *, alpha, tr, lc, ctx_row, k_gate, x_off=0, y_off=0,
                 rows=None, mod2=None, router=None):
    bsz, _, d = x.shape
    rows = x.shape[1] if rows is None else rows
    n_exp = 0 if router is None else router.shape[1]
    gb = jnp.stack([gain, bias])
    in_specs = [pl.BlockSpec((None, tr, d), lambda b, j: (b, j + x_off // tr, 0)),
                pl.BlockSpec((None, tr, d), lambda b, j: (b, j + y_off // tr, 0)),
                pl.BlockSpec(mod.shape, lambda b, j: (0, 0)),
                pl.BlockSpec((2, d), lambda b, j: (0, 0))]
    args = [x, y, mod, gb]
    out_specs = [pl.BlockSpec((None, tr, d), lambda b, j: (b, j, 0))]
    out_shape = [jax.ShapeDtypeStruct((bsz, rows, d), F32)]
    if mod2 is not None:
        out_specs.append(pl.BlockSpec((None, tr, d), lambda b, j: (b, j, 0)))
        out_shape.append(jax.ShapeDtypeStruct((bsz, rows, d), BF16))
    if router is not None:
        rpad = jnp.zeros((d, LANES), F32).at[:, :n_exp].set(router)
        in_specs.append(pl.BlockSpec((d, LANES), lambda b, j: (0, 0)))
        args.append(rpad)
        out_specs.append(pl.BlockSpec((n_exp, None, tr, LANES), lambda b, j: (0, b, j, 0)))
        out_shape.append(jax.ShapeDtypeStruct((n_exp, bsz, rows, LANES), F32))
    k_shift, k_scale = mod2 if mod2 is not None else (0, 0)
    kern = functools.partial(_res_kernel, alpha=alpha, row_off_blk=x_off // tr, nctx_blk=lc // tr,
                             ctx_row=ctx_row, k_gate=k_gate, k_shift=k_shift, k_scale=k_scale,
                             with_mod=mod2 is not None, n_exp=n_exp)
    return pl.pallas_call(
        kern,
        grid=(bsz, rows // tr),
        in_specs=in_specs,
        out_specs=out_specs,
        out_shape=out_shape,
        compiler_params=_cparams(("parallel", "parallel")),
        name="residual_ln",
    )(*args)


def _pad_rows(w, rows):
    return jnp.zeros((rows,) + w.shape[1:], w.dtype).at[:w.shape[0]].set(w)


def _layer_layout(p, pool_w, rw):
    lr_raw = p["w2_f"].shape[0]
    gl_raw = p["g_up"].shape[0]
    vr_raw = p["v2"].shape[0] if "v2" in p else 0
    lr = -(-max(lr_raw, vr_raw, 1) // LANES) * LANES
    gl = -(-gl_raw // LANES) * LANES
    wl = 5 * lr + gl
    wl_pad = -(-wl // 1024) * 1024 if wl > 512 else wl
    core = pool_w + 3 * rw
    sizes = [lr_raw] * 4 + [gl_raw] + ([vr_raw] if vr_raw else [])
    slots = [lr] * 4 + [gl] + [lr]
    d = p["w_in"].shape[0]

    def relayout(src, rows_shape):
        dst = jnp.zeros(rows_shape + (core + wl_pad,), src.dtype)
        dst = dst.at[..., :core].set(src[..., :core])
        s_off, d_off = core, core
        for size, slot in zip(sizes, slots):
            dst = dst.at[..., d_off:d_off + size].set(src[..., s_off:s_off + size])
            s_off += size
            d_off += slot
        return dst

    w_in = relayout(p["w_in"], (d,)).astype(BF16)
    zero_pool = jnp.zeros((pool_w,), F32)
    mu_p = relayout(jnp.concatenate([zero_pool, p["mu_prev"]]), ())
    mu_n = relayout(jnp.concatenate([zero_pool, p["mu_next"]]), ())
    mu_lora = jnp.zeros((SUBLANES, wl_pad), F32).at[0].set(mu_p[core:]).at[1].set(mu_n[core:])
    seg = lambda a, i: a[pool_w + i * rw:pool_w + (i + 1) * rw]
    zeros = jnp.zeros((rw,), F32)
    vec = jnp.stack([seg(mu_p, 0), seg(mu_n, 0), seg(mu_p, 1), seg(mu_n, 1), seg(mu_p, 2), seg(mu_n, 2),
                     p["w0_f"], p["w0_b"], p["a0_f"], p["a0_b"], p.get("v0", zeros),
                     p["k_k"], p["k_a"], p["r_k"].reshape(-1), p["lnx_g"], p["lnx_b"]])
    lw = {n: _pad_rows(p[n], lr).astype(BF16) for n in ("w2_f", "w2_b", "a2_f", "a2_b")}
    lw["g_up"] = _pad_rows(p["g_up"], gl).astype(BF16)
    if vr_raw:
        lw["v2"] = _pad_rows(p["v2"], lr).astype(BF16)
    return dict(w_in=w_in, mu_lora=mu_lora, vec=vec, lw=lw, lr=lr, gl=gl, wl=wl_pad, core=core)


def _mixer(x_all, mod, p, lay, v_first, pool_consts, *, lc, tr, tc, grid_w, tb, ctx_row):
    bsz, t, d = x_all.shape
    mix_w = p["w_out"].shape[0]
    pool_w = mix_w // 4
    pg = pool_w // len(POOL_WINDOWS)
    rw = mix_w - pool_w
    h = _ln_mod(x_all, mod, lc=lc, tr=tr, ctx_row=ctx_row, k_shift=0, k_scale=1)
    u = _matmul(h.reshape(bsz * t, d), lay["w_in"], out_dtype=F32, tm=1024, tn=1024, tk=d,
                name="in_proj").reshape(bsz, t, -1)
    pool_wts = p["pool_w"].astype(BF16)
    pool_scale = p["pool_scale"].reshape(1, pool_w)
    pools = [_pool_group(u, pool_consts, pool_wts, pool_scale, i, lc=lc, grid_w=grid_w, tb=tb, pg=pg)
             for i in range(len(POOL_WINDOWS))]
    act = _lora_act(u, lay["mu_lora"], lc=lc, tr=tr, col0=lay["core"], wl=lay["wl"], lr=lay["lr"],
                    gl=lay["gl"])
    tm = _rwkv_terms(u, act, lay["vec"], lay["lw"], v_first, lc=lc, tr=tr, tc=tc, pool_w=pool_w,
                     rw=rw, lr=lay["lr"], gl=lay["gl"])
    y_f, y_b = _wkv_scan(tm, lc=lc, chunk=CHUNK, pairs=SCAN_PAIRS)
    out = _rwkv_out(y_f, y_b, tm["bonus"], tm["gate"], lay["vec"], tr=tr, tc=tc)
    mix_in = jnp.concatenate(pools + [out], axis=-1)
    mix = _matmul(mix_in.reshape(bsz * t, mix_w), p["w_out"].astype(BF16), out_dtype=F32,
                  tm=1024, tn=1024, tk=mix_w, name="out_proj").reshape(bsz, t, d)
    return mix, tm["v"]


def _forward(x, c, ctx, c_ctx, layers, grid_w):
    bsz, seq, d = x.shape
    lc = ctx.shape[1]
    depth = len(layers)
    alpha = (2 * depth) ** 0.25
    assert bsz < COND_ROWS and lc % CHUNK == 0 and seq % CHUNK == 0 and seq % grid_w == 0
    tr = _pick(math.gcd(lc, seq), 256, SUBLANES)
    tb = tr if tr % grid_w == 0 else grid_w
    assert seq % tb == 0 and tb % grid_w == 0
    ctx_row = bsz
    cond = jnp.zeros((COND_ROWS, d), F32).at[:bsz].set(c).at[ctx_row].set(c_ctx)
    x_all = jnp.concatenate([ctx, x], axis=1)
    pool_consts = _pool_consts(lc, seq, grid_w, tb)
    common = dict(tr=tr, lc=lc, ctx_row=ctx_row)
    v_first = None
    for i, p in enumerate(layers):
        last = i == depth - 1
        mix_w = p["w_out"].shape[0]
        pool_w = mix_w // 4
        rw = mix_w - pool_w
        assert pool_w % (len(POOL_WINDOWS) * LANES) == 0 and rw % (SCAN_PAIRS * LANES) == 0
        tc = _pick(rw, 256, SCAN_PAIRS * LANES)
        lay = _layer_layout(p, pool_w, rw)
        mod = _modulation(cond, p["w_ada"], p["b_ada"])
        mix, v_cur = _mixer(x_all, mod, p, lay, v_first, pool_consts, tc=tc, grid_w=grid_w, tb=tb,
                            **common)
        if v_first is None:
            v_first = v_cur
        if not last:
            x1, h2 = _residual_ln(x_all, mix, mod, p["ln1_g"], p["ln1_b"], alpha=alpha, k_gate=2,
                                  mod2=(3, 4), **common)
            t = x_all.shape[1]
            hid = _glu(h2.reshape(bsz * t, d), p["ffn_w1"].astype(BF16)[None], p["ffn_w3"].astype(BF16)[None],
                       None, tm=1024, tn=256, name="ffn_glu")
            ffn = _matmul(hid, p["ffn_w2"].astype(BF16), out_dtype=F32, tm=512, tn=1024, tk=5504,
                          name="ffn_down").reshape(bsz, t, d)
            x_all, = _residual_ln(x1, ffn, mod, p["ln2_g"], p["ln2_b"], alpha=alpha, k_gate=5, **common)
        else:
            x1, h2, gates = _residual_ln(x_all, mix, mod, p["ln1_g"], p["ln1_b"], alpha=alpha, k_gate=2,
                                         mod2=(3, 4), router=p["router"], x_off=lc, y_off=lc, rows=seq,
                                         **common)
            n_exp = p["router"].shape[1]
            gates = gates.reshape(n_exp, bsz * seq, LANES)
            hid = _glu(h2.reshape(bsz * seq, d), p["exp_w1"].astype(BF16), p["exp_w3"].astype(BF16),
                       gates, tm=1024, tn=512, name="moe_glu")
            w2 = p["exp_w2"].astype(BF16).reshape(-1, d)
            moe = _matmul(hid, w2, out_dtype=F32, tm=1024, tn=1024, tk=2048,
                          name="moe_down").reshape(bsz, seq, d)
            x_out, = _residual_ln(x1, moe, mod, p["ln2_g"], p["ln2_b"], alpha=alpha, k_gate=5,
                                  tr=tr, lc=0, ctx_row=ctx_row)
            return x_out
    return x_all[:, lc:]


_LAYER0 = ("w_ada", "b_ada", "w_in", "mu_prev", "mu_next", "pool_w", "pool_scale", "w0_f", "w2_f",
           "w0_b", "w2_b", "a0_f", "a2_f", "a0_b", "a2_b", "g_up", "k_k", "k_a", "r_k", "lnx_g",
           "lnx_b", "w_out", "ln1_g", "ln1_b", "ln2_g", "ln2_b", "ffn_w1", "ffn_w3", "ffn_w2")
_LAYER1 = _LAYER0[:26] + ("v0", "v2", "router", "exp_w1", "exp_w3", "exp_w2")


def kernel(x, c, ctx, c_ctx, l0_w_ada, l0_b_ada, l0_w_in, l0_mu_prev, l0_mu_next, l0_pool_w, l0_pool_scale, l0_w0_f, l0_w2_f, l0_w0_b, l0_w2_b, l0_a0_f, l0_a2_f, l0_a0_b, l0_a2_b, l0_g_up, l0_k_k, l0_k_a, l0_r_k, l0_lnx_g, l0_lnx_b, l0_w_out, l0_ln1_g, l0_ln1_b, l0_ln2_g, l0_ln2_b, l0_ffn_w1, l0_ffn_w3, l0_ffn_w2, l1_w_ada, l1_b_ada, l1_w_in, l1_mu_prev, l1_mu_next, l1_pool_w, l1_pool_scale, l1_w0_f, l1_w2_f, l1_w0_b, l1_w2_b, l1_a0_f, l1_a2_f, l1_a0_b, l1_a2_b, l1_g_up, l1_k_k, l1_k_a, l1_r_k, l1_lnx_g, l1_lnx_b, l1_w_out, l1_ln1_g, l1_ln1_b, l1_ln2_g, l1_ln2_b, l1_v0, l1_v2, l1_router, l1_exp_w1, l1_exp_w3, l1_exp_w2):
    l0 = dict(zip(_LAYER0, (l0_w_ada, l0_b_ada, l0_w_in, l0_mu_prev, l0_mu_next, l0_pool_w, l0_pool_scale, l0_w0_f, l0_w2_f, l0_w0_b, l0_w2_b, l0_a0_f, l0_a2_f, l0_a0_b, l0_a2_b, l0_g_up, l0_k_k, l0_k_a, l0_r_k, l0_lnx_g, l0_lnx_b, l0_w_out, l0_ln1_g, l0_ln1_b, l0_ln2_g, l0_ln2_b, l0_ffn_w1, l0_ffn_w3, l0_ffn_w2)))
    l1 = dict(zip(_LAYER1, (l1_w_ada, l1_b_ada, l1_w_in, l1_mu_prev, l1_mu_next, l1_pool_w, l1_pool_scale, l1_w0_f, l1_w2_f, l1_w0_b, l1_w2_b, l1_a0_f, l1_a2_f, l1_a0_b, l1_a2_b, l1_g_up, l1_k_k, l1_k_a, l1_r_k, l1_lnx_g, l1_lnx_b, l1_w_out, l1_ln1_g, l1_ln1_b, l1_ln2_g, l1_ln2_b, l1_v0, l1_v2, l1_router, l1_exp_w1, l1_exp_w3, l1_exp_w2)))
    return _forward(x, c, ctx, c_ctx, [l0, l1], GRID_W)
```

```python
import functools
import math

import numpy as np
import jax
import jax.numpy as jnp
from jax import lax
from jax.experimental import pallas as pl
from jax.experimental.pallas import tpu as pltpu

F32 = jnp.float32
BF16 = jnp.bfloat16

GRID_W = 64
POOL_WINDOWS = (2, 4, 8, 16)
HEAD = 64
LANES = 128
SUBLANES = 8
N_ADA = 6
LN_EPS = 1e-5
GN_EPS = 64e-5
NORM_EPS = 1e-12
CHUNK = 64
SCAN_PAIRS = 4
INV_PASSES = 1
VMEM_LIMIT = 56 * 1024 * 1024
COND_ROWS = 16


def _cparams(sem):
    return pltpu.CompilerParams(dimension_semantics=sem, vmem_limit_bytes=VMEM_LIMIT)


def _pick(n, target, mult):
    best = None
    for d in range(mult, min(n, target) + 1, mult):
        if n % d == 0:
            best = d
    return best if best is not None else n


def _dot(a, b):
    return jnp.dot(a, b, preferred_element_type=F32)


def _dot_nt(a, b):
    return lax.dot_general(a, b, (((1,), (1,)), ((), ())), preferred_element_type=F32)


def _split2(x):
    hi = x.astype(BF16)
    lo = (x - hi.astype(F32)).astype(BF16)
    return hi, lo


def _split3(x):
    hi = x.astype(BF16)
    r1 = x - hi.astype(F32)
    mid = r1.astype(BF16)
    lo = (r1 - mid.astype(F32)).astype(BF16)
    return hi, mid, lo


def _mm(a, b, passes=1):
    if passes == 1:
        return _dot(a.astype(BF16), b.astype(BF16))
    a_hi, a_lo = _split2(a)
    b_hi, b_lo = _split2(b)
    return _dot(a_hi, b_hi) + _dot(a_lo, b_hi) + _dot(a_hi, b_lo)


def _sigmoid(x):
    return 1.0 / (1.0 + jnp.exp(-x))


def _layer_norm_rows(x):
    mu = jnp.mean(x, axis=-1, keepdims=True)
    xc = x - mu
    var = jnp.mean(xc * xc, axis=-1, keepdims=True)
    return xc * lax.rsqrt(var + LN_EPS)


def _mod_kernel(c_ref, w_ref, b_ref, o_ref):
    c = c_ref[...]
    s = c * _sigmoid(c)
    o_ref[...] = _mm(s, w_ref[...], passes=3) + b_ref[...]


def _modulation(cond, w_ada, b_ada):
    d, n = w_ada.shape
    tn = _pick(n, 512, LANES)
    return pl.pallas_call(
        _mod_kernel,
        grid=(n // tn,),
        in_specs=[pl.BlockSpec((COND_ROWS, d), lambda j: (0, 0)),
                  pl.BlockSpec((d, tn), lambda j: (0, j)),
                  pl.BlockSpec((1, tn), lambda j: (0, j))],
        out_specs=pl.BlockSpec((COND_ROWS, tn), lambda j: (0, j)),
        out_shape=jax.ShapeDtypeStruct((COND_ROWS, n), F32),
        compiler_params=_cparams(("parallel",)),
        name="modulation",
    )(cond, w_ada, b_ada.reshape(1, n))


def _mod_row(mod_ref, row, chunk, d):
    return mod_ref[pl.ds(row, 1), chunk * d:(chunk + 1) * d]


def _ln_mod_kernel(x_ref, mod_ref, h_ref, *, nctx_blk, ctx_row, k_shift, k_scale):
    b, j = pl.program_id(0), pl.program_id(1)
    d = x_ref.shape[-1]
    row = jnp.where(j < nctx_blk, ctx_row, b)
    shift = _mod_row(mod_ref, row, k_shift, d)
    scale = _mod_row(mod_ref, row, k_scale, d)
    h_ref[...] = (_layer_norm_rows(x_ref[...]) * (1.0 + scale) + shift).astype(h_ref.dtype)


def _ln_mod(x_all, mod, *, lc, tr, ctx_row, k_shift, k_scale):
    bsz, t, d = x_all.shape
    kern = functools.partial(_ln_mod_kernel, nctx_blk=lc // tr, ctx_row=ctx_row,
                             k_shift=k_shift, k_scale=k_scale)
    return pl.pallas_call(
        kern,
        grid=(bsz, t // tr),
        in_specs=[pl.BlockSpec((None, tr, d), lambda b, j: (b, j, 0)),
                  pl.BlockSpec(mod.shape, lambda b, j: (0, 0))],
        out_specs=pl.BlockSpec((None, tr, d), lambda b, j: (b, j, 0)),
        out_shape=jax.ShapeDtypeStruct((bsz, t, d), BF16),
        compiler_params=_cparams(("parallel", "parallel")),
        name="ln_modulate",
    )(x_all, mod)


def _mm_kernel(a_ref, w_ref, o_ref, *scratch, nk):
    if nk == 1:
        o_ref[...] = _dot(a_ref[...], w_ref[...]).astype(o_ref.dtype)
        return
    acc_ref, = scratch
    k = pl.program_id(2)

    @pl.when(k == 0)
    def _():
        acc_ref[...] = jnp.zeros_like(acc_ref)

    acc_ref[...] += _dot(a_ref[...], w_ref[...])

    @pl.when(k == nk - 1)
    def _():
        o_ref[...] = acc_ref[...].astype(o_ref.dtype)


def _matmul(a, w, *, out_dtype, tm, tn, tk, name):
    m, kdim = a.shape
    n = w.shape[1]
    tm, tn, tk = _pick(m, tm, SUBLANES), _pick(n, tn, LANES), _pick(kdim, tk, LANES)
    nk = kdim // tk
    scratch = [pltpu.VMEM((tm, tn), F32)] if nk > 1 else []
    return pl.pallas_call(
        functools.partial(_mm_kernel, nk=nk),
        grid=(m // tm, n // tn, nk),
        in_specs=[pl.BlockSpec((tm, tk), lambda i, j, k: (i, k)),
                  pl.BlockSpec((tk, tn), lambda i, j, k: (k, j))],
        out_specs=pl.BlockSpec((tm, tn), lambda i, j, k: (i, j)),
        out_shape=jax.ShapeDtypeStruct((m, n), out_dtype),
        scratch_shapes=scratch,
        compiler_params=_cparams(("parallel", "parallel", "arbitrary")),
        name=name,
    )(a, w)


def _glu_kernel(a_ref, w1_ref, w3_ref, o_ref):
    a = a_ref[...]
    p1 = _dot(a, w1_ref[...])
    p3 = _dot(a, w3_ref[...])
    o_ref[...] = (p1 * _sigmoid(p1) * p3).astype(o_ref.dtype)


def _glu(a, w1, w3, *, tm, tn, name):
    m, kdim = a.shape
    f = w1.shape[1]
    tm, tn = _pick(m, tm, SUBLANES), _pick(f, tn, LANES)
    w_spec = pl.BlockSpec((kdim, tn), lambda i, j: (0, j))
    return pl.pallas_call(
        _glu_kernel,
        grid=(m // tm, f // tn),
        in_specs=[pl.BlockSpec((tm, kdim), lambda i, j: (i, 0)), w_spec, w_spec],
        out_specs=pl.BlockSpec((tm, tn), lambda i, j: (i, j)),
        out_shape=jax.ShapeDtypeStruct((m, f), BF16),
        compiler_params=_cparams(("parallel", "parallel")),
        name=name,
    )(a, w1, w3)


def _pool_consts(lc, tl, grid_w, tb):
    rows = tl // grid_w
    cb, c1, inv = [], [], []
    for win in POOL_WINDOWS:
        half = win // 2
        t = np.arange(tb)
        same_row = (t[:, None] // grid_w) == (t[None, :] // grid_w)
        dc = (t[None, :] % grid_w) - (t[:, None] % grid_w)
        cb.append((same_row & (dc >= -half) & (dc < half)).astype(np.float32))
        tc = np.arange(lc)
        d1 = tc[None, :] - tc[:, None]
        c1.append(((d1 >= -half) & (d1 < half)).astype(np.float32))
        cnt1 = np.minimum(tc + half, lc) - np.maximum(tc - half, 0)
        g = np.arange(grid_w)
        cntc = np.minimum(g + half, grid_w) - np.maximum(g - half, 0)
        r = np.arange(rows)
        cntr = np.minimum(r + half, rows) - np.maximum(r - half, 0)
        cnt2 = (cntr[:, None] * cntc[None, :]).reshape(-1)
        iv = 1.0 / np.concatenate([cnt1, cnt2]).astype(np.float64)
        inv.append(np.broadcast_to(iv[:, None], (lc + tl, LANES)).astype(np.float32))
    return (jnp.asarray(np.stack(cb), BF16), jnp.asarray(np.stack(c1), BF16),
            jnp.asarray(np.stack(inv), F32))


def _pool_kernel(u_ref, cb_ref, c1_ref, inv_ref, pw_ref, ps_ref, o_ref, s1_ref, acc_ref,
                 *, lc, tb, grid_w, half):
    t = u_ref.shape[0]
    tl = t - lc
    gc_hi, gc_lo = _split2(u_ref[0:lc, :])
    c1 = c1_ref[...]
    acc_ref[0:lc, :] = _dot(c1, gc_hi) + _dot(c1, gc_lo)
    cb = cb_ref[...]
    for blk in range(tl // tb):
        lo, hi = lc + blk * tb, lc + (blk + 1) * tb
        g_hi, g_lo = _split2(u_ref[lo:hi, :])
        s1_ref[lo:hi, :] = _dot(cb, g_hi) + _dot(cb, g_lo)
    acc_ref[lc:t, :] = s1_ref[lc:t, :]
    for dr in range(-half, half):
        if dr == 0:
            continue
        sh = abs(dr) * grid_w
        if sh >= tl:
            continue
        if dr > 0:
            acc_ref[lc:t - sh, :] += s1_ref[lc + sh:t, :]
        else:
            acc_ref[lc + sh:t, :] += s1_ref[lc:t - sh, :]
    n_rep = u_ref.shape[1] // LANES
    diff = acc_ref[...] * jnp.concatenate([inv_ref[...]] * n_rep, axis=1) - u_ref[...]
    o_ref[...] = (_dot(diff.astype(BF16), pw_ref[...]) * ps_ref[...]).astype(o_ref.dtype)


def _pool_group(u_all, consts, pool_w, pool_scale, idx, *, lc, grid_w, tb, pg):
    bsz, t, _ = u_all.shape
    cb, c1, inv = consts
    half = POOL_WINDOWS[idx] // 2
    kern = functools.partial(_pool_kernel, lc=lc, tb=tb, grid_w=grid_w, half=half)
    return pl.pallas_call(
        kern,
        grid=(bsz,),
        in_specs=[pl.BlockSpec((None, t, pg), lambda b: (b, 0, idx)),
                  pl.BlockSpec((None, tb, tb), lambda b: (idx, 0, 0)),
                  pl.BlockSpec((None, lc, lc), lambda b: (idx, 0, 0)),
                  pl.BlockSpec((None, t, LANES), lambda b: (idx, 0, 0)),
                  pl.BlockSpec((None, pg, pg), lambda b: (idx, 0, 0)),
                  pl.BlockSpec((1, pg), lambda b: (0, idx))],
        out_specs=pl.BlockSpec((None, t, pg), lambda b: (b, 0, 0)),
        out_shape=jax.ShapeDtypeStruct((bsz, t, pg), BF16),
        scratch_shapes=[pltpu.VMEM((t, pg), F32), pltpu.VMEM((t, pg), F32)],
        compiler_params=_cparams(("parallel",)),
        name="pool_mixer_%d" % idx,
    )(u_all, cb, c1, inv, pool_w, pool_scale)


def _token_shift(f, prev8, next8, mu_p, mu_n, seg_first, seg_last):
    tr = f.shape[0]
    row = lax.broadcasted_iota(jnp.int32, f.shape, 0)
    prev_row = jnp.where(seg_first, 0.0, prev8[SUBLANES - 1:SUBLANES, :])
    next_row = jnp.where(seg_last, 0.0, next8[0:1, :])
    prev = jnp.where(row == 0, prev_row, pltpu.roll(f, 1, 0))
    nxt = jnp.where(row == tr - 1, next_row, pltpu.roll(f, tr - 1, 0))
    return f + mu_p * (prev - f) + mu_n * (nxt - f)


def _segment_flags(j, nctx_blk, n_blk):
    seg_first = jnp.logical_or(j == 0, j == nctx_blk)
    seg_last = jnp.logical_or(j == nctx_blk - 1, j == n_blk - 1)
    return seg_first, seg_last


def _shift_specs(tr, tc, col_blk, t):
    r8 = tr // SUBLANES
    last8 = t // SUBLANES - 1
    cur = pl.BlockSpec((None, tr, tc), lambda b, j, c: (b, j, col_blk(c)))
    prv = pl.BlockSpec((None, SUBLANES, tc),
                       lambda b, j, c: (b, jnp.maximum(j * r8 - 1, 0), col_blk(c)))
    nxt = pl.BlockSpec((None, SUBLANES, tc),
                       lambda b, j, c: (b, jnp.minimum((j + 1) * r8, last8), col_blk(c)))
    return [cur, prv, nxt]


def _lora_kernel(u_ref, up_ref, un_ref, mu_ref, o_ref, *, nctx_blk, n_blk, lr, gl):
    j = pl.program_id(1)
    seg_first, seg_last = _segment_flags(j, nctx_blk, n_blk)
    z = _token_shift(u_ref[...], up_ref[...], un_ref[...], mu_ref[0:1, :], mu_ref[1:2, :],
                     seg_first, seg_last)
    col = lax.broadcasted_iota(jnp.int32, z.shape, 1)
    act = jnp.where(col < 2 * lr, jnp.tanh(z),
                    jnp.where(jnp.logical_and(col >= 4 * lr, col < 4 * lr + gl), _sigmoid(z), z))
    o_ref[...] = act.astype(o_ref.dtype)


def _lora_act(u_all, mu_lora, *, lc, tr, col0, wl, lr, gl):
    bsz, t, _ = u_all.shape
    specs = _shift_specs(tr, wl, lambda c: col0 // wl, t)
    kern = functools.partial(_lora_kernel, nctx_blk=lc // tr, n_blk=t // tr, lr=lr, gl=gl)
    return pl.pallas_call(
        kern,
        grid=(bsz, t // tr, 1),
        in_specs=specs + [pl.BlockSpec((SUBLANES, wl), lambda b, j, c: (0, 0))],
        out_specs=pl.BlockSpec((None, tr, wl), lambda b, j, c: (b, j, 0)),
        out_shape=jax.ShapeDtypeStruct((bsz, t, wl), BF16),
        compiler_params=_cparams(("parallel", "parallel", "arbitrary")),
        name="lora_act",
    )(u_all, u_all, u_all, mu_lora)


def _head_ones():
    r = lax.broadcasted_iota(jnp.int32, (LANES, LANES), 0)
    c = lax.broadcasted_iota(jnp.int32, (LANES, LANES), 1)
    return jnp.where((r ^ c) < HEAD, 1.0, 0.0).astype(BF16)


def _head_sum(x, ones):
    outs = []
    for s in range(x.shape[1] // LANES):
        hi, mid, lo = _split3(x[:, s * LANES:(s + 1) * LANES])
        outs.append(_dot(hi, ones) + _dot(mid, ones) + _dot(lo, ones))
    return outs[0] if len(outs) == 1 else jnp.concatenate(outs, axis=1)


(_V_MPR, _V_MNR, _V_MPK, _V_MNK, _V_MPV, _V_MNV, _V_W0F, _V_W0B, _V_A0F, _V_A0B,
 _V_V0, _V_KK, _V_KA, _V_RK, _V_LG, _V_LB) = range(16)
_DECAY_SCALE = math.exp(-0.5)


def _terms_kernel(*refs, nctx_blk, n_blk, lr, gl, has_vres):
    (ur, urp, urn, uk, ukp, ukn, uv, uvp, uvn, act_ref, vec_ref,
     w2f_ref, w2b_ref, a2f_ref, a2b_ref, gup_ref) = refs[:16]
    pos = 16
    if has_vres:
        v2_ref, vfirst_ref = refs[pos:pos + 2]
        pos += 2
    (r_o, v_o, kk_o, lwf_o, lwb_o, kf_o, kb_o, af_o, ab_o, gate_o, bonus_o) = refs[pos:]

    j = pl.program_id(1)
    seg_first, seg_last = _segment_flags(j, nctx_blk, n_blk)
    vec = lambda i: vec_ref[i:i + 1, :]
    r = _token_shift(ur[...], urp[...], urn[...], vec(_V_MPR), vec(_V_MNR), seg_first, seg_last)
    k = _token_shift(uk[...], ukp[...], ukn[...], vec(_V_MPK), vec(_V_MNK), seg_first, seg_last)
    v = _token_shift(uv[...], uvp[...], uvn[...], vec(_V_MPV), vec(_V_MNV), seg_first, seg_last)

    act = act_ref[...]
    a_wf, a_wb = act[:, 0:lr], act[:, lr:2 * lr]
    a_af, a_ab = act[:, 2 * lr:3 * lr], act[:, 3 * lr:4 * lr]
    a_g = act[:, 4 * lr:4 * lr + gl]
    if has_vres:
        a_v = act[:, 4 * lr + gl:5 * lr + gl]
        v = v + (vfirst_ref[...] - v) * _sigmoid(vec(_V_V0) + _dot(a_v, v2_ref[...]))

    lwf_o[...] = -_DECAY_SCALE * _sigmoid(vec(_V_W0F) + _dot(a_wf, w2f_ref[...]))
    lwb_o[...] = -_DECAY_SCALE * _sigmoid(vec(_V_W0B) + _dot(a_wb, w2b_ref[...]))
    a_f = _sigmoid(vec(_V_A0F) + _dot(a_af, a2f_ref[...]))
    a_b = _sigmoid(vec(_V_A0B) + _dot(a_ab, a2b_ref[...]))

    ones = _head_ones()
    kkr = k * vec(_V_KK)
    norm = jnp.sqrt(_head_sum(kkr * kkr, ones))
    kk_o[...] = kkr / jnp.maximum(norm, NORM_EPS)
    k_f = k * (1.0 + (a_f - 1.0) * vec(_V_KA))
    k_b = k * (1.0 + (a_b - 1.0) * vec(_V_KA))
    bonus_o[...] = _head_sum(r * (k_f + k_b) * vec(_V_RK), ones) * v
    gate_o[...] = _dot(a_g, gup_ref[...])
    r_o[...] = r
    v_o[...] = v
    kf_o[...] = k_f
    kb_o[...] = k_b
    af_o[...] = a_f
    ab_o[...] = a_b


def _rwkv_terms(u_all, act, vec, lw, v_first, *, lc, tr, tc, pool_w, rw, lr, gl):
    bsz, t, _ = u_all.shape
    has_vres = v_first is not None
    nb = rw // tc
    specs = []
    for part in range(3):
        base = (pool_w + part * rw) // tc
        specs += _shift_specs(tr, tc, (lambda c, base=base: base + c), t)
    specs.append(pl.BlockSpec((None, tr, act.shape[-1]), lambda b, j, c: (b, j, 0)))
    specs.append(pl.BlockSpec((vec.shape[0], tc), lambda b, j, c: (0, c)))
    wspec = lambda rows: pl.BlockSpec((rows, tc), lambda b, j, c: (0, c))
    specs += [wspec(lr), wspec(lr), wspec(lr), wspec(lr), wspec(gl)]
    args = [u_all] * 9 + [act, vec, lw["w2_f"], lw["w2_b"], lw["a2_f"], lw["a2_b"], lw["g_up"]]
    if has_vres:
        specs += [wspec(lr), pl.BlockSpec((None, tr, tc), lambda b, j, c: (b, j, c))]
        args += [lw["v2"], v_first]
    out_spec = pl.BlockSpec((None, tr, tc), lambda b, j, c: (b, j, c))
    out_shape = jax.ShapeDtypeStruct((bsz, t, rw), F32)
    kern = functools.partial(_terms_kernel, nctx_blk=lc // tr, n_blk=t // tr, lr=lr, gl=gl,
                             has_vres=has_vres)
    names = ("r", "v", "kk", "lw_f", "lw_b", "k_f", "k_b", "a_f", "a_b", "gate", "bonus")
    outs = pl.pallas_call(
        kern,
        grid=(bsz, t // tr, nb),
        in_specs=specs,
        out_specs=[out_spec] * len(names),
        out_shape=[out_shape] * len(names),
        compiler_params=_cparams(("parallel", "parallel", "parallel")),
        name="rwkv_terms",
    )(*args)
    return dict(zip(names, outs))


def _cumsum_rows(x, rev):
    n = x.shape[0]
    row = lax.broadcasted_iota(jnp.int32, x.shape, 0)
    s = 1
    while s < n:
        if rev:
            x = x + jnp.where(row < n - s, pltpu.roll(x, n - s, 0), 0.0)
        else:
            x = x + jnp.where(row >= s, pltpu.roll(x, s, 0), 0.0)
        s *= 2
    return x


def _wkv_chunks(chains):
    n = chains[0][0].shape[0]
    hp = 2 * n
    lane = lax.broadcasted_iota(jnp.int32, (n, LANES), 1)
    first_head = lane < HEAD
    ri = lax.broadcasted_iota(jnp.int32, (hp, hp), 0)
    ci = lax.broadcasted_iota(jnp.int32, (hp, hp), 1)
    blk = ri ^ ci
    eye = ri == ci
    tdiff = (ci & (n - 1)) - (ri & (n - 1))
    dist = {False: jnp.where(blk < n, tdiff, hp), True: jnp.where(blk < n, -tdiff, hp)}

    def stack(x):
        return jnp.concatenate([jnp.where(first_head, x, 0.0), jnp.where(first_head, 0.0, x)], axis=0)

    pre = []
    for r, v, kk, lw, k, a, state, rev in chains:
        c = _cumsum_rows(lw, rev)
        ctot = c[0:1, :] if rev else c[n - 1:n, :]
        e_pos, e_neg = jnp.exp(c), jnp.exp(-c)
        e_prev, e_rem = jnp.exp(c - lw), jnp.exp(ctot - c)
        kka = kk * a
        bt, kt = kka * e_neg, k * e_neg
        pre.append(dict(s_at=stack(-kk * e_prev), s_rt=stack(r * e_pos), s_v=stack(v), rt=r * e_pos,
                        rhs1=jnp.concatenate([bt, bt, kt, kt], axis=0).astype(BF16),
                        lhs_t=jnp.concatenate([stack(kka * e_rem), stack(k * e_rem)], axis=0),
                        wtot=jnp.exp(ctot), strict=dist[rev] < 0, incl=dist[rev] <= 0))

    s1 = [_dot_nt(jnp.concatenate([p["s_at"], p["s_rt"]], axis=0).astype(BF16), p["rhs1"]) for p in pre]
    nmat = [jnp.where(p["strict"], s[:hp, :hp], 0.0) for p, s in zip(pre, s1)]
    m_ak = [jnp.where(p["strict"], s[:hp, hp:], 0.0) for p, s in zip(pre, s1)]
    m_rbk = [jnp.concatenate([jnp.where(p["incl"], s[hp:, :hp], 0.0), jnp.where(p["incl"], s[hp:, hp:], 0.0)],
                             axis=1).astype(BF16) for p, s in zip(pre, s1)]
    z = [_mm(m, p["s_v"]) for m, p in zip(m_ak, pre)]

    n8 = [jnp.where(blk < SUBLANES, m, 0.0) for m in nmat]
    tmat = [jnp.where(eye, 1.0, m) for m in n8]
    n2 = [_mm(m, m, INV_PASSES) for m in n8]
    st = [_mm(jnp.concatenate([t, m], axis=0), m, INV_PASSES) for t, m in zip(tmat, n2)]
    tmat = [t + s[:hp] for t, s in zip(tmat, st)]
    tmat = [t + _mm(t, s[hp:], INV_PASSES) for t, s in zip(tmat, st)]
    size = SUBLANES
    while size < n:
        sel = jnp.logical_and(blk >= size, blk < 2 * size)
        x = [_mm(jnp.where(sel, m, 0.0), t, INV_PASSES) for m, t in zip(nmat, tmat)]
        tmat = [t + _mm(t, xx, INV_PASSES) for t, xx in zip(tmat, x)]
        size *= 2

    gu = [_mm(t, jnp.concatenate([p["s_at"], zz], axis=1)) for t, p, zz in zip(tmat, pre, z)]
    zeros = jnp.zeros((hp, LANES), F32)
    rhs4 = [jnp.concatenate([g, jnp.concatenate([zeros, p["s_v"]], axis=1)], axis=0).astype(BF16)
            for g, p in zip(gu, pre)]
    top = [_dot(m, rr) for m, rr in zip(m_rbk, rhs4)]
    bot = [_dot(p["lhs_t"].T.astype(BF16), rr) for p, rr in zip(pre, rhs4)]

    outs = []
    for p, tp, bt_, chain in zip(pre, top, bot, chains):
        q = p["rt"] + tp[:n, :LANES] + tp[n:, :LANES]
        amat = jnp.where(eye, jnp.broadcast_to(p["wtot"], (LANES, LANES)), 0.0) + bt_[:, :LANES]
        qa = jnp.concatenate([q, amat], axis=0).astype(BF16)
        s_hi, s_lo = _split2(chain[6])
        outs.append(_dot(qa, s_hi) + _dot(qa, s_lo))
    return [(o[:n] + tp[:n, LANES:] + tp[n:, LANES:], o[n:] + bt_[:, LANES:])
            for o, tp, bt_ in zip(outs, top, bot)]


def _scan_kernel(*refs, pairs):
    ins, (yf_ref, yb_ref, state_ref) = refs[:12], refs[12:]
    s = pl.program_id(1)

    @pl.when(s == 0)
    def _():
        state_ref[...] = jnp.zeros_like(state_ref)

    chains = [(d, p, slice(p * LANES, (p + 1) * LANES)) for d in range(2) for p in range(pairs)]
    loaded = [tuple(ref[:, cols] for ref in ins[6 * d:6 * d + 6]) + (state_ref[d, p], bool(d))
              for d, p, cols in chains]
    for (d, p, cols), (y, new_state) in zip(chains, _wkv_chunks(loaded)):
        (yb_ref if d else yf_ref)[:, cols] = y
        state_ref[d, p] = new_state


def _wkv_scan(tm, *, lc, chunk, pairs):
    bsz, t, rw = tm["r"].shape
    width = pairs * LANES
    ngrp = rw // width
    nctx, ntot = lc // chunk, t // chunk

    def fwd(g, s):
        return (g // ngrp, s, g % ngrp)

    def bwd(g, s):
        return (g // ngrp, jnp.where(s < nctx, nctx - 1 - s, ntot - 1 - (s - nctx)), g % ngrp)

    blk = (None, chunk, width)
    names_f = ("r", "v", "kk", "lw_f", "k_f", "a_f")
    names_b = ("r", "v", "kk", "lw_b", "k_b", "a_b")
    in_specs = [pl.BlockSpec(blk, fwd)] * 6 + [pl.BlockSpec(blk, bwd)] * 6
    args = [tm[n] for n in names_f] + [tm[n] for n in names_b]
    return pl.pallas_call(
        functools.partial(_scan_kernel, pairs=pairs),
        grid=(bsz * ngrp, ntot),
        in_specs=in_specs,
        out_specs=[pl.BlockSpec(blk, fwd), pl.BlockSpec(blk, bwd)],
        out_shape=[jax.ShapeDtypeStruct((bsz, t, rw), F32)] * 2,
        scratch_shapes=[pltpu.VMEM((2, pairs, LANES, LANES), F32)],
        compiler_params=_cparams(("parallel", "arbitrary")),
        name="wkv7_scan",
    )(*args)


def _rwkv_out_kernel(yf_ref, yb_ref, bonus_ref, gate_ref, vec_ref, o_ref):
    y = yf_ref[...] + yb_ref[...]
    ones = _head_ones()
    inv_n = 1.0 / HEAD
    mu = _head_sum(y, ones) * inv_n
    yc = y - mu
    var = _head_sum(yc * yc, ones) * inv_n
    yn = yc * lax.rsqrt(var + GN_EPS)
    yn = yn * vec_ref[_V_LG:_V_LG + 1, :] + vec_ref[_V_LB:_V_LB + 1, :]
    o_ref[...] = ((yn + bonus_ref[...]) * gate_ref[...]).astype(o_ref.dtype)


def _rwkv_out(y_f, y_b, bonus, gate, vec, *, tr, tc):
    bsz, t, rw = y_f.shape
    spec = pl.BlockSpec((None, tr, tc), lambda b, j, c: (b, j, c))
    return pl.pallas_call(
        _rwkv_out_kernel,
        grid=(bsz, t // tr, rw // tc),
        in_specs=[spec] * 4 + [pl.BlockSpec((vec.shape[0], tc), lambda b, j, c: (0, c))],
        out_specs=spec,
        out_shape=jax.ShapeDtypeStruct((bsz, t, rw), BF16),
        compiler_params=_cparams(("parallel", "parallel", "parallel")),
        name="rwkv_out",
    )(y_f, y_b, bonus, gate, vec)


_R_E1, _R_E2, _R_G1, _R_G2 = range(4)

def _res_kernel(*refs, alpha, row_off_blk, nctx_blk, ctx_row, k_gate, k_shift, k_scale,
                with_mod, n_exp):
    x_ref, y_ref, mod_ref, gb_ref = refs[:4]
    pos = 4
    if n_exp:
        router_ref = refs[pos]
        pos += 1
    outs = refs[pos:]
    b, j = pl.program_id(0), pl.program_id(1)
    d = x_ref.shape[-1]
    row = jnp.where(j + row_off_blk < nctx_blk, ctx_row, b)
    gate = _mod_row(mod_ref, row, k_gate, d)
    xn = _layer_norm_rows(alpha * x_ref[...] + gate * y_ref[...]) * gb_ref[0:1, :] + gb_ref[1:2, :]
    outs[0][...] = xn
    if not with_mod:
        return
    h = _layer_norm_rows(xn) * (1.0 + _mod_row(mod_ref, row, k_scale, d)) + _mod_row(mod_ref, row, k_shift, d)
    outs[1][...] = h.astype(outs[1].dtype)
    if not n_exp:
        return
    logits = _mm(h, router_ref[...], passes=3)
    lane = lax.broadcasted_iota(jnp.int32, logits.shape, 1).astype(F32)
    neg = -jnp.inf
    logits = jnp.where(lane < n_exp, logits, neg)
    m1 = jnp.max(logits, axis=-1, keepdims=True)
    i1 = jnp.min(jnp.where(logits == m1, lane, float(LANES)), axis=-1, keepdims=True)
    rest = jnp.where(lane == i1, neg, logits)
    m2 = jnp.max(rest, axis=-1, keepdims=True)
    i2 = jnp.min(jnp.where(rest == m2, lane, float(LANES)), axis=-1, keepdims=True)
    e2 = jnp.exp(m2 - m1)
    g1 = 1.0 / (1.0 + e2)
    g2 = e2 / (1.0 + e2)
    outs[2][...] = jnp.where(lane == _R_E1, i1, jnp.where(lane == _R_E2, i2,
                             jnp.where(lane == _R_G1, g1, jnp.where(lane == _R_G2, g2, 0.0))))


def _residual_ln(x, y, mod, gain, bias, *, alpha, tr, lc, ctx_row, k_gate, x_off=0, y_off=0,
                 rows=None, mod2=None, router=None, h_dtype=BF16):
    bsz, _, d = x.shape
    rows = x.shape[1] if rows is None else rows
    n_exp = 0 if router is None else router.shape[1]
    gb = jnp.stack([gain, bias])
    in_specs = [pl.BlockSpec((None, tr, d), lambda b, j: (b, j + x_off // tr, 0)),
                pl.BlockSpec((None, tr, d), lambda b, j: (b, j + y_off // tr, 0)),
                pl.BlockSpec(mod.shape, lambda b, j: (0, 0)),
                pl.BlockSpec((2, d), lambda b, j: (0, 0))]
    args = [x, y, mod, gb]
    out_specs = [pl.BlockSpec((None, tr, d), lambda b, j: (b, j, 0))]
    out_shape = [jax.ShapeDtypeStruct((bsz, rows, d), F32)]
    if mod2 is not None:
        out_specs.append(pl.BlockSpec((None, tr, d), lambda b, j: (b, j, 0)))
        out_shape.append(jax.ShapeDtypeStruct((bsz, rows, d), h_dtype))
    if router is not None:
        rpad = jnp.zeros((d, LANES), F32).at[:, :n_exp].set(router)
        in_specs.append(pl.BlockSpec((d, LANES), lambda b, j: (0, 0)))
        args.append(rpad)
        out_specs.append(pl.BlockSpec((None, tr, LANES), lambda b, j: (b, j, 0)))
        out_shape.append(jax.ShapeDtypeStruct((bsz, rows, LANES), F32))
    k_shift, k_scale = mod2 if mod2 is not None else (0, 0)
    kern = functools.partial(_res_kernel, alpha=alpha, row_off_blk=x_off // tr, nctx_blk=lc // tr,
                             ctx_row=ctx_row, k_gate=k_gate, k_shift=k_shift, k_scale=k_scale,
                             with_mod=mod2 is not None, n_exp=n_exp)
    return pl.pallas_call(
        kern,
        grid=(bsz, rows // tr),
        in_specs=in_specs,
        out_specs=out_specs,
        out_shape=out_shape,
        compiler_params=_cparams(("parallel", "parallel")),
        name="residual_ln",
    )(*args)


def _route_plan(route, n_exp, tile):
    n = route.shape[0]
    e_flat = jnp.concatenate([route[:, _R_E1], route[:, _R_E2]]).astype(jnp.int32)
    experts = jnp.arange(n_exp, dtype=jnp.int32)
    onehot = (e_flat[:, None] == experts[None, :]).astype(jnp.int32)
    csum = jnp.cumsum(onehot, axis=0)
    rank = jnp.sum((csum - onehot) * onehot, axis=1)
    tiles_e = (csum[-1] + tile - 1) // tile
    tile_end = jnp.cumsum(tiles_e)
    tile_start = tile_end - tiles_e
    pos = jnp.sum(onehot * tile_start[None, :], axis=1) * tile + rank
    n_tiles = (2 * n) // tile + n_exp
    tidx = jnp.arange(n_tiles, dtype=jnp.int32)
    valid = tidx < tile_end[-1]
    last_e = jnp.max(jnp.where(tiles_e > 0, experts, 0))
    tile_expert = jnp.sum((tidx[:, None] >= tile_end[None, :]).astype(jnp.int32), axis=1)
    tile_expert = jnp.where(valid, tile_expert, last_e)
    token = jnp.arange(2 * n, dtype=jnp.int32) % n
    src_token = jnp.zeros((n_tiles * tile,), jnp.int32).at[pos].set(token)
    return src_token, pos, tile_expert, valid.astype(jnp.int32)


def _gather_kernel(idx_ref, src_ref, out_ref, sem):
    rows = out_ref.shape[0]

    def row_copy(j, src_row):
        return pltpu.make_async_copy(src_ref.at[pl.ds(src_row, 1)], out_ref.at[pl.ds(j, 1)], sem)

    def start(j, carry):
        row_copy(j, idx_ref[0, j]).start()
        return carry

    def wait(j, carry):
        row_copy(j, 0).wait()
        return carry

    lax.fori_loop(0, rows, start, 0)
    lax.fori_loop(0, rows, wait, 0)


def _gather_rows(src, idx, *, rows):
    n_out = idx.shape[0]
    d = src.shape[1]
    rows = _pick(n_out, rows, SUBLANES)
    return pl.pallas_call(
        _gather_kernel,
        grid=(n_out // rows,),
        in_specs=[pl.BlockSpec((None, 1, rows), lambda i: (i, 0, 0), memory_space=pltpu.SMEM),
                  pl.BlockSpec(memory_space=pl.ANY)],
        out_specs=pl.BlockSpec((rows, d), lambda i: (i, 0)),
        out_shape=jax.ShapeDtypeStruct((n_out, d), src.dtype),
        scratch_shapes=[pltpu.SemaphoreType.DMA(())],
        compiler_params=_cparams(("arbitrary",)),
        name="row_gather",
    )(idx.reshape(n_out // rows, 1, rows), src)


def _moe_glu_kernel(te_ref, tv_ref, a_ref, w1_ref, w3_ref, o_ref):
    i = pl.program_id(1)

    @pl.when(tv_ref[i] != 0)
    def _():
        a = a_ref[...].astype(BF16)
        p1 = _dot(a, w1_ref[...])
        p3 = _dot(a, w3_ref[...])
        o_ref[...] = (p1 * _sigmoid(p1) * p3).astype(o_ref.dtype)

    @pl.when(tv_ref[i] == 0)
    def _():
        o_ref[...] = jnp.zeros_like(o_ref)


def _moe_down_kernel(te_ref, tv_ref, h_ref, w2_ref, o_ref):
    i = pl.program_id(1)

    @pl.when(tv_ref[i] != 0)
    def _():
        o_ref[...] = _dot(h_ref[...], w2_ref[...])

    @pl.when(tv_ref[i] == 0)
    def _():
        o_ref[...] = jnp.zeros_like(o_ref)


def _moe_experts(xs, w1, w3, w2, tile_expert, tile_valid, *, tile, tn_up, tn_down):
    r, d = xs.shape
    _, _, f = w1.shape
    n_tiles = r // tile
    tn_up, tn_down = _pick(f, tn_up, LANES), _pick(d, tn_down, LANES)
    up_spec = pl.BlockSpec((None, d, tn_up), lambda j, i, te, tv: (te[i], 0, j))
    hid = pl.pallas_call(
        _moe_glu_kernel,
        grid_spec=pltpu.PrefetchScalarGridSpec(
            num_scalar_prefetch=2, grid=(f // tn_up, n_tiles),
            in_specs=[pl.BlockSpec((tile, d), lambda j, i, te, tv: (i, 0)), up_spec, up_spec],
            out_specs=pl.BlockSpec((tile, tn_up), lambda j, i, te, tv: (i, j))),
        out_shape=jax.ShapeDtypeStruct((r, f), BF16),
        compiler_params=_cparams(("parallel", "arbitrary")),
        name="moe_glu",
    )(tile_expert, tile_valid, xs, w1, w3)
    return pl.pallas_call(
        _moe_down_kernel,
        grid_spec=pltpu.PrefetchScalarGridSpec(
            num_scalar_prefetch=2, grid=(d // tn_down, n_tiles),
            in_specs=[pl.BlockSpec((tile, f), lambda j, i, te, tv: (i, 0)),
                      pl.BlockSpec((None, f, tn_down), lambda j, i, te, tv: (te[i], 0, j))],
            out_specs=pl.BlockSpec((tile, tn_down), lambda j, i, te, tv: (i, j))),
        out_shape=jax.ShapeDtypeStruct((r, d), F32),
        compiler_params=_cparams(("parallel", "arbitrary")),
        name="moe_down",
    )(tile_expert, tile_valid, hid, w2)


def _moe_out_kernel(x_ref, y1_ref, y2_ref, route_ref, mod_ref, gb_ref, o_ref, *, alpha, k_gate):
    b = pl.program_id(0)
    d = x_ref.shape[-1]
    route = route_ref[...]
    y = route[:, _R_G1:_R_G1 + 1] * y1_ref[...] + route[:, _R_G2:_R_G2 + 1] * y2_ref[...]
    gate = _mod_row(mod_ref, b, k_gate, d)
    o_ref[...] = _layer_norm_rows(alpha * x_ref[...] + gate * y) * gb_ref[0:1, :] + gb_ref[1:2, :]


def _moe_residual_ln(x, yg, route, mod, gain, bias, *, alpha, tr, k_gate):
    bsz, rows, d = x.shape
    row_spec = pl.BlockSpec((None, tr, d), lambda b, j: (b, j, 0))
    return pl.pallas_call(
        functools.partial(_moe_out_kernel, alpha=alpha, k_gate=k_gate),
        grid=(bsz, rows // tr),
        in_specs=[row_spec,
                  pl.BlockSpec((None, None, tr, d), lambda b, j: (0, b, j, 0)),
                  pl.BlockSpec((None, None, tr, d), lambda b, j: (1, b, j, 0)),
                  pl.BlockSpec((None, tr, LANES), lambda b, j: (b, j, 0)),
                  pl.BlockSpec(mod.shape, lambda b, j: (0, 0)),
                  pl.BlockSpec((2, d), lambda b, j: (0, 0))],
        out_specs=row_spec,
        out_shape=jax.ShapeDtypeStruct((bsz, rows, d), F32),
        compiler_params=_cparams(("parallel", "parallel")),
        name="moe_residual_ln",
    )(x, yg, yg, route, mod, jnp.stack([gain, bias]))


def _pad_rows(w, rows):
    return jnp.zeros((rows,) + w.shape[1:], w.dtype).at[:w.shape[0]].set(w)


def _layer_layout(p, pool_w, rw):
    lr_raw = p["w2_f"].shape[0]
    gl_raw = p["g_up"].shape[0]
    vr_raw = p["v2"].shape[0] if "v2" in p else 0
    lr = -(-max(lr_raw, vr_raw, 1) // LANES) * LANES
    gl = -(-gl_raw // LANES) * LANES
    wl = 5 * lr + gl
    wl_pad = -(-wl // 1024) * 1024 if wl > 512 else wl
    core = pool_w + 3 * rw
    sizes = [lr_raw] * 4 + [gl_raw] + ([vr_raw] if vr_raw else [])
    slots = [lr] * 4 + [gl] + [lr]
    d = p["w_in"].shape[0]

    def relayout(src, rows_shape):
        dst = jnp.zeros(rows_shape + (core + wl_pad,), src.dtype)
        dst = dst.at[..., :core].set(src[..., :core])
        s_off, d_off = core, core
        for size, slot in zip(sizes, slots):
            dst = dst.at[..., d_off:d_off + size].set(src[..., s_off:s_off + size])
            s_off += size
            d_off += slot
        return dst

    w_in = relayout(p["w_in"], (d,)).astype(BF16)
    zero_pool = jnp.zeros((pool_w,), F32)
    mu_p = relayout(jnp.concatenate([zero_pool, p["mu_prev"]]), ())
    mu_n = relayout(jnp.concatenate([zero_pool, p["mu_next"]]), ())
    mu_lora = jnp.zeros((SUBLANES, wl_pad), F32).at[0].set(mu_p[core:]).at[1].set(mu_n[core:])
    seg = lambda a, i: a[pool_w + i * rw:pool_w + (i + 1) * rw]
    zeros = jnp.zeros((rw,), F32)
    vec = jnp.stack([seg(mu_p, 0), seg(mu_n, 0), seg(mu_p, 1), seg(mu_n, 1), seg(mu_p, 2), seg(mu_n, 2),
                     p["w0_f"], p["w0_b"], p["a0_f"], p["a0_b"], p.get("v0", zeros),
                     p["k_k"], p["k_a"], p["r_k"].reshape(-1), p["lnx_g"], p["lnx_b"]])
    lw = {n: _pad_rows(p[n], lr).astype(BF16) for n in ("w2_f", "w2_b", "a2_f", "a2_b")}
    lw["g_up"] = _pad_rows(p["g_up"], gl).astype(BF16)
    if vr_raw:
        lw["v2"] = _pad_rows(p["v2"], lr).astype(BF16)
    return dict(w_in=w_in, mu_lora=mu_lora, vec=vec, lw=lw, lr=lr, gl=gl, wl=wl_pad, core=core)


def _mixer(x_all, mod, p, lay, v_first, pool_consts, *, lc, tr, tc, grid_w, tb, ctx_row):
    bsz, t, d = x_all.shape
    mix_w = p["w_out"].shape[0]
    pool_w = mix_w // 4
    pg = pool_w // len(POOL_WINDOWS)
    rw = mix_w - pool_w
    h = _ln_mod(x_all, mod, lc=lc, tr=tr, ctx_row=ctx_row, k_shift=0, k_scale=1)
    u = _matmul(h.reshape(bsz * t, d), lay["w_in"], out_dtype=F32, tm=1024, tn=1024, tk=d,
                name="in_proj").reshape(bsz, t, -1)
    pool_wts = p["pool_w"].astype(BF16)
    pool_scale = p["pool_scale"].reshape(1, pool_w)
    pools = [_pool_group(u, pool_consts, pool_wts, pool_scale, i, lc=lc, grid_w=grid_w, tb=tb, pg=pg)
             for i in range(len(POOL_WINDOWS))]
    act = _lora_act(u, lay["mu_lora"], lc=lc, tr=tr, col0=lay["core"], wl=lay["wl"], lr=lay["lr"],
                    gl=lay["gl"])
    tm = _rwkv_terms(u, act, lay["vec"], lay["lw"], v_first, lc=lc, tr=tr, tc=tc, pool_w=pool_w,
                     rw=rw, lr=lay["lr"], gl=lay["gl"])
    y_f, y_b = _wkv_scan(tm, lc=lc, chunk=CHUNK, pairs=SCAN_PAIRS)
    out = _rwkv_out(y_f, y_b, tm["bonus"], tm["gate"], lay["vec"], tr=tr, tc=tc)
    mix_in = jnp.concatenate(pools + [out], axis=-1)
    mix = _matmul(mix_in.reshape(bsz * t, mix_w), p["w_out"].astype(BF16), out_dtype=F32,
                  tm=1024, tn=1024, tk=mix_w, name="out_proj").reshape(bsz, t, d)
    return mix, tm["v"]


def _forward(x, c, ctx, c_ctx, layers, grid_w):
    bsz, seq, d = x.shape
    lc = ctx.shape[1]
    depth = len(layers)
    alpha = (2 * depth) ** 0.25
    assert bsz < COND_ROWS and lc % CHUNK == 0 and seq % CHUNK == 0 and seq % grid_w == 0
    tr = _pick(math.gcd(lc, seq), 256, SUBLANES)
    tb = tr if tr % grid_w == 0 else grid_w
    assert seq % tb == 0 and tb % grid_w == 0
    ctx_row = bsz
    cond = jnp.zeros((COND_ROWS, d), F32).at[:bsz].set(c).at[ctx_row].set(c_ctx)
    x_all = jnp.concatenate([ctx, x], axis=1)
    pool_consts = _pool_consts(lc, seq, grid_w, tb)
    common = dict(tr=tr, lc=lc, ctx_row=ctx_row)
    v_first = None
    for i, p in enumerate(layers):
        last = i == depth - 1
        mix_w = p["w_out"].shape[0]
        pool_w = mix_w // 4
        rw = mix_w - pool_w
        assert pool_w % (len(POOL_WINDOWS) * LANES) == 0 and rw % (SCAN_PAIRS * LANES) == 0
        tc = _pick(rw, 256, LANES)
        lay = _layer_layout(p, pool_w, rw)
        mod = _modulation(cond, p["w_ada"], p["b_ada"])
        mix, v_cur = _mixer(x_all, mod, p, lay, v_first, pool_consts, tc=tc, grid_w=grid_w, tb=tb,
                            **common)
        if v_first is None:
            v_first = v_cur
        if not last:
            x1, h2 = _residual_ln(x_all, mix, mod, p["ln1_g"], p["ln1_b"], alpha=alpha, k_gate=2,
                                  mod2=(3, 4), **common)
            t = x_all.shape[1]
            hid = _glu(h2.reshape(bsz * t, d), p["ffn_w1"].astype(BF16), p["ffn_w3"].astype(BF16),
                       tm=1024, tn=256, name="ffn_glu")
            ffn = _matmul(hid, p["ffn_w2"].astype(BF16), out_dtype=F32, tm=512, tn=1024, tk=5504,
                          name="ffn_down").reshape(bsz, t, d)
            x_all, = _residual_ln(x1, ffn, mod, p["ln2_g"], p["ln2_b"], alpha=alpha, k_gate=5, **common)
        else:
            x1, h2, route = _residual_ln(x_all, mix, mod, p["ln1_g"], p["ln1_b"], alpha=alpha, k_gate=2,
                                         mod2=(3, 4), router=p["router"], h_dtype=F32, x_off=lc,
                                         y_off=lc, rows=seq, **common)
            n_exp = p["router"].shape[1]
            ntok = bsz * seq
            tile = _pick(2 * ntok, 512, SUBLANES)
            src_token, pos, tile_expert, tile_valid = _route_plan(route.reshape(ntok, LANES), n_exp, tile)
            xs = _gather_rows(h2.reshape(ntok, d), src_token, rows=256)
            ys = _moe_experts(xs, p["exp_w1"].astype(BF16), p["exp_w3"].astype(BF16),
                              p["exp_w2"].astype(BF16), tile_expert, tile_valid, tile=tile,
                              tn_up=512, tn_down=1024)
            yg = _gather_rows(ys, pos, rows=256).reshape(2, bsz, seq, d)
            return _moe_residual_ln(x1, yg, route, mod, p["ln2_g"], p["ln2_b"], alpha=alpha, tr=tr,
                                    k_gate=5)
    return x_all[:, lc:]


_LAYER0 = ("w_ada", "b_ada", "w_in", "mu_prev", "mu_next", "pool_w", "pool_scale", "w0_f", "w2_f",
           "w0_b", "w2_b", "a0_f", "a2_f", "a0_b", "a2_b", "g_up", "k_k", "k_a", "r_k", "lnx_g",
           "lnx_b", "w_out", "ln1_g", "ln1_b", "ln2_g", "ln2_b", "ffn_w1", "ffn_w3", "ffn_w2")
_LAYER1 = _LAYER0[:26] + ("v0", "v2", "router", "exp_w1", "exp_w3", "exp_w2")


def kernel(x, c, ctx, c_ctx, l0_w_ada, l0_b_ada, l0_w_in, l0_mu_prev, l0_mu_next, l0_pool_w, l0_pool_scale, l0_w0_f, l0_w2_f, l0_w0_b, l0_w2_b, l0_a0_f, l0_a2_f, l0_a0_b, l0_a2_b, l0_g_up, l0_k_k, l0_k_a, l0_r_k, l0_lnx_g, l0_lnx_b, l0_w_out, l0_ln1_g, l0_ln1_b, l0_ln2_g, l0_ln2_b, l0_ffn_w1, l0_ffn_w3, l0_ffn_w2, l1_w_ada, l1_b_ada, l1_w_in, l1_mu_prev, l1_mu_next, l1_pool_w, l1_pool_scale, l1_w0_f, l1_w2_f, l1_w0_b, l1_w2_b, l1_a0_f, l1_a2_f, l1_a0_b, l1_a2_b, l1_g_up, l1_k_k, l1_k_a, l1_r_k, l1_lnx_g, l1_lnx_b, l1_w_out, l1_ln1_g, l1_ln1_b, l1_ln2_g, l1_ln2_b, l1_v0, l1_v2, l1_router, l1_exp_w1, l1_exp_w3, l1_exp_w2):
    l0 = dict(zip(_LAYER0, (l0_w_ada, l0_b_ada, l0_w_in, l0_mu_prev, l0_mu_next, l0_pool_w, l0_pool_scale, l0_w0_f, l0_w2_f, l0_w0_b, l0_w2_b, l0_a0_f, l0_a2_f, l0_a0_b, l0_a2_b, l0_g_up, l0_k_k, l0_k_a, l0_r_k, l0_lnx_g, l0_lnx_b, l0_w_out, l0_ln1_g, l0_ln1_b, l0_ln2_g, l0_ln2_b, l0_ffn_w1, l0_ffn_w3, l0_ffn_w2)))
    l1 = dict(zip(_LAYER1, (l1_w_ada, l1_b_ada, l1_w_in, l1_mu_prev, l1_mu_next, l1_pool_w, l1_pool_scale, l1_w0_f, l1_w2_f, l1_w0_b, l1_w2_b, l1_a0_f, l1_a2_f, l1_a0_b, l1_a2_b, l1_g_up, l1_k_k, l1_k_a, l1_r_k, l1_lnx_g, l1_lnx_b, l1_w_out, l1_ln1_g, l1_ln1_b, l1_ln2_g, l1_ln2_b, l1_v0, l1_v2, l1_router, l1_exp_w1, l1_exp_w3, l1_exp_w2)))
    return _forward(x, c, ctx, c_ctx, [l0, l1], GRID_W)
```

```python
import functools
import math

import numpy as np
import jax
import jax.numpy as jnp
from jax import lax
from jax.experimental import pallas as pl
from jax.experimental.pallas import tpu as pltpu

F32 = jnp.float32
BF16 = jnp.bfloat16

GRID_W = 64
POOL_WINDOWS = (2, 4, 8, 16)
HEAD = 64
LANES = 128
SUBLANES = 8
HALO = 16
N_ADA = 6
LN_EPS = 1e-5
GN_EPS = 64e-5
NORM_EPS = 1e-12
CHUNK = 64
SCAN_PAIRS = 4
INV_PASSES = 1
VMEM_LIMIT = 56 * 1024 * 1024
COND_ROWS = 16


def _cparams(sem):
    return pltpu.CompilerParams(dimension_semantics=sem, vmem_limit_bytes=VMEM_LIMIT)


def _pick(n, target, mult):
    best = None
    for d in range(mult, min(n, target) + 1, mult):
        if n % d == 0:
            best = d
    return best if best is not None else n


def _dot(a, b):
    return jnp.dot(a, b, preferred_element_type=F32)


def _dot_nt(a, b):
    return lax.dot_general(a, b, (((1,), (1,)), ((), ())), preferred_element_type=F32)


def _split2(x):
    hi = x.astype(BF16)
    lo = (x - hi.astype(F32)).astype(BF16)
    return hi, lo


def _split3(x):
    hi = x.astype(BF16)
    r1 = x - hi.astype(F32)
    mid = r1.astype(BF16)
    lo = (r1 - mid.astype(F32)).astype(BF16)
    return hi, mid, lo


def _mm(a, b, passes=1):
    if passes == 1:
        return _dot(a.astype(BF16), b.astype(BF16))
    a_hi, a_lo = _split2(a)
    b_hi, b_lo = _split2(b)
    return _dot(a_hi, b_hi) + _dot(a_lo, b_hi) + _dot(a_hi, b_lo)


def _sigmoid(x):
    return 1.0 / (1.0 + jnp.exp(-x))


def _layer_norm_rows(x):
    mu = jnp.mean(x, axis=-1, keepdims=True)
    xc = x - mu
    var = jnp.mean(xc * xc, axis=-1, keepdims=True)
    return xc * lax.rsqrt(var + LN_EPS)


def _mod_kernel(c_ref, w_ref, b_ref, o_ref):
    c = c_ref[...]
    s = c * _sigmoid(c)
    o_ref[...] = _mm(s, w_ref[...], passes=3) + b_ref[...]


def _modulation(cond, w_ada, b_ada):
    d, n = w_ada.shape
    tn = _pick(n, 512, LANES)
    return pl.pallas_call(
        _mod_kernel,
        grid=(n // tn,),
        in_specs=[pl.BlockSpec((COND_ROWS, d), lambda j: (0, 0)),
                  pl.BlockSpec((d, tn), lambda j: (0, j)),
                  pl.BlockSpec((1, tn), lambda j: (0, j))],
        out_specs=pl.BlockSpec((COND_ROWS, tn), lambda j: (0, j)),
        out_shape=jax.ShapeDtypeStruct((COND_ROWS, n), F32),
        compiler_params=_cparams(("parallel",)),
        name="modulation",
    )(cond, w_ada, b_ada.reshape(1, n))


def _mod_row(mod_ref, row, chunk, d):
    return mod_ref[pl.ds(row, 1), chunk * d:(chunk + 1) * d]


def _ln_mod_kernel(x_ref, mod_ref, h_ref, *, nctx_blk, ctx_row, k_shift, k_scale):
    b, j = pl.program_id(0), pl.program_id(1)
    d = x_ref.shape[-1]
    row = jnp.where(j < nctx_blk, ctx_row, b)
    shift = _mod_row(mod_ref, row, k_shift, d)
    scale = _mod_row(mod_ref, row, k_scale, d)
    h_ref[...] = (_layer_norm_rows(x_ref[...]) * (1.0 + scale) + shift).astype(h_ref.dtype)


def _ln_mod(x_all, mod, *, lc, tr, ctx_row, k_shift, k_scale):
    bsz, t, d = x_all.shape
    kern = functools.partial(_ln_mod_kernel, nctx_blk=lc // tr, ctx_row=ctx_row,
                             k_shift=k_shift, k_scale=k_scale)
    return pl.pallas_call(
        kern,
        grid=(bsz, t // tr),
        in_specs=[pl.BlockSpec((None, tr, d), lambda b, j: (b, j, 0)),
                  pl.BlockSpec(mod.shape, lambda b, j: (0, 0))],
        out_specs=pl.BlockSpec((None, tr, d), lambda b, j: (b, j, 0)),
        out_shape=jax.ShapeDtypeStruct((bsz, t, d), BF16),
        compiler_params=_cparams(("parallel", "parallel")),
        name="ln_modulate",
    )(x_all, mod)


def _mm_kernel(a_ref, w_ref, o_ref, *scratch, nk):
    if nk == 1:
        o_ref[...] = _dot(a_ref[...], w_ref[...]).astype(o_ref.dtype)
        return
    acc_ref, = scratch
    k = pl.program_id(2)

    @pl.when(k == 0)
    def _():
        acc_ref[...] = jnp.zeros_like(acc_ref)

    acc_ref[...] += _dot(a_ref[...], w_ref[...])

    @pl.when(k == nk - 1)
    def _():
        o_ref[...] = acc_ref[...].astype(o_ref.dtype)


def _matmul(a, w, *, out_dtype, tm, tn, tk, name):
    m, kdim = a.shape
    n = w.shape[1]
    tm, tn, tk = _pick(m, tm, SUBLANES), _pick(n, tn, LANES), _pick(kdim, tk, LANES)
    nk = kdim // tk
    scratch = [pltpu.VMEM((tm, tn), F32)] if nk > 1 else []
    return pl.pallas_call(
        functools.partial(_mm_kernel, nk=nk),
        grid=(m // tm, n // tn, nk),
        in_specs=[pl.BlockSpec((tm, tk), lambda i, j, k: (i, k)),
                  pl.BlockSpec((tk, tn), lambda i, j, k: (k, j))],
        out_specs=pl.BlockSpec((tm, tn), lambda i, j, k: (i, j)),
        out_shape=jax.ShapeDtypeStruct((m, n), out_dtype),
        scratch_shapes=scratch,
        compiler_params=_cparams(("parallel", "parallel", "arbitrary")),
        name=name,
    )(a, w)


def _glu_kernel(a_ref, w1_ref, w3_ref, o_ref):
    a = a_ref[...]
    p1 = _dot(a, w1_ref[...])
    p3 = _dot(a, w3_ref[...])
    o_ref[...] = (p1 * _sigmoid(p1) * p3).astype(o_ref.dtype)


def _glu(a, w1, w3, *, tm, tn, name):
    m, kdim = a.shape
    f = w1.shape[1]
    tm, tn = _pick(m, tm, SUBLANES), _pick(f, tn, LANES)
    w_spec = pl.BlockSpec((kdim, tn), lambda i, j: (0, j))
    return pl.pallas_call(
        _glu_kernel,
        grid=(m // tm, f // tn),
        in_specs=[pl.BlockSpec((tm, kdim), lambda i, j: (i, 0)), w_spec, w_spec],
        out_specs=pl.BlockSpec((tm, tn), lambda i, j: (i, j)),
        out_shape=jax.ShapeDtypeStruct((m, f), BF16),
        compiler_params=_cparams(("parallel", "parallel")),
        name=name,
    )(a, w1, w3)


def _pool_consts(lc, tl, grid_w, tb):
    rows = tl // grid_w
    cb, c1, inv = [], [], []
    for win in POOL_WINDOWS:
        half = win // 2
        t = np.arange(tb)
        same_row = (t[:, None] // grid_w) == (t[None, :] // grid_w)
        dc = (t[None, :] % grid_w) - (t[:, None] % grid_w)
        cb.append((same_row & (dc >= -half) & (dc < half)).astype(np.float32))
        tc = np.arange(lc)
        d1 = tc[None, :] - tc[:, None]
        c1.append(((d1 >= -half) & (d1 < half)).astype(np.float32))
        cnt1 = np.minimum(tc + half, lc) - np.maximum(tc - half, 0)
        g = np.arange(grid_w)
        cntc = np.minimum(g + half, grid_w) - np.maximum(g - half, 0)
        r = np.arange(rows)
        cntr = np.minimum(r + half, rows) - np.maximum(r - half, 0)
        cnt2 = (cntr[:, None] * cntc[None, :]).reshape(-1)
        iv = 1.0 / np.concatenate([cnt1, cnt2]).astype(np.float64)
        inv.append(np.broadcast_to(iv[:, None], (lc + tl, LANES)).astype(np.float32))
    return (jnp.asarray(np.stack(cb), BF16), jnp.asarray(np.stack(c1), BF16),
            jnp.asarray(np.stack(inv), F32))


def _pool_kernel(u_ref, cb_ref, c1_ref, inv_ref, pw_ref, ps_ref, o_ref, s1_ref, acc_ref,
                 *, lc, tb, grid_w, half):
    t = u_ref.shape[0]
    tl = t - lc
    acc_ref[0:lc, :] = _dot(c1_ref[...], u_ref[0:lc, :])
    cb = cb_ref[...]
    for blk in range(tl // tb):
        lo, hi = lc + blk * tb, lc + (blk + 1) * tb
        s1_ref[lo:hi, :] = _dot(cb, u_ref[lo:hi, :])
    acc_ref[lc:t, :] = s1_ref[lc:t, :]
    for dr in range(-half, half):
        if dr == 0:
            continue
        sh = abs(dr) * grid_w
        if sh >= tl:
            continue
        if dr > 0:
            acc_ref[lc:t - sh, :] += s1_ref[lc + sh:t, :]
        else:
            acc_ref[lc + sh:t, :] += s1_ref[lc:t - sh, :]
    n_rep = u_ref.shape[1] // LANES
    diff = acc_ref[...] * jnp.concatenate([inv_ref[...]] * n_rep, axis=1) - u_ref[...].astype(F32)
    o_ref[...] = (_dot(diff.astype(BF16), pw_ref[...]) * ps_ref[...]).astype(o_ref.dtype)


def _pool_group(u_all, consts, pool_w, pool_scale, idx, *, lc, grid_w, tb, pg):
    bsz, t, _ = u_all.shape
    cb, c1, inv = consts
    half = POOL_WINDOWS[idx] // 2
    kern = functools.partial(_pool_kernel, lc=lc, tb=tb, grid_w=grid_w, half=half)
    return pl.pallas_call(
        kern,
        grid=(bsz,),
        in_specs=[pl.BlockSpec((None, t, pg), lambda b: (b, 0, idx)),
                  pl.BlockSpec((None, tb, tb), lambda b: (idx, 0, 0)),
                  pl.BlockSpec((None, lc, lc), lambda b: (idx, 0, 0)),
                  pl.BlockSpec((None, t, LANES), lambda b: (idx, 0, 0)),
                  pl.BlockSpec((None, pg, pg), lambda b: (idx, 0, 0)),
                  pl.BlockSpec((1, pg), lambda b: (0, idx))],
        out_specs=pl.BlockSpec((None, t, pg), lambda b: (b, 0, 0)),
        out_shape=jax.ShapeDtypeStruct((bsz, t, pg), BF16),
        scratch_shapes=[pltpu.VMEM((t, pg), F32), pltpu.VMEM((t, pg), F32)],
        compiler_params=_cparams(("parallel",)),
        name="pool_mixer_%d" % idx,
    )(u_all, cb, c1, inv, pool_w, pool_scale)


def _token_shift(f_ref, prev_ref, next_ref, mu_p, mu_n, seg_first, seg_last):
    f = f_ref[...].astype(F32)
    tr = f.shape[0]
    row = lax.broadcasted_iota(jnp.int32, f.shape, 0)
    prev_row = jnp.where(seg_first, 0.0, prev_ref[...].astype(F32)[HALO - 1:HALO, :])
    next_row = jnp.where(seg_last, 0.0, next_ref[...].astype(F32)[0:1, :])
    prev = jnp.where(row == 0, prev_row, pltpu.roll(f, 1, 0))
    nxt = jnp.where(row == tr - 1, next_row, pltpu.roll(f, tr - 1, 0))
    return f + mu_p * (prev - f) + mu_n * (nxt - f)


def _segment_flags(j, nctx_blk, n_blk):
    seg_first = jnp.logical_or(j == 0, j == nctx_blk)
    seg_last = jnp.logical_or(j == nctx_blk - 1, j == n_blk - 1)
    return seg_first, seg_last


def _shift_specs(tr, tc, col_blk, t):
    per_blk = tr // HALO
    last = t // HALO - 1
    cur = pl.BlockSpec((None, tr, tc), lambda b, j, c: (b, j, col_blk(c)))
    prv = pl.BlockSpec((None, HALO, tc),
                       lambda b, j, c: (b, jnp.maximum(j * per_blk - 1, 0), col_blk(c)))
    nxt = pl.BlockSpec((None, HALO, tc),
                       lambda b, j, c: (b, jnp.minimum((j + 1) * per_blk, last), col_blk(c)))
    return [cur, prv, nxt]


def _lora_kernel(u_ref, up_ref, un_ref, mu_ref, o_ref, *, nctx_blk, n_blk, lr, gl):
    j = pl.program_id(1)
    seg_first, seg_last = _segment_flags(j, nctx_blk, n_blk)
    z = _token_shift(u_ref, up_ref, un_ref, mu_ref[0:1, :], mu_ref[1:2, :], seg_first, seg_last)
    col = lax.broadcasted_iota(jnp.int32, z.shape, 1)
    act = jnp.where(col < 2 * lr, jnp.tanh(z),
                    jnp.where(jnp.logical_and(col >= 4 * lr, col < 4 * lr + gl), _sigmoid(z), z))
    o_ref[...] = act.astype(o_ref.dtype)


def _lora_act(u_all, mu_lora, *, lc, tr, col0, wl, lr, gl):
    bsz, t, _ = u_all.shape
    specs = _shift_specs(tr, wl, lambda c: col0 // wl, t)
    kern = functools.partial(_lora_kernel, nctx_blk=lc // tr, n_blk=t // tr, lr=lr, gl=gl)
    return pl.pallas_call(
        kern,
        grid=(bsz, t // tr, 1),
        in_specs=specs + [pl.BlockSpec((SUBLANES, wl), lambda b, j, c: (0, 0))],
        out_specs=pl.BlockSpec((None, tr, wl), lambda b, j, c: (b, j, 0)),
        out_shape=jax.ShapeDtypeStruct((bsz, t, wl), BF16),
        compiler_params=_cparams(("parallel", "parallel", "arbitrary")),
        name="lora_act",
    )(u_all, u_all, u_all, mu_lora)


def _head_ones():
    r = lax.broadcasted_iota(jnp.int32, (LANES, LANES), 0)
    c = lax.broadcasted_iota(jnp.int32, (LANES, LANES), 1)
    return jnp.where((r ^ c) < HEAD, 1.0, 0.0).astype(BF16)


def _head_sum(x, ones):
    outs = []
    for s in range(x.shape[1] // LANES):
        hi, mid, lo = _split3(x[:, s * LANES:(s + 1) * LANES])
        outs.append(_dot(hi, ones) + _dot(mid, ones) + _dot(lo, ones))
    return outs[0] if len(outs) == 1 else jnp.concatenate(outs, axis=1)


(_V_MPR, _V_MNR, _V_MPK, _V_MNK, _V_MPV, _V_MNV, _V_W0F, _V_W0B, _V_A0F, _V_A0B,
 _V_V0, _V_KK, _V_KA, _V_RK, _V_LG, _V_LB) = range(16)
_DECAY_SCALE = math.exp(-0.5)


def _terms_kernel(*refs, nctx_blk, n_blk, lr, gl, has_vres):
    (ur, urp, urn, uk, ukp, ukn, uv, uvp, uvn, act_ref, vec_ref,
     w2f_ref, w2b_ref, a2f_ref, a2b_ref, gup_ref) = refs[:16]
    pos = 16
    if has_vres:
        v2_ref, vfirst_ref = refs[pos:pos + 2]
        pos += 2
    (r_o, v_o, kk_o, lwf_o, lwb_o, kf_o, kb_o, af_o, ab_o, gate_o, bonus_o) = refs[pos:]

    j = pl.program_id(1)
    seg_first, seg_last = _segment_flags(j, nctx_blk, n_blk)
    vec = lambda i: vec_ref[i:i + 1, :]
    r = _token_shift(ur, urp, urn, vec(_V_MPR), vec(_V_MNR), seg_first, seg_last)
    k = _token_shift(uk, ukp, ukn, vec(_V_MPK), vec(_V_MNK), seg_first, seg_last)
    v = _token_shift(uv, uvp, uvn, vec(_V_MPV), vec(_V_MNV), seg_first, seg_last)

    act = act_ref[...]
    a_wf, a_wb = act[:, 0:lr], act[:, lr:2 * lr]
    a_af, a_ab = act[:, 2 * lr:3 * lr], act[:, 3 * lr:4 * lr]
    a_g = act[:, 4 * lr:4 * lr + gl]
    if has_vres:
        a_v = act[:, 4 * lr + gl:5 * lr + gl]
        v = v + (vfirst_ref[...].astype(F32) - v) * _sigmoid(vec(_V_V0) + _dot(a_v, v2_ref[...]))

    lwf_o[...] = -_DECAY_SCALE * _sigmoid(vec(_V_W0F) + _dot(a_wf, w2f_ref[...]))
    lwb_o[...] = -_DECAY_SCALE * _sigmoid(vec(_V_W0B) + _dot(a_wb, w2b_ref[...]))
    a_f = _sigmoid(vec(_V_A0F) + _dot(a_af, a2f_ref[...]))
    a_b = _sigmoid(vec(_V_A0B) + _dot(a_ab, a2b_ref[...]))

    ones = _head_ones()
    kkr = k * vec(_V_KK)
    norm = jnp.sqrt(_head_sum(kkr * kkr, ones))
    kk_o[...] = (kkr / jnp.maximum(norm, NORM_EPS)).astype(kk_o.dtype)
    k_f = k * (1.0 + (a_f - 1.0) * vec(_V_KA))
    k_b = k * (1.0 + (a_b - 1.0) * vec(_V_KA))
    bonus_o[...] = (_head_sum(r * (k_f + k_b) * vec(_V_RK), ones) * v).astype(bonus_o.dtype)
    gate_o[...] = _dot(a_g, gup_ref[...]).astype(gate_o.dtype)
    r_o[...] = r.astype(r_o.dtype)
    v_o[...] = v.astype(v_o.dtype)
    kf_o[...] = k_f.astype(kf_o.dtype)
    kb_o[...] = k_b.astype(kb_o.dtype)
    af_o[...] = a_f.astype(af_o.dtype)
    ab_o[...] = a_b.astype(ab_o.dtype)


def _rwkv_terms(u_all, act, vec, lw, v_first, *, lc, tr, tc, pool_w, rw, lr, gl):
    bsz, t, _ = u_all.shape
    has_vres = v_first is not None
    nb = rw // tc
    specs = []
    for part in range(3):
        base = (pool_w + part * rw) // tc
        specs += _shift_specs(tr, tc, (lambda c, base=base: base + c), t)
    specs.append(pl.BlockSpec((None, tr, act.shape[-1]), lambda b, j, c: (b, j, 0)))
    specs.append(pl.BlockSpec((vec.shape[0], tc), lambda b, j, c: (0, c)))
    wspec = lambda rows: pl.BlockSpec((rows, tc), lambda b, j, c: (0, c))
    specs += [wspec(lr), wspec(lr), wspec(lr), wspec(lr), wspec(gl)]
    args = [u_all] * 9 + [act, vec, lw["w2_f"], lw["w2_b"], lw["a2_f"], lw["a2_b"], lw["g_up"]]
    if has_vres:
        specs += [wspec(lr), pl.BlockSpec((None, tr, tc), lambda b, j, c: (b, j, c))]
        args += [lw["v2"], v_first]
    out_spec = pl.BlockSpec((None, tr, tc), lambda b, j, c: (b, j, c))
    kern = functools.partial(_terms_kernel, nctx_blk=lc // tr, n_blk=t // tr, lr=lr, gl=gl,
                             has_vres=has_vres)
    names = ("r", "v", "kk", "lw_f", "lw_b", "k_f", "k_b", "a_f", "a_b", "gate", "bonus")
    dtypes = [F32 if n.startswith("lw") else BF16 for n in names]
    outs = pl.pallas_call(
        kern,
        grid=(bsz, t // tr, nb),
        in_specs=specs,
        out_specs=[out_spec] * len(names),
        out_shape=[jax.ShapeDtypeStruct((bsz, t, rw), dt) for dt in dtypes],
        compiler_params=_cparams(("parallel", "parallel", "parallel")),
        name="rwkv_terms",
    )(*args)
    return dict(zip(names, outs))


def _cumsum_rows(x, rev):
    n = x.shape[0]
    row = lax.broadcasted_iota(jnp.int32, x.shape, 0)
    s = 1
    while s < n:
        if rev:
            x = x + jnp.where(row < n - s, pltpu.roll(x, n - s, 0), 0.0)
        else:
            x = x + jnp.where(row >= s, pltpu.roll(x, s, 0), 0.0)
        s *= 2
    return x


def _wkv_chunks(chains):
    n = chains[0][0].shape[0]
    hp = 2 * n
    lane = lax.broadcasted_iota(jnp.int32, (n, LANES), 1)
    first_head = lane < HEAD
    ri = lax.broadcasted_iota(jnp.int32, (hp, hp), 0)
    ci = lax.broadcasted_iota(jnp.int32, (hp, hp), 1)
    blk = ri ^ ci
    eye = ri == ci
    tdiff = (ci & (n - 1)) - (ri & (n - 1))
    dist = {False: jnp.where(blk < n, tdiff, hp), True: jnp.where(blk < n, -tdiff, hp)}

    def stack(x):
        return jnp.concatenate([jnp.where(first_head, x, 0.0), jnp.where(first_head, 0.0, x)], axis=0)

    pre = []
    for r, v, kk, lw, k, a, state, rev in chains:
        c = _cumsum_rows(lw, rev)
        ctot = c[0:1, :] if rev else c[n - 1:n, :]
        e_pos, e_neg = jnp.exp(c), jnp.exp(-c)
        e_prev, e_rem = jnp.exp(c - lw), jnp.exp(ctot - c)
        kka = kk * a
        bt, kt = kka * e_neg, k * e_neg
        pre.append(dict(s_at=stack(-kk * e_prev), s_rt=stack(r * e_pos), s_v=stack(v), rt=r * e_pos,
                        rhs1=jnp.concatenate([bt, bt, kt, kt], axis=0).astype(BF16),
                        lhs_t=jnp.concatenate([stack(kka * e_rem), stack(k * e_rem)], axis=0),
                        wtot=jnp.exp(ctot), strict=dist[rev] < 0, incl=dist[rev] <= 0))

    s1 = [_dot_nt(jnp.concatenate([p["s_at"], p["s_rt"]], axis=0).astype(BF16), p["rhs1"]) for p in pre]
    nmat = [jnp.where(p["strict"], s[:hp, :hp], 0.0) for p, s in zip(pre, s1)]
    m_ak = [jnp.where(p["strict"], s[:hp, hp:], 0.0) for p, s in zip(pre, s1)]
    m_rbk = [jnp.concatenate([jnp.where(p["incl"], s[hp:, :hp], 0.0), jnp.where(p["incl"], s[hp:, hp:], 0.0)],
                             axis=1).astype(BF16) for p, s in zip(pre, s1)]
    z = [_mm(m, p["s_v"]) for m, p in zip(m_ak, pre)]

    n8 = [jnp.where(blk < SUBLANES, m, 0.0) for m in nmat]
    tmat = [jnp.where(eye, 1.0, m) for m in n8]
    n2 = [_mm(m, m, INV_PASSES) for m in n8]
    st = [_mm(jnp.concatenate([t, m], axis=0), m, INV_PASSES) for t, m in zip(tmat, n2)]
    tmat = [t + s[:hp] for t, s in zip(tmat, st)]
    tmat = [t + _mm(t, s[hp:], INV_PASSES) for t, s in zip(tmat, st)]
    size = SUBLANES
    while size < n:
        sel = jnp.logical_and(blk >= size, blk < 2 * size)
        x = [_mm(jnp.where(sel, m, 0.0), t, INV_PASSES) for m, t in zip(nmat, tmat)]
        tmat = [t + _mm(t, xx, INV_PASSES) for t, xx in zip(tmat, x)]
        size *= 2

    gu = [_mm(t, jnp.concatenate([p["s_at"], zz], axis=1)) for t, p, zz in zip(tmat, pre, z)]
    zeros = jnp.zeros((hp, LANES), F32)
    rhs4 = [jnp.concatenate([g, jnp.concatenate([zeros, p["s_v"]], axis=1)], axis=0).astype(BF16)
            for g, p in zip(gu, pre)]
    top = [_dot(m, rr) for m, rr in zip(m_rbk, rhs4)]
    bot = [_dot(p["lhs_t"].T.astype(BF16), rr) for p, rr in zip(pre, rhs4)]

    outs = []
    for p, tp, bt_, chain in zip(pre, top, bot, chains):
        q = p["rt"] + tp[:n, :LANES] + tp[n:, :LANES]
        amat = jnp.where(eye, jnp.broadcast_to(p["wtot"], (LANES, LANES)), 0.0) + bt_[:, :LANES]
        qa = jnp.concatenate([q, amat], axis=0).astype(BF16)
        s_hi, s_lo = _split2(chain[6])
        outs.append(_dot(qa, s_hi) + _dot(qa, s_lo))
    return [(o[:n] + tp[:n, LANES:] + tp[n:, LANES:], o[n:] + bt_[:, LANES:])
            for o, tp, bt_ in zip(outs, top, bot)]


def _scan_kernel(*refs, pairs):
    ins, (yf_ref, yb_ref, state_ref) = refs[:12], refs[12:]
    s = pl.program_id(1)

    @pl.when(s == 0)
    def _():
        state_ref[...] = jnp.zeros_like(state_ref)

    chains = [(d, p, slice(p * LANES, (p + 1) * LANES)) for d in range(2) for p in range(pairs)]
    loaded = [tuple(ref[:, cols].astype(F32) for ref in ins[6 * d:6 * d + 6]) + (state_ref[d, p], bool(d))
              for d, p, cols in chains]
    for (d, p, cols), (y, new_state) in zip(chains, _wkv_chunks(loaded)):
        y_ref = yb_ref if d else yf_ref
        y_ref[:, cols] = y.astype(y_ref.dtype)
        state_ref[d, p] = new_state


def _wkv_scan(tm, *, lc, chunk, pairs):
    bsz, t, rw = tm["r"].shape
    width = pairs * LANES
    ngrp = rw // width
    nctx, ntot = lc // chunk, t // chunk

    def fwd(g, s):
        return (g // ngrp, s, g % ngrp)

    def bwd(g, s):
        return (g // ngrp, jnp.where(s < nctx, nctx - 1 - s, ntot - 1 - (s - nctx)), g % ngrp)

    blk = (None, chunk, width)
    names_f = ("r", "v", "kk", "lw_f", "k_f", "a_f")
    names_b = ("r", "v", "kk", "lw_b", "k_b", "a_b")
    in_specs = [pl.BlockSpec(blk, fwd)] * 6 + [pl.BlockSpec(blk, bwd)] * 6
    args = [tm[n] for n in names_f] + [tm[n] for n in names_b]
    return pl.pallas_call(
        functools.partial(_scan_kernel, pairs=pairs),
        grid=(bsz * ngrp, ntot),
        in_specs=in_specs,
        out_specs=[pl.BlockSpec(blk, fwd), pl.BlockSpec(blk, bwd)],
        out_shape=[jax.ShapeDtypeStruct((bsz, t, rw), BF16)] * 2,
        scratch_shapes=[pltpu.VMEM((2, pairs, LANES, LANES), F32)],
        compiler_params=_cparams(("parallel", "arbitrary")),
        name="wkv7_scan",
    )(*args)


def _rwkv_out_kernel(yf_ref, yb_ref, bonus_ref, gate_ref, vec_ref, o_ref):
    y = yf_ref[...].astype(F32) + yb_ref[...].astype(F32)
    ones = _head_ones()
    inv_n = 1.0 / HEAD
    mu = _head_sum(y, ones) * inv_n
    yc = y - mu
    var = _head_sum(yc * yc, ones) * inv_n
    yn = yc * lax.rsqrt(var + GN_EPS)
    yn = yn * vec_ref[_V_LG:_V_LG + 1, :] + vec_ref[_V_LB:_V_LB + 1, :]
    o_ref[...] = ((yn + bonus_ref[...].astype(F32)) * gate_ref[...].astype(F32)).astype(o_ref.dtype)


def _rwkv_out(y_f, y_b, bonus, gate, vec, *, tr, tc):
    bsz, t, rw = y_f.shape
    spec = pl.BlockSpec((None, tr, tc), lambda b, j, c: (b, j, c))
    return pl.pallas_call(
        _rwkv_out_kernel,
        grid=(bsz, t // tr, rw // tc),
        in_specs=[spec] * 4 + [pl.BlockSpec((vec.shape[0], tc), lambda b, j, c: (0, c))],
        out_specs=spec,
        out_shape=jax.ShapeDtypeStruct((bsz, t, rw), BF16),
        compiler_params=_cparams(("parallel", "parallel", "parallel")),
        name="rwkv_out",
    )(y_f, y_b, bonus, gate, vec)


_R_E1, _R_E2, _R_G1, _R_G2 = range(4)

def _res_kernel(*refs, alpha, row_off_blk, nctx_blk, ctx_row, k_gate, k_shift, k_scale,
                with_mod, n_exp):
    x_ref, y_ref, mod_ref, gb_ref = refs[:4]
    pos = 4
    if n_exp:
        router_ref = refs[pos]
        pos += 1
    outs = refs[pos:]
    b, j = pl.program_id(0), pl.program_id(1)
    d = x_ref.shape[-1]
    row = jnp.where(j + row_off_blk < nctx_blk, ctx_row, b)
    gate = _mod_row(mod_ref, row, k_gate, d)
    xn = _layer_norm_rows(alpha * x_ref[...] + gate * y_ref[...]) * gb_ref[0:1, :] + gb_ref[1:2, :]
    outs[0][...] = xn
    if not with_mod:
        return
    h = _layer_norm_rows(xn) * (1.0 + _mod_row(mod_ref, row, k_scale, d)) + _mod_row(mod_ref, row, k_shift, d)
    outs[1][...] = h.astype(outs[1].dtype)
    if not n_exp:
        return
    logits = _mm(h, router_ref[...], passes=3)
    lane = lax.broadcasted_iota(jnp.int32, logits.shape, 1).astype(F32)
    neg = -jnp.inf
    logits = jnp.where(lane < n_exp, logits, neg)
    m1 = jnp.max(logits, axis=-1, keepdims=True)
    i1 = jnp.min(jnp.where(logits == m1, lane, float(LANES)), axis=-1, keepdims=True)
    rest = jnp.where(lane == i1, neg, logits)
    m2 = jnp.max(rest, axis=-1, keepdims=True)
    i2 = jnp.min(jnp.where(rest == m2, lane, float(LANES)), axis=-1, keepdims=True)
    e2 = jnp.exp(m2 - m1)
    g1 = 1.0 / (1.0 + e2)
    g2 = e2 / (1.0 + e2)
    outs[2][...] = jnp.where(lane == _R_E1, i1, jnp.where(lane == _R_E2, i2,
                             jnp.where(lane == _R_G1, g1, jnp.where(lane == _R_G2, g2, 0.0))))


def _residual_ln(x, y, mod, gain, bias, *, alpha, tr, lc, ctx_row, k_gate, x_off=0, y_off=0,
                 rows=None, mod2=None, router=None, h_dtype=BF16):
    bsz, _, d = x.shape
    rows = x.shape[1] if rows is None else rows
    n_exp = 0 if router is None else router.shape[1]
    gb = jnp.stack([gain, bias])
    in_specs = [pl.BlockSpec((None, tr, d), lambda b, j: (b, j + x_off // tr, 0)),
                pl.BlockSpec((None, tr, d), lambda b, j: (b, j + y_off // tr, 0)),
                pl.BlockSpec(mod.shape, lambda b, j: (0, 0)),
                pl.BlockSpec((2, d), lambda b, j: (0, 0))]
    args = [x, y, mod, gb]
    out_specs = [pl.BlockSpec((None, tr, d), lambda b, j: (b, j, 0))]
    out_shape = [jax.ShapeDtypeStruct((bsz, rows, d), F32)]
    if mod2 is not None:
        out_specs.append(pl.BlockSpec((None, tr, d), lambda b, j: (b, j, 0)))
        out_shape.append(jax.ShapeDtypeStruct((bsz, rows, d), h_dtype))
    if router is not None:
        rpad = jnp.zeros((d, LANES), F32).at[:, :n_exp].set(router)
        in_specs.append(pl.BlockSpec((d, LANES), lambda b, j: (0, 0)))
        args.append(rpad)
        out_specs.append(pl.BlockSpec((None, tr, LANES), lambda b, j: (b, j, 0)))
        out_shape.append(jax.ShapeDtypeStruct((bsz, rows, LANES), F32))
    k_shift, k_scale = mod2 if mod2 is not None else (0, 0)
    kern = functools.partial(_res_kernel, alpha=alpha, row_off_blk=x_off // tr, nctx_blk=lc // tr,
                             ctx_row=ctx_row, k_gate=k_gate, k_shift=k_shift, k_scale=k_scale,
                             with_mod=mod2 is not None, n_exp=n_exp)
    return pl.pallas_call(
        kern,
        grid=(bsz, rows // tr),
        in_specs=in_specs,
        out_specs=out_specs,
        out_shape=out_shape,
        compiler_params=_cparams(("parallel", "parallel")),
        name="residual_ln",
    )(*args)


def _route_plan(route, n_exp, tile):
    n = route.shape[0]
    e_flat = jnp.concatenate([route[:, _R_E1], route[:, _R_E2]]).astype(jnp.int32)
    experts = jnp.arange(n_exp, dtype=jnp.int32)
    onehot = (e_flat[:, None] == experts[None, :]).astype(jnp.int32)
    csum = jnp.cumsum(onehot, axis=0)
    rank = jnp.sum((csum - onehot) * onehot, axis=1)
    tiles_e = (csum[-1] + tile - 1) // tile
    tile_end = jnp.cumsum(tiles_e)
    tile_start = tile_end - tiles_e
    pos = jnp.sum(onehot * tile_start[None, :], axis=1) * tile + rank
    n_tiles = (2 * n) // tile + n_exp
    tidx = jnp.arange(n_tiles, dtype=jnp.int32)
    valid = tidx < tile_end[-1]
    last_e = jnp.max(jnp.where(tiles_e > 0, experts, 0))
    tile_expert = jnp.sum((tidx[:, None] >= tile_end[None, :]).astype(jnp.int32), axis=1)
    tile_expert = jnp.where(valid, tile_expert, last_e)
    token = jnp.arange(2 * n, dtype=jnp.int32) % n
    src_token = jnp.zeros((n_tiles * tile,), jnp.int32).at[pos].set(token)
    return src_token, pos, tile_expert, valid.astype(jnp.int32)


def _gather_kernel(idx_ref, src_ref, out_ref, sem):
    rows = out_ref.shape[0]

    def row_copy(j, src_row):
        return pltpu.make_async_copy(src_ref.at[pl.ds(src_row, 1)], out_ref.at[pl.ds(j, 1)], sem)

    def start(j, carry):
        row_copy(j, idx_ref[0, j]).start()
        return carry

    def wait(j, carry):
        row_copy(j, 0).wait()
        return carry

    lax.fori_loop(0, rows, start, 0, unroll=8)
    lax.fori_loop(0, rows, wait, 0, unroll=8)


def _gather_rows(src, idx, *, rows):
    n_out = idx.shape[0]
    d = src.shape[1]
    rows = _pick(n_out, rows, SUBLANES)
    return pl.pallas_call(
        _gather_kernel,
        grid=(n_out // rows,),
        in_specs=[pl.BlockSpec((None, 1, rows), lambda i: (i, 0, 0), memory_space=pltpu.SMEM),
                  pl.BlockSpec(memory_space=pl.ANY)],
        out_specs=pl.BlockSpec((rows, d), lambda i: (i, 0)),
        out_shape=jax.ShapeDtypeStruct((n_out, d), src.dtype),
        scratch_shapes=[pltpu.SemaphoreType.DMA(())],
        compiler_params=_cparams(("arbitrary",)),
        name="row_gather",
    )(idx.reshape(n_out // rows, 1, rows), src)


def _moe_glu_kernel(te_ref, tv_ref, a_ref, w1_ref, w3_ref, o_ref):
    i = pl.program_id(1)

    @pl.when(tv_ref[i] != 0)
    def _():
        a = a_ref[...].astype(BF16)
        p1 = _dot(a, w1_ref[...])
        p3 = _dot(a, w3_ref[...])
        o_ref[...] = (p1 * _sigmoid(p1) * p3).astype(o_ref.dtype)

    @pl.when(tv_ref[i] == 0)
    def _():
        o_ref[...] = jnp.zeros_like(o_ref)


def _moe_down_kernel(te_ref, tv_ref, h_ref, w2_ref, o_ref):
    i = pl.program_id(1)

    @pl.when(tv_ref[i] != 0)
    def _():
        o_ref[...] = _dot(h_ref[...], w2_ref[...])

    @pl.when(tv_ref[i] == 0)
    def _():
        o_ref[...] = jnp.zeros_like(o_ref)


def _moe_experts(xs, w1, w3, w2, tile_expert, tile_valid, *, tile, tn_up, tn_down):
    r, d = xs.shape
    _, _, f = w1.shape
    n_tiles = r // tile
    tn_up, tn_down = _pick(f, tn_up, LANES), _pick(d, tn_down, LANES)
    up_spec = pl.BlockSpec((None, d, tn_up), lambda j, i, te, tv: (te[i], 0, j))
    hid = pl.pallas_call(
        _moe_glu_kernel,
        grid_spec=pltpu.PrefetchScalarGridSpec(
            num_scalar_prefetch=2, grid=(f // tn_up, n_tiles),
            in_specs=[pl.BlockSpec((tile, d), lambda j, i, te, tv: (i, 0)), up_spec, up_spec],
            out_specs=pl.BlockSpec((tile, tn_up), lambda j, i, te, tv: (i, j))),
        out_shape=jax.ShapeDtypeStruct((r, f), BF16),
        compiler_params=_cparams(("parallel", "arbitrary")),
        name="moe_glu",
    )(tile_expert, tile_valid, xs, w1, w3)
    return pl.pallas_call(
        _moe_down_kernel,
        grid_spec=pltpu.PrefetchScalarGridSpec(
            num_scalar_prefetch=2, grid=(d // tn_down, n_tiles),
            in_specs=[pl.BlockSpec((tile, f), lambda j, i, te, tv: (i, 0)),
                      pl.BlockSpec((None, f, tn_down), lambda j, i, te, tv: (te[i], 0, j))],
            out_specs=pl.BlockSpec((tile, tn_down), lambda j, i, te, tv: (i, j))),
        out_shape=jax.ShapeDtypeStruct((r, d), F32),
        compiler_params=_cparams(("parallel", "arbitrary")),
        name="moe_down",
    )(tile_expert, tile_valid, hid, w2)


def _moe_out_kernel(x_ref, y1_ref, y2_ref, route_ref, mod_ref, gb_ref, o_ref, *, alpha, k_gate):
    b = pl.program_id(0)
    d = x_ref.shape[-1]
    route = route_ref[...]
    y = route[:, _R_G1:_R_G1 + 1] * y1_ref[...] + route[:, _R_G2:_R_G2 + 1] * y2_ref[...]
    gate = _mod_row(mod_ref, b, k_gate, d)
    o_ref[...] = _layer_norm_rows(alpha * x_ref[...] + gate * y) * gb_ref[0:1, :] + gb_ref[1:2, :]


def _moe_residual_ln(x, yg, route, mod, gain, bias, *, alpha, tr, k_gate):
    bsz, rows, d = x.shape
    row_spec = pl.BlockSpec((None, tr, d), lambda b, j: (b, j, 0))
    return pl.pallas_call(
        functools.partial(_moe_out_kernel, alpha=alpha, k_gate=k_gate),
        grid=(bsz, rows // tr),
        in_specs=[row_spec,
                  pl.BlockSpec((None, None, tr, d), lambda b, j: (0, b, j, 0)),
                  pl.BlockSpec((None, None, tr, d), lambda b, j: (1, b, j, 0)),
                  pl.BlockSpec((None, tr, LANES), lambda b, j: (b, j, 0)),
                  pl.BlockSpec(mod.shape, lambda b, j: (0, 0)),
                  pl.BlockSpec((2, d), lambda b, j: (0, 0))],
        out_specs=row_spec,
        out_shape=jax.ShapeDtypeStruct((bsz, rows, d), F32),
        compiler_params=_cparams(("parallel", "parallel")),
        name="moe_residual_ln",
    )(x, yg, yg, route, mod, jnp.stack([gain, bias]))


def _pad_rows(w, rows):
    return jnp.zeros((rows,) + w.shape[1:], w.dtype).at[:w.shape[0]].set(w)


def _layer_layout(p, pool_w, rw):
    lr_raw = p["w2_f"].shape[0]
    gl_raw = p["g_up"].shape[0]
    vr_raw = p["v2"].shape[0] if "v2" in p else 0
    lr = -(-max(lr_raw, vr_raw, 1) // LANES) * LANES
    gl = -(-gl_raw // LANES) * LANES
    wl = 5 * lr + gl
    wl_pad = -(-wl // 1024) * 1024 if wl > 512 else wl
    core = pool_w + 3 * rw
    sizes = [lr_raw] * 4 + [gl_raw] + ([vr_raw] if vr_raw else [])
    slots = [lr] * 4 + [gl] + [lr]
    d = p["w_in"].shape[0]

    def relayout(src, rows_shape):
        dst = jnp.zeros(rows_shape + (core + wl_pad,), src.dtype)
        dst = dst.at[..., :core].set(src[..., :core])
        s_off, d_off = core, core
        for size, slot in zip(sizes, slots):
            dst = dst.at[..., d_off:d_off + size].set(src[..., s_off:s_off + size])
            s_off += size
            d_off += slot
        return dst

    w_in = relayout(p["w_in"], (d,)).astype(BF16)
    zero_pool = jnp.zeros((pool_w,), F32)
    mu_p = relayout(jnp.concatenate([zero_pool, p["mu_prev"]]), ())
    mu_n = relayout(jnp.concatenate([zero_pool, p["mu_next"]]), ())
    mu_lora = jnp.zeros((SUBLANES, wl_pad), F32).at[0].set(mu_p[core:]).at[1].set(mu_n[core:])
    seg = lambda a, i: a[pool_w + i * rw:pool_w + (i + 1) * rw]
    zeros = jnp.zeros((rw,), F32)
    vec = jnp.stack([seg(mu_p, 0), seg(mu_n, 0), seg(mu_p, 1), seg(mu_n, 1), seg(mu_p, 2), seg(mu_n, 2),
                     p["w0_f"], p["w0_b"], p["a0_f"], p["a0_b"], p.get("v0", zeros),
                     p["k_k"], p["k_a"], p["r_k"].reshape(-1), p["lnx_g"], p["lnx_b"]])
    lw = {n: _pad_rows(p[n], lr).astype(BF16) for n in ("w2_f", "w2_b", "a2_f", "a2_b")}
    lw["g_up"] = _pad_rows(p["g_up"], gl).astype(BF16)
    if vr_raw:
        lw["v2"] = _pad_rows(p["v2"], lr).astype(BF16)
    return dict(w_in=w_in, mu_lora=mu_lora, vec=vec, lw=lw, lr=lr, gl=gl, wl=wl_pad, core=core)


def _mixer(x_all, mod, p, lay, v_first, pool_consts, *, lc, tr, tc, grid_w, tb, ctx_row):
    bsz, t, d = x_all.shape
    mix_w = p["w_out"].shape[0]
    pool_w = mix_w // 4
    pg = pool_w // len(POOL_WINDOWS)
    rw = mix_w - pool_w
    h = _ln_mod(x_all, mod, lc=lc, tr=tr, ctx_row=ctx_row, k_shift=0, k_scale=1)
    u = _matmul(h.reshape(bsz * t, d), lay["w_in"], out_dtype=BF16, tm=1024, tn=1024, tk=d,
                name="in_proj").reshape(bsz, t, -1)
    pool_wts = p["pool_w"].astype(BF16)
    pool_scale = p["pool_scale"].reshape(1, pool_w)
    pools = [_pool_group(u, pool_consts, pool_wts, pool_scale, i, lc=lc, grid_w=grid_w, tb=tb, pg=pg)
             for i in range(len(POOL_WINDOWS))]
    act = _lora_act(u, lay["mu_lora"], lc=lc, tr=tr, col0=lay["core"], wl=lay["wl"], lr=lay["lr"],
                    gl=lay["gl"])
    tm = _rwkv_terms(u, act, lay["vec"], lay["lw"], v_first, lc=lc, tr=tr, tc=tc, pool_w=pool_w,
                     rw=rw, lr=lay["lr"], gl=lay["gl"])
    y_f, y_b = _wkv_scan(tm, lc=lc, chunk=CHUNK, pairs=SCAN_PAIRS)
    out = _rwkv_out(y_f, y_b, tm["bonus"], tm["gate"], lay["vec"], tr=tr, tc=tc)
    mix_in = jnp.concatenate(pools + [out], axis=-1)
    mix = _matmul(mix_in.reshape(bsz * t, mix_w), p["w_out"].astype(BF16), out_dtype=F32,
                  tm=1024, tn=1024, tk=mix_w, name="out_proj").reshape(bsz, t, d)
    return mix, tm["v"]


def _forward(x, c, ctx, c_ctx, layers, grid_w):
    bsz, seq, d = x.shape
    lc = ctx.shape[1]
    depth = len(layers)
    alpha = (2 * depth) ** 0.25
    assert bsz < COND_ROWS and lc % CHUNK == 0 and seq % CHUNK == 0 and seq % grid_w == 0
    tr = _pick(math.gcd(lc, seq), 256, SUBLANES)
    tb = tr if tr % grid_w == 0 else grid_w
    assert seq % tb == 0 and tb % grid_w == 0
    ctx_row = bsz
    cond = jnp.zeros((COND_ROWS, d), F32).at[:bsz].set(c).at[ctx_row].set(c_ctx)
    x_all = jnp.concatenate([ctx, x], axis=1)
    pool_consts = _pool_consts(lc, seq, grid_w, tb)
    common = dict(tr=tr, lc=lc, ctx_row=ctx_row)
    v_first = None
    for i, p in enumerate(layers):
        last = i == depth - 1
        mix_w = p["w_out"].shape[0]
        pool_w = mix_w // 4
        rw = mix_w - pool_w
        assert pool_w % (len(POOL_WINDOWS) * LANES) == 0 and rw % (SCAN_PAIRS * LANES) == 0
        tc = _pick(rw, 512, LANES)
        lay = _layer_layout(p, pool_w, rw)
        mod = _modulation(cond, p["w_ada"], p["b_ada"])
        mix, v_cur = _mixer(x_all, mod, p, lay, v_first, pool_consts, tc=tc, grid_w=grid_w, tb=tb,
                            **common)
        if v_first is None:
            v_first = v_cur
        if not last:
            x1, h2 = _residual_ln(x_all, mix, mod, p["ln1_g"], p["ln1_b"], alpha=alpha, k_gate=2,
                                  mod2=(3, 4), **common)
            t = x_all.shape[1]
            hid = _glu(h2.reshape(bsz * t, d), p["ffn_w1"].astype(BF16), p["ffn_w3"].astype(BF16),
                       tm=1024, tn=256, name="ffn_glu")
            ffn = _matmul(hid, p["ffn_w2"].astype(BF16), out_dtype=F32, tm=512, tn=1024, tk=5504,
                          name="ffn_down").reshape(bsz, t, d)
            x_all, = _residual_ln(x1, ffn, mod, p["ln2_g"], p["ln2_b"], alpha=alpha, k_gate=5, **common)
        else:
            x1, h2, route = _residual_ln(x_all, mix, mod, p["ln1_g"], p["ln1_b"], alpha=alpha, k_gate=2,
                                         mod2=(3, 4), router=p["router"], h_dtype=F32, x_off=lc,
                                         y_off=lc, rows=seq, **common)
            n_exp = p["router"].shape[1]
            ntok = bsz * seq
            tile = _pick(2 * ntok, 512, SUBLANES)
            src_token, pos, tile_expert, tile_valid = _route_plan(route.reshape(ntok, LANES), n_exp, tile)
            xs = _gather_rows(h2.reshape(ntok, d), src_token, rows=256)
            ys = _moe_experts(xs, p["exp_w1"].astype(BF16), p["exp_w3"].astype(BF16),
                              p["exp_w2"].astype(BF16), tile_expert, tile_valid, tile=tile,
                              tn_up=512, tn_down=1024)
            yg = _gather_rows(ys, pos, rows=256).reshape(2, bsz, seq, d)
            return _moe_residual_ln(x1, yg, route, mod, p["ln2_g"], p["ln2_b"], alpha=alpha, tr=tr,
                                    k_gate=5)
    return x_all[:, lc:]


_LAYER0 = ("w_ada", "b_ada", "w_in", "mu_prev", "mu_next", "pool_w", "pool_scale", "w0_f", "w2_f",
           "w0_b", "w2_b", "a0_f", "a2_f", "a0_b", "a2_b", "g_up", "k_k", "k_a", "r_k", "lnx_g",
           "lnx_b", "w_out", "ln1_g", "ln1_b", "ln2_g", "ln2_b", "ffn_w1", "ffn_w3", "ffn_w2")
_LAYER1 = _LAYER0[:26] + ("v0", "v2", "router", "exp_w1", "exp_w3", "exp_w2")


def kernel(x, c, ctx, c_ctx, l0_w_ada, l0_b_ada, l0_w_in, l0_mu_prev, l0_mu_next, l0_pool_w, l0_pool_scale, l0_w0_f, l0_w2_f, l0_w0_b, l0_w2_b, l0_a0_f, l0_a2_f, l0_a0_b, l0_a2_b, l0_g_up, l0_k_k, l0_k_a, l0_r_k, l0_lnx_g, l0_lnx_b, l0_w_out, l0_ln1_g, l0_ln1_b, l0_ln2_g, l0_ln2_b, l0_ffn_w1, l0_ffn_w3, l0_ffn_w2, l1_w_ada, l1_b_ada, l1_w_in, l1_mu_prev, l1_mu_next, l1_pool_w, l1_pool_scale, l1_w0_f, l1_w2_f, l1_w0_b, l1_w2_b, l1_a0_f, l1_a2_f, l1_a0_b, l1_a2_b, l1_g_up, l1_k_k, l1_k_a, l1_r_k, l1_lnx_g, l1_lnx_b, l1_w_out, l1_ln1_g, l1_ln1_b, l1_ln2_g, l1_ln2_b, l1_v0, l1_v2, l1_router, l1_exp_w1, l1_exp_w3, l1_exp_w2):
    l0 = dict(zip(_LAYER0, (l0_w_ada, l0_b_ada, l0_w_in, l0_mu_prev, l0_mu_next, l0_pool_w, l0_pool_scale, l0_w0_f, l0_w2_f, l0_w0_b, l0_w2_b, l0_a0_f, l0_a2_f, l0_a0_b, l0_a2_b, l0_g_up, l0_k_k, l0_k_a, l0_r_k, l0_lnx_g, l0_lnx_b, l0_w_out, l0_ln1_g, l0_ln1_b, l0_ln2_g, l0_ln2_b, l0_ffn_w1, l0_ffn_w3, l0_ffn_w2)))
    l1 = dict(zip(_LAYER1, (l1_w_ada, l1_b_ada, l1_w_in, l1_mu_prev, l1_mu_next, l1_pool_w, l1_pool_scale, l1_w0_f, l1_w2_f, l1_w0_b, l1_w2_b, l1_a0_f, l1_a2_f, l1_a0_b, l1_a2_b, l1_g_up, l1_k_k, l1_k_a, l1_r_k, l1_lnx_g, l1_lnx_b, l1_w_out, l1_ln1_g, l1_ln1_b, l1_ln2_g, l1_ln2_b, l1_v0, l1_v2, l1_router, l1_exp_w1, l1_exp_w3, l1_exp_w2)))
    return _forward(x, c, ctx, c_ctx, [l0, l1], GRID_W)
```

```python
import functools
import math

import numpy as np
import jax
import jax.numpy as jnp
from jax import lax
from jax.experimental import pallas as pl
from jax.experimental.pallas import tpu as pltpu

F32 = jnp.float32
BF16 = jnp.bfloat16

GRID_W = 64
POOL_WINDOWS = (2, 4, 8, 16)
HEAD = 64
LANES = 128
SUBLANES = 8
HALO = 16
N_ADA = 6
LN_EPS = 1e-5
GN_EPS = 64e-5
NORM_EPS = 1e-12
CHUNK = 64
SCAN_PAIRS = 8
SCAN_SUB = 2
INV_PASSES = 1
VMEM_LIMIT = 56 * 1024 * 1024
COND_ROWS = 16


def _cparams(sem):
    return pltpu.CompilerParams(dimension_semantics=sem, vmem_limit_bytes=VMEM_LIMIT)


def _pick(n, target, mult):
    best = None
    for d in range(mult, min(n, target) + 1, mult):
        if n % d == 0:
            best = d
    return best if best is not None else n


def _dot(a, b):
    return jnp.dot(a, b, preferred_element_type=F32)


def _dot_nt(a, b):
    return lax.dot_general(a, b, (((1,), (1,)), ((), ())), preferred_element_type=F32)


def _split2(x):
    hi = x.astype(BF16)
    lo = (x - hi.astype(F32)).astype(BF16)
    return hi, lo


def _split3(x):
    hi = x.astype(BF16)
    r1 = x - hi.astype(F32)
    mid = r1.astype(BF16)
    lo = (r1 - mid.astype(F32)).astype(BF16)
    return hi, mid, lo


def _mm(a, b, passes=1):
    if passes == 1:
        return _dot(a.astype(BF16), b.astype(BF16))
    a_hi, a_lo = _split2(a)
    b_hi, b_lo = _split2(b)
    return _dot(a_hi, b_hi) + _dot(a_lo, b_hi) + _dot(a_hi, b_lo)


def _sigmoid(x):
    return 1.0 / (1.0 + jnp.exp(-x))


def _layer_norm_rows(x):
    mu = jnp.mean(x, axis=-1, keepdims=True)
    xc = x - mu
    var = jnp.mean(xc * xc, axis=-1, keepdims=True)
    return xc * lax.rsqrt(var + LN_EPS)


def _mod_kernel(c_ref, w_ref, b_ref, o_ref):
    c = c_ref[...]
    s = c * _sigmoid(c)
    o_ref[...] = _mm(s, w_ref[...], passes=3) + b_ref[...]


def _modulation(cond, w_ada, b_ada):
    d, n = w_ada.shape
    tn = _pick(n, 512, LANES)
    return pl.pallas_call(
        _mod_kernel,
        grid=(n // tn,),
        in_specs=[pl.BlockSpec((COND_ROWS, d), lambda j: (0, 0)),
                  pl.BlockSpec((d, tn), lambda j: (0, j)),
                  pl.BlockSpec((1, tn), lambda j: (0, j))],
        out_specs=pl.BlockSpec((COND_ROWS, tn), lambda j: (0, j)),
        out_shape=jax.ShapeDtypeStruct((COND_ROWS, n), F32),
        compiler_params=_cparams(("parallel",)),
        name="modulation",
    )(cond, w_ada, b_ada.reshape(1, n))


def _mod_row(mod_ref, row, chunk, d):
    return mod_ref[pl.ds(row, 1), chunk * d:(chunk + 1) * d]


def _ln_mod_kernel(x_ref, mod_ref, h_ref, *, nctx_blk, ctx_row, k_shift, k_scale):
    b, j = pl.program_id(0), pl.program_id(1)
    d = x_ref.shape[-1]
    row = jnp.where(j < nctx_blk, ctx_row, b)
    shift = _mod_row(mod_ref, row, k_shift, d)
    scale = _mod_row(mod_ref, row, k_scale, d)
    h_ref[...] = (_layer_norm_rows(x_ref[...]) * (1.0 + scale) + shift).astype(h_ref.dtype)


def _ln_mod(x_all, mod, *, lc, tr, ctx_row, k_shift, k_scale):
    bsz, t, d = x_all.shape
    kern = functools.partial(_ln_mod_kernel, nctx_blk=lc // tr, ctx_row=ctx_row,
                             k_shift=k_shift, k_scale=k_scale)
    return pl.pallas_call(
        kern,
        grid=(bsz, t // tr),
        in_specs=[pl.BlockSpec((None, tr, d), lambda b, j: (b, j, 0)),
                  pl.BlockSpec(mod.shape, lambda b, j: (0, 0))],
        out_specs=pl.BlockSpec((None, tr, d), lambda b, j: (b, j, 0)),
        out_shape=jax.ShapeDtypeStruct((bsz, t, d), BF16),
        compiler_params=_cparams(("parallel", "parallel")),
        name="ln_modulate",
    )(x_all, mod)


def _mm_kernel(a_ref, w_ref, o_ref, *scratch, nk):
    if nk == 1:
        o_ref[...] = _dot(a_ref[...], w_ref[...]).astype(o_ref.dtype)
        return
    acc_ref, = scratch
    k = pl.program_id(2)

    @pl.when(k == 0)
    def _():
        acc_ref[...] = jnp.zeros_like(acc_ref)

    acc_ref[...] += _dot(a_ref[...], w_ref[...])

    @pl.when(k == nk - 1)
    def _():
        o_ref[...] = acc_ref[...].astype(o_ref.dtype)


def _matmul(a, w, *, out_dtype, tm, tn, tk, name):
    m, kdim = a.shape
    n = w.shape[1]
    tm, tn, tk = _pick(m, tm, SUBLANES), _pick(n, tn, LANES), _pick(kdim, tk, LANES)
    nk = kdim // tk
    scratch = [pltpu.VMEM((tm, tn), F32)] if nk > 1 else []
    return pl.pallas_call(
        functools.partial(_mm_kernel, nk=nk),
        grid=(m // tm, n // tn, nk),
        in_specs=[pl.BlockSpec((tm, tk), lambda i, j, k: (i, k)),
                  pl.BlockSpec((tk, tn), lambda i, j, k: (k, j))],
        out_specs=pl.BlockSpec((tm, tn), lambda i, j, k: (i, j)),
        out_shape=jax.ShapeDtypeStruct((m, n), out_dtype),
        scratch_shapes=scratch,
        compiler_params=_cparams(("parallel", "parallel", "arbitrary")),
        name=name,
    )(a, w)


def _glu_kernel(a_ref, w1_ref, w3_ref, o_ref):
    a = a_ref[...]
    p1 = _dot(a, w1_ref[...])
    p3 = _dot(a, w3_ref[...])
    o_ref[...] = (p1 * _sigmoid(p1) * p3).astype(o_ref.dtype)


def _glu(a, w1, w3, *, tm, tn, name):
    m, kdim = a.shape
    f = w1.shape[1]
    tm, tn = _pick(m, tm, SUBLANES), _pick(f, tn, LANES)
    w_spec = pl.BlockSpec((kdim, tn), lambda i, j: (0, j))
    return pl.pallas_call(
        _glu_kernel,
        grid=(m // tm, f // tn),
        in_specs=[pl.BlockSpec((tm, kdim), lambda i, j: (i, 0)), w_spec, w_spec],
        out_specs=pl.BlockSpec((tm, tn), lambda i, j: (i, j)),
        out_shape=jax.ShapeDtypeStruct((m, f), BF16),
        compiler_params=_cparams(("parallel", "parallel")),
        name=name,
    )(a, w1, w3)


def _pool_consts(lc, tl, grid_w, tb):
    rows = tl // grid_w
    cb, c1, inv = [], [], []
    for win in POOL_WINDOWS:
        half = win // 2
        t = np.arange(tb)
        same_row = (t[:, None] // grid_w) == (t[None, :] // grid_w)
        dc = (t[None, :] % grid_w) - (t[:, None] % grid_w)
        cb.append((same_row & (dc >= -half) & (dc < half)).astype(np.float32))
        tc = np.arange(lc)
        d1 = tc[None, :] - tc[:, None]
        c1.append(((d1 >= -half) & (d1 < half)).astype(np.float32))
        cnt1 = np.minimum(tc + half, lc) - np.maximum(tc - half, 0)
        g = np.arange(grid_w)
        cntc = np.minimum(g + half, grid_w) - np.maximum(g - half, 0)
        r = np.arange(rows)
        cntr = np.minimum(r + half, rows) - np.maximum(r - half, 0)
        cnt2 = (cntr[:, None] * cntc[None, :]).reshape(-1)
        iv = 1.0 / np.concatenate([cnt1, cnt2]).astype(np.float64)
        inv.append(np.broadcast_to(iv[:, None], (lc + tl, LANES)).astype(np.float32))
    return (jnp.asarray(np.stack(cb), BF16), jnp.asarray(np.stack(c1), BF16),
            jnp.asarray(np.stack(inv), F32))


def _pool_kernel(u_ref, cb_ref, c1_ref, inv_ref, pw_ref, ps_ref, o_ref, s1_ref, acc_ref,
                 *, lc, tb, grid_w, half):
    t = u_ref.shape[0]
    tl = t - lc
    acc_ref[0:lc, :] = _dot(c1_ref[...], u_ref[0:lc, :])
    cb = cb_ref[...]
    for blk in range(tl // tb):
        lo, hi = lc + blk * tb, lc + (blk + 1) * tb
        s1_ref[lo:hi, :] = _dot(cb, u_ref[lo:hi, :])
    acc_ref[lc:t, :] = s1_ref[lc:t, :]
    for dr in range(-half, half):
        if dr == 0:
            continue
        sh = abs(dr) * grid_w
        if sh >= tl:
            continue
        if dr > 0:
            acc_ref[lc:t - sh, :] += s1_ref[lc + sh:t, :]
        else:
            acc_ref[lc + sh:t, :] += s1_ref[lc:t - sh, :]
    n_rep = u_ref.shape[1] // LANES
    diff = acc_ref[...] * jnp.concatenate([inv_ref[...]] * n_rep, axis=1) - u_ref[...].astype(F32)
    o_ref[...] = (_dot(diff.astype(BF16), pw_ref[...]) * ps_ref[...]).astype(o_ref.dtype)


def _pool_group(u_all, consts, pool_w, pool_scale, idx, *, lc, grid_w, tb, pg):
    bsz, t, _ = u_all.shape
    cb, c1, inv = consts
    half = POOL_WINDOWS[idx] // 2
    kern = functools.partial(_pool_kernel, lc=lc, tb=tb, grid_w=grid_w, half=half)
    return pl.pallas_call(
        kern,
        grid=(bsz,),
        in_specs=[pl.BlockSpec((None, t, pg), lambda b: (b, 0, idx)),
                  pl.BlockSpec((None, tb, tb), lambda b: (idx, 0, 0)),
                  pl.BlockSpec((None, lc, lc), lambda b: (idx, 0, 0)),
                  pl.BlockSpec((None, t, LANES), lambda b: (idx, 0, 0)),
                  pl.BlockSpec((None, pg, pg), lambda b: (idx, 0, 0)),
                  pl.BlockSpec((1, pg), lambda b: (0, idx))],
        out_specs=pl.BlockSpec((None, t, pg), lambda b: (b, 0, 0)),
        out_shape=jax.ShapeDtypeStruct((bsz, t, pg), BF16),
        scratch_shapes=[pltpu.VMEM((t, pg), F32), pltpu.VMEM((t, pg), F32)],
        compiler_params=_cparams(("parallel",)),
        name="pool_mixer_%d" % idx,
    )(u_all, cb, c1, inv, pool_w, pool_scale)


def _token_shift(f_ref, prev_ref, next_ref, mu_p, mu_n, seg_first, seg_last):
    f = f_ref[...].astype(F32)
    tr = f.shape[0]
    row = lax.broadcasted_iota(jnp.int32, f.shape, 0)
    prev_row = jnp.where(seg_first, 0.0, prev_ref[...].astype(F32)[HALO - 1:HALO, :])
    next_row = jnp.where(seg_last, 0.0, next_ref[...].astype(F32)[0:1, :])
    prev = jnp.where(row == 0, prev_row, pltpu.roll(f, 1, 0))
    nxt = jnp.where(row == tr - 1, next_row, pltpu.roll(f, tr - 1, 0))
    return f + mu_p * (prev - f) + mu_n * (nxt - f)


def _segment_flags(j, nctx_blk, n_blk):
    seg_first = jnp.logical_or(j == 0, j == nctx_blk)
    seg_last = jnp.logical_or(j == nctx_blk - 1, j == n_blk - 1)
    return seg_first, seg_last


def _shift_specs(tr, tc, col_blk, t):
    per_blk = tr // HALO
    last = t // HALO - 1
    cur = pl.BlockSpec((None, tr, tc), lambda b, j, c: (b, j, col_blk(c)))
    prv = pl.BlockSpec((None, HALO, tc),
                       lambda b, j, c: (b, jnp.maximum(j * per_blk - 1, 0), col_blk(c)))
    nxt = pl.BlockSpec((None, HALO, tc),
                       lambda b, j, c: (b, jnp.minimum((j + 1) * per_blk, last), col_blk(c)))
    return [cur, prv, nxt]


def _lora_kernel(u_ref, up_ref, un_ref, mu_ref, o_ref, *, nctx_blk, n_blk, lr, gl):
    j = pl.program_id(1)
    seg_first, seg_last = _segment_flags(j, nctx_blk, n_blk)
    z = _token_shift(u_ref, up_ref, un_ref, mu_ref[0:1, :], mu_ref[1:2, :], seg_first, seg_last)
    col = lax.broadcasted_iota(jnp.int32, z.shape, 1)
    act = jnp.where(col < 2 * lr, jnp.tanh(z),
                    jnp.where(jnp.logical_and(col >= 4 * lr, col < 4 * lr + gl), _sigmoid(z), z))
    o_ref[...] = act.astype(o_ref.dtype)


def _lora_act(u_all, mu_lora, *, lc, tr, col0, wl, lr, gl):
    bsz, t, _ = u_all.shape
    specs = _shift_specs(tr, wl, lambda c: col0 // wl, t)
    kern = functools.partial(_lora_kernel, nctx_blk=lc // tr, n_blk=t // tr, lr=lr, gl=gl)
    return pl.pallas_call(
        kern,
        grid=(bsz, t // tr, 1),
        in_specs=specs + [pl.BlockSpec((SUBLANES, wl), lambda b, j, c: (0, 0))],
        out_specs=pl.BlockSpec((None, tr, wl), lambda b, j, c: (b, j, 0)),
        out_shape=jax.ShapeDtypeStruct((bsz, t, wl), BF16),
        compiler_params=_cparams(("parallel", "parallel", "arbitrary")),
        name="lora_act",
    )(u_all, u_all, u_all, mu_lora)


def _head_ones():
    r = lax.broadcasted_iota(jnp.int32, (LANES, LANES), 0)
    c = lax.broadcasted_iota(jnp.int32, (LANES, LANES), 1)
    return jnp.where((r ^ c) < HEAD, 1.0, 0.0).astype(BF16)


def _head_sum(x, ones):
    outs = []
    for s in range(x.shape[1] // LANES):
        hi, mid, lo = _split3(x[:, s * LANES:(s + 1) * LANES])
        outs.append(_dot(hi, ones) + _dot(mid, ones) + _dot(lo, ones))
    return outs[0] if len(outs) == 1 else jnp.concatenate(outs, axis=1)


(_V_MPR, _V_MNR, _V_MPK, _V_MNK, _V_MPV, _V_MNV, _V_W0F, _V_W0B, _V_A0F, _V_A0B,
 _V_V0, _V_KK, _V_KA, _V_RK, _V_LG, _V_LB) = range(16)
_DECAY_SCALE = math.exp(-0.5)


def _terms_kernel(*refs, nctx_blk, n_blk, lr, gl, has_vres):
    (ur, urp, urn, uk, ukp, ukn, uv, uvp, uvn, act_ref, vec_ref,
     w2f_ref, w2b_ref, a2f_ref, a2b_ref, gup_ref) = refs[:16]
    pos = 16
    if has_vres:
        v2_ref, vfirst_ref = refs[pos:pos + 2]
        pos += 2
    (r_o, v_o, kk_o, lwf_o, lwb_o, kf_o, kb_o, af_o, ab_o, gate_o, bonus_o) = refs[pos:]

    j = pl.program_id(1)
    seg_first, seg_last = _segment_flags(j, nctx_blk, n_blk)
    vec = lambda i: vec_ref[i:i + 1, :]
    r = _token_shift(ur, urp, urn, vec(_V_MPR), vec(_V_MNR), seg_first, seg_last)
    k = _token_shift(uk, ukp, ukn, vec(_V_MPK), vec(_V_MNK), seg_first, seg_last)
    v = _token_shift(uv, uvp, uvn, vec(_V_MPV), vec(_V_MNV), seg_first, seg_last)

    act = act_ref[...]
    a_wf, a_wb = act[:, 0:lr], act[:, lr:2 * lr]
    a_af, a_ab = act[:, 2 * lr:3 * lr], act[:, 3 * lr:4 * lr]
    a_g = act[:, 4 * lr:4 * lr + gl]
    if has_vres:
        a_v = act[:, 4 * lr + gl:5 * lr + gl]
        v = v + (vfirst_ref[...].astype(F32) - v) * _sigmoid(vec(_V_V0) + _dot(a_v, v2_ref[...]))

    lwf_o[...] = -_DECAY_SCALE * _sigmoid(vec(_V_W0F) + _dot(a_wf, w2f_ref[...]))
    lwb_o[...] = -_DECAY_SCALE * _sigmoid(vec(_V_W0B) + _dot(a_wb, w2b_ref[...]))
    a_f = _sigmoid(vec(_V_A0F) + _dot(a_af, a2f_ref[...]))
    a_b = _sigmoid(vec(_V_A0B) + _dot(a_ab, a2b_ref[...]))

    ones = _head_ones()
    kkr = k * vec(_V_KK)
    norm = jnp.sqrt(_head_sum(kkr * kkr, ones))
    kk_o[...] = (kkr / jnp.maximum(norm, NORM_EPS)).astype(kk_o.dtype)
    k_f = k * (1.0 + (a_f - 1.0) * vec(_V_KA))
    k_b = k * (1.0 + (a_b - 1.0) * vec(_V_KA))
    bonus_o[...] = (_head_sum(r * (k_f + k_b) * vec(_V_RK), ones) * v).astype(bonus_o.dtype)
    gate_o[...] = _dot(a_g, gup_ref[...]).astype(gate_o.dtype)
    r_o[...] = r.astype(r_o.dtype)
    v_o[...] = v.astype(v_o.dtype)
    kf_o[...] = k_f.astype(kf_o.dtype)
    kb_o[...] = k_b.astype(kb_o.dtype)
    af_o[...] = a_f.astype(af_o.dtype)
    ab_o[...] = a_b.astype(ab_o.dtype)


def _rwkv_terms(u_all, act, vec, lw, v_first, *, lc, tr, tc, pool_w, rw, lr, gl):
    bsz, t, _ = u_all.shape
    has_vres = v_first is not None
    nb = rw // tc
    specs = []
    for part in range(3):
        base = (pool_w + part * rw) // tc
        specs += _shift_specs(tr, tc, (lambda c, base=base: base + c), t)
    specs.append(pl.BlockSpec((None, tr, act.shape[-1]), lambda b, j, c: (b, j, 0)))
    specs.append(pl.BlockSpec((vec.shape[0], tc), lambda b, j, c: (0, c)))
    wspec = lambda rows: pl.BlockSpec((rows, tc), lambda b, j, c: (0, c))
    specs += [wspec(lr), wspec(lr), wspec(lr), wspec(lr), wspec(gl)]
    args = [u_all] * 9 + [act, vec, lw["w2_f"], lw["w2_b"], lw["a2_f"], lw["a2_b"], lw["g_up"]]
    if has_vres:
        specs += [wspec(lr), pl.BlockSpec((None, tr, tc), lambda b, j, c: (b, j, c))]
        args += [lw["v2"], v_first]
    out_spec = pl.BlockSpec((None, tr, tc), lambda b, j, c: (b, j, c))
    kern = functools.partial(_terms_kernel, nctx_blk=lc // tr, n_blk=t // tr, lr=lr, gl=gl,
                             has_vres=has_vres)
    names = ("r", "v", "kk", "lw_f", "lw_b", "k_f", "k_b", "a_f", "a_b", "gate", "bonus")
    dtypes = [F32 if n.startswith("lw") else BF16 for n in names]
    outs = pl.pallas_call(
        kern,
        grid=(bsz, t // tr, nb),
        in_specs=specs,
        out_specs=[out_spec] * len(names),
        out_shape=[jax.ShapeDtypeStruct((bsz, t, rw), dt) for dt in dtypes],
        compiler_params=_cparams(("parallel", "parallel", "parallel")),
        name="rwkv_terms",
    )(*args)
    return dict(zip(names, outs))


def _cumsum_rows(x, rev):
    n = x.shape[0]
    row = lax.broadcasted_iota(jnp.int32, x.shape, 0)
    s = 1
    while s < n:
        if rev:
            x = x + jnp.where(row < n - s, pltpu.roll(x, n - s, 0), 0.0)
        else:
            x = x + jnp.where(row >= s, pltpu.roll(x, s, 0), 0.0)
        s *= 2
    return x


def _wkv_chunks(chains):
    n = chains[0][0].shape[0]
    hp = 2 * n
    lane = lax.broadcasted_iota(jnp.int32, (n, LANES), 1)
    first_head = lane < HEAD
    ri = lax.broadcasted_iota(jnp.int32, (hp, hp), 0)
    ci = lax.broadcasted_iota(jnp.int32, (hp, hp), 1)
    blk = ri ^ ci
    eye = ri == ci
    tdiff = (ci & (n - 1)) - (ri & (n - 1))
    dist = {False: jnp.where(blk < n, tdiff, hp), True: jnp.where(blk < n, -tdiff, hp)}

    def stack(x):
        return jnp.concatenate([jnp.where(first_head, x, 0.0), jnp.where(first_head, 0.0, x)], axis=0)

    pre = []
    for r, v, kk, lw, k, a, state, rev in chains:
        c = _cumsum_rows(lw, rev)
        ctot = c[0:1, :] if rev else c[n - 1:n, :]
        e_pos, e_neg = jnp.exp(c), jnp.exp(-c)
        e_prev, e_rem = jnp.exp(c - lw), jnp.exp(ctot - c)
        kka = kk * a
        bt, kt = kka * e_neg, k * e_neg
        pre.append(dict(s_at=stack(-kk * e_prev), s_rt=stack(r * e_pos), s_v=stack(v), rt=r * e_pos,
                        rhs1=jnp.concatenate([bt, bt, kt, kt], axis=0).astype(BF16),
                        lhs_t=jnp.concatenate([stack(kka * e_rem), stack(k * e_rem)], axis=0),
                        wtot=jnp.exp(ctot), strict=dist[rev] < 0, incl=dist[rev] <= 0))

    s1 = [_dot_nt(jnp.concatenate([p["s_at"], p["s_rt"]], axis=0).astype(BF16), p["rhs1"]) for p in pre]
    nmat = [jnp.where(p["strict"], s[:hp, :hp], 0.0) for p, s in zip(pre, s1)]
    m_ak = [jnp.where(p["strict"], s[:hp, hp:], 0.0) for p, s in zip(pre, s1)]
    m_rbk = [jnp.concatenate([jnp.where(p["incl"], s[hp:, :hp], 0.0), jnp.where(p["incl"], s[hp:, hp:], 0.0)],
                             axis=1).astype(BF16) for p, s in zip(pre, s1)]
    z = [_mm(m, p["s_v"]) for m, p in zip(m_ak, pre)]

    n8 = [jnp.where(blk < SUBLANES, m, 0.0) for m in nmat]
    tmat = [jnp.where(eye, 1.0, m) for m in n8]
    n2 = [_mm(m, m, INV_PASSES) for m in n8]
    st = [_mm(jnp.concatenate([t, m], axis=0), m, INV_PASSES) for t, m in zip(tmat, n2)]
    tmat = [t + s[:hp] for t, s in zip(tmat, st)]
    tmat = [t + _mm(t, s[hp:], INV_PASSES) for t, s in zip(tmat, st)]
    size = SUBLANES
    while size < n:
        sel = jnp.logical_and(blk >= size, blk < 2 * size)
        pieces = [[(g * 2 * size + (0 if chain[7] else size), g * 2 * size + (size if chain[7] else 2 * size))
                   for g in range(hp // (2 * size))] for chain in chains]

        def take(mat, rows):
            return jnp.concatenate([mat[a:b] for a, b in rows], axis=0)

        def put(mat, rows, part, keep):
            out, at, prev = [], 0, 0
            for a, b in rows:
                if a > prev:
                    out.append(mat[prev:a] if keep else jnp.zeros((a - prev, mat.shape[1]), F32))
                out.append(part[at:at + b - a])
                at, prev = at + b - a, b
            if prev < mat.shape[0]:
                out.append(mat[prev:] if keep else jnp.zeros((mat.shape[0] - prev, mat.shape[1]), F32))
            return jnp.concatenate(out, axis=0)

        x = [_mm(take(jnp.where(sel, m, 0.0), rows), t, INV_PASSES) for m, t, rows in zip(nmat, tmat, pieces)]
        x = [put(t, rows, xx, keep=False) for t, rows, xx in zip(tmat, pieces, x)]
        th = [take(t, rows) for t, rows in zip(tmat, pieces)]
        th = [h + _mm(h, xx, INV_PASSES) for h, xx in zip(th, x)]
        tmat = [put(t, rows, h, keep=True) for t, rows, h in zip(tmat, pieces, th)]
        size *= 2

    gu = [_mm(t, jnp.concatenate([p["s_at"], zz], axis=1)) for t, p, zz in zip(tmat, pre, z)]
    zeros = jnp.zeros((hp, LANES), F32)
    rhs4 = [jnp.concatenate([g, jnp.concatenate([zeros, p["s_v"]], axis=1)], axis=0).astype(BF16)
            for g, p in zip(gu, pre)]
    tb = [_dot(jnp.concatenate([m, p["lhs_t"].T.astype(BF16)], axis=0), rr)
          for m, p, rr in zip(m_rbk, pre, rhs4)]
    top = [x_[:hp] for x_ in tb]
    bot = [x_[hp:] for x_ in tb]

    outs = []
    for p, tp, bt_, chain in zip(pre, top, bot, chains):
        q = p["rt"] + tp[:n, :LANES] + tp[n:, :LANES]
        amat = jnp.where(eye, jnp.broadcast_to(p["wtot"], (LANES, LANES)), 0.0) + bt_[:, :LANES]
        qa = jnp.concatenate([q, amat], axis=0).astype(BF16)
        s_hi, s_lo = _split2(chain[6])
        outs.append(_dot(qa, s_hi) + _dot(qa, s_lo))
    return [(o[:n] + tp[:n, LANES:] + tp[n:, LANES:], o[n:] + bt_[:, LANES:])
            for o, tp, bt_ in zip(outs, top, bot)]


def _scan_kernel(*refs, pairs, chunk):
    ins, (yf_ref, yb_ref, state_ref) = refs[:12], refs[12:]
    s = pl.program_id(1)
    n_sub = yf_ref.shape[0] // chunk

    @pl.when(s == 0)
    def _():
        state_ref[...] = jnp.zeros_like(state_ref)

    chains = [(d, p, slice(p * LANES, (p + 1) * LANES)) for d in range(2) for p in range(pairs)]

    def sub_step(i, carry):
        rows = [pl.ds(pl.multiple_of((n_sub - 1 - i if d else i) * chunk, chunk), chunk) for d in range(2)]
        loaded = [tuple(ref[rows[d], cols].astype(F32) for ref in ins[6 * d:6 * d + 6])
                  + (state_ref[d, p], bool(d)) for d, p, cols in chains]
        for (d, p, cols), (y, new_state) in zip(chains, _wkv_chunks(loaded)):
            y_ref = yb_ref if d else yf_ref
            y_ref[rows[d], cols] = y.astype(y_ref.dtype)
            state_ref[d, p] = new_state
        return carry

    lax.fori_loop(0, n_sub, sub_step, 0)


def _wkv_scan(tm, *, lc, chunk, pairs):
    bsz, t, rw = tm["r"].shape
    width = pairs * LANES
    ngrp = rw // width
    n_sub = math.gcd(math.gcd(lc // chunk, (t - lc) // chunk), SCAN_SUB)
    step_rows = n_sub * chunk
    nctx, ntot = lc // step_rows, t // step_rows

    def fwd(g, s):
        return (g // ngrp, s, g % ngrp)

    def bwd(g, s):
        return (g // ngrp, jnp.where(s < nctx, nctx - 1 - s, ntot - 1 - (s - nctx)), g % ngrp)

    blk = (None, step_rows, width)
    names_f = ("r", "v", "kk", "lw_f", "k_f", "a_f")
    names_b = ("r", "v", "kk", "lw_b", "k_b", "a_b")
    in_specs = [pl.BlockSpec(blk, fwd)] * 6 + [pl.BlockSpec(blk, bwd)] * 6
    args = [tm[n] for n in names_f] + [tm[n] for n in names_b]
    return pl.pallas_call(
        functools.partial(_scan_kernel, pairs=pairs, chunk=chunk),
        grid=(bsz * ngrp, ntot),
        in_specs=in_specs,
        out_specs=[pl.BlockSpec(blk, fwd), pl.BlockSpec(blk, bwd)],
        out_shape=[jax.ShapeDtypeStruct((bsz, t, rw), BF16)] * 2,
        scratch_shapes=[pltpu.VMEM((2, pairs, LANES, LANES), F32)],
        compiler_params=_cparams(("parallel", "arbitrary")),
        name="wkv7_scan",
    )(*args)


def _rwkv_out_kernel(yf_ref, yb_ref, bonus_ref, gate_ref, vec_ref, o_ref):
    y = yf_ref[...].astype(F32) + yb_ref[...].astype(F32)
    ones = _head_ones()
    inv_n = 1.0 / HEAD
    mu = _head_sum(y, ones) * inv_n
    yc = y - mu
    var = _head_sum(yc * yc, ones) * inv_n
    yn = yc * lax.rsqrt(var + GN_EPS)
    yn = yn * vec_ref[_V_LG:_V_LG + 1, :] + vec_ref[_V_LB:_V_LB + 1, :]
    o_ref[...] = ((yn + bonus_ref[...].astype(F32)) * gate_ref[...].astype(F32)).astype(o_ref.dtype)


def _rwkv_out(y_f, y_b, bonus, gate, vec, *, tr, tc):
    bsz, t, rw = y_f.shape
    spec = pl.BlockSpec((None, tr, tc), lambda b, j, c: (b, j, c))
    return pl.pallas_call(
        _rwkv_out_kernel,
        grid=(bsz, t // tr, rw // tc),
        in_specs=[spec] * 4 + [pl.BlockSpec((vec.shape[0], tc), lambda b, j, c: (0, c))],
        out_specs=spec,
        out_shape=jax.ShapeDtypeStruct((bsz, t, rw), BF16),
        compiler_params=_cparams(("parallel", "parallel", "parallel")),
        name="rwkv_out",
    )(y_f, y_b, bonus, gate, vec)


_R_E1, _R_E2, _R_G1, _R_G2 = range(4)

def _res_kernel(*refs, alpha, row_off_blk, nctx_blk, ctx_row, k_gate, k_shift, k_scale,
                with_mod, n_exp):
    x_ref, y_ref, mod_ref, gb_ref = refs[:4]
    pos = 4
    if n_exp:
        router_ref = refs[pos]
        pos += 1
    outs = refs[pos:]
    b, j = pl.program_id(0), pl.program_id(1)
    d = x_ref.shape[-1]
    row = jnp.where(j + row_off_blk < nctx_blk, ctx_row, b)
    gate = _mod_row(mod_ref, row, k_gate, d)
    xn = _layer_norm_rows(alpha * x_ref[...] + gate * y_ref[...]) * gb_ref[0:1, :] + gb_ref[1:2, :]
    outs[0][...] = xn
    if not with_mod:
        return
    h = _layer_norm_rows(xn) * (1.0 + _mod_row(mod_ref, row, k_scale, d)) + _mod_row(mod_ref, row, k_shift, d)
    outs[1][...] = h.astype(outs[1].dtype)
    if not n_exp:
        return
    logits = _mm(h, router_ref[...], passes=3)
    lane = lax.broadcasted_iota(jnp.int32, logits.shape, 1).astype(F32)
    neg = -jnp.inf
    logits = jnp.where(lane < n_exp, logits, neg)
    m1 = jnp.max(logits, axis=-1, keepdims=True)
    i1 = jnp.min(jnp.where(logits == m1, lane, float(LANES)), axis=-1, keepdims=True)
    rest = jnp.where(lane == i1, neg, logits)
    m2 = jnp.max(rest, axis=-1, keepdims=True)
    i2 = jnp.min(jnp.where(rest == m2, lane, float(LANES)), axis=-1, keepdims=True)
    e2 = jnp.exp(m2 - m1)
    g1 = 1.0 / (1.0 + e2)
    g2 = e2 / (1.0 + e2)
    outs[2][...] = jnp.where(lane == _R_E1, i1, jnp.where(lane == _R_E2, i2,
                             jnp.where(lane == _R_G1, g1, jnp.where(lane == _R_G2, g2, 0.0))))


def _residual_ln(x, y, mod, gain, bias, *, alpha, tr, lc, ctx_row, k_gate, x_off=0, y_off=0,
                 rows=None, mod2=None, router=None, h_dtype=BF16):
    bsz, _, d = x.shape
    rows = x.shape[1] if rows is None else rows
    n_exp = 0 if router is None else router.shape[1]
    gb = jnp.stack([gain, bias])
    in_specs = [pl.BlockSpec((None, tr, d), lambda b, j: (b, j + x_off // tr, 0)),
                pl.BlockSpec((None, tr, d), lambda b, j: (b, j + y_off // tr, 0)),
                pl.BlockSpec(mod.shape, lambda b, j: (0, 0)),
                pl.BlockSpec((2, d), lambda b, j: (0, 0))]
    args = [x, y, mod, gb]
    out_specs = [pl.BlockSpec((None, tr, d), lambda b, j: (b, j, 0))]
    out_shape = [jax.ShapeDtypeStruct((bsz, rows, d), F32)]
    if mod2 is not None:
        out_specs.append(pl.BlockSpec((None, tr, d), lambda b, j: (b, j, 0)))
        out_shape.append(jax.ShapeDtypeStruct((bsz, rows, d), h_dtype))
    if router is not None:
        rpad = jnp.zeros((d, LANES), F32).at[:, :n_exp].set(router)
        in_specs.append(pl.BlockSpec((d, LANES), lambda b, j: (0, 0)))
        args.append(rpad)
        out_specs.append(pl.BlockSpec((None, tr, LANES), lambda b, j: (b, j, 0)))
        out_shape.append(jax.ShapeDtypeStruct((bsz, rows, LANES), F32))
    k_shift, k_scale = mod2 if mod2 is not None else (0, 0)
    kern = functools.partial(_res_kernel, alpha=alpha, row_off_blk=x_off // tr, nctx_blk=lc // tr,
                             ctx_row=ctx_row, k_gate=k_gate, k_shift=k_shift, k_scale=k_scale,
                             with_mod=mod2 is not None, n_exp=n_exp)
    return pl.pallas_call(
        kern,
        grid=(bsz, rows // tr),
        in_specs=in_specs,
        out_specs=out_specs,
        out_shape=out_shape,
        compiler_params=_cparams(("parallel", "parallel")),
        name="residual_ln",
    )(*args)


def _route_plan(route, n_exp, tile):
    n = route.shape[0]
    e_flat = jnp.concatenate([route[:, _R_E1], route[:, _R_E2]]).astype(jnp.int32)
    experts = jnp.arange(n_exp, dtype=jnp.int32)
    onehot = (e_flat[:, None] == experts[None, :]).astype(jnp.int32)
    csum = jnp.cumsum(onehot, axis=0)
    rank = jnp.sum((csum - onehot) * onehot, axis=1)
    tiles_e = (csum[-1] + tile - 1) // tile
    tile_end = jnp.cumsum(tiles_e)
    tile_start = tile_end - tiles_e
    pos = jnp.sum(onehot * tile_start[None, :], axis=1) * tile + rank
    n_tiles = (2 * n) // tile + n_exp
    tidx = jnp.arange(n_tiles, dtype=jnp.int32)
    valid = tidx < tile_end[-1]
    last_e = jnp.max(jnp.where(tiles_e > 0, experts, 0))
    tile_expert = jnp.sum((tidx[:, None] >= tile_end[None, :]).astype(jnp.int32), axis=1)
    tile_expert = jnp.where(valid, tile_expert, last_e)
    token = jnp.arange(2 * n, dtype=jnp.int32) % n
    src_token = jnp.zeros((n_tiles * tile,), jnp.int32).at[pos].set(token)
    return src_token, pos, tile_expert, valid.astype(jnp.int32)


def _gather_kernel(idx_ref, src_ref, out_ref, sem):
    rows = out_ref.shape[0]

    def row_copy(j, src_row):
        return pltpu.make_async_copy(src_ref.at[pl.ds(src_row, 1)], out_ref.at[pl.ds(j, 1)], sem)

    def start(j, carry):
        row_copy(j, idx_ref[0, j]).start()
        return carry

    def wait(j, carry):
        row_copy(j, 0).wait()
        return carry

    lax.fori_loop(0, rows, start, 0, unroll=8)
    lax.fori_loop(0, rows, wait, 0, unroll=8)


def _gather_rows(src, idx, *, rows):
    n_out = idx.shape[0]
    d = src.shape[1]
    rows = _pick(n_out, rows, SUBLANES)
    return pl.pallas_call(
        _gather_kernel,
        grid=(n_out // rows,),
        in_specs=[pl.BlockSpec((None, 1, rows), lambda i: (i, 0, 0), memory_space=pltpu.SMEM),
                  pl.BlockSpec(memory_space=pl.ANY)],
        out_specs=pl.BlockSpec((rows, d), lambda i: (i, 0)),
        out_shape=jax.ShapeDtypeStruct((n_out, d), src.dtype),
        scratch_shapes=[pltpu.SemaphoreType.DMA(())],
        compiler_params=_cparams(("arbitrary",)),
        name="row_gather",
    )(idx.reshape(n_out // rows, 1, rows), src)


def _moe_glu_kernel(te_ref, tv_ref, a_ref, w1_ref, w3_ref, o_ref):
    i = pl.program_id(1)

    @pl.when(tv_ref[i] != 0)
    def _():
        a = a_ref[...].astype(BF16)
        p1 = _dot(a, w1_ref[...])
        p3 = _dot(a, w3_ref[...])
        o_ref[...] = (p1 * _sigmoid(p1) * p3).astype(o_ref.dtype)

    @pl.when(tv_ref[i] == 0)
    def _():
        o_ref[...] = jnp.zeros_like(o_ref)


def _moe_down_kernel(te_ref, tv_ref, h_ref, w2_ref, o_ref):
    i = pl.program_id(1)

    @pl.when(tv_ref[i] != 0)
    def _():
        o_ref[...] = _dot(h_ref[...], w2_ref[...])

    @pl.when(tv_ref[i] == 0)
    def _():
        o_ref[...] = jnp.zeros_like(o_ref)


def _moe_experts(xs, w1, w3, w2, tile_expert, tile_valid, *, tile, tn_up, tn_down):
    r, d = xs.shape
    _, _, f = w1.shape
    n_tiles = r // tile
    tn_up, tn_down = _pick(f, tn_up, LANES), _pick(d, tn_down, LANES)
    up_spec = pl.BlockSpec((None, d, tn_up), lambda j, i, te, tv: (te[i], 0, j))
    hid = pl.pallas_call(
        _moe_glu_kernel,
        grid_spec=pltpu.PrefetchScalarGridSpec(
            num_scalar_prefetch=2, grid=(f // tn_up, n_tiles),
            in_specs=[pl.BlockSpec((tile, d), lambda j, i, te, tv: (i, 0)), up_spec, up_spec],
            out_specs=pl.BlockSpec((tile, tn_up), lambda j, i, te, tv: (i, j))),
        out_shape=jax.ShapeDtypeStruct((r, f), BF16),
        compiler_params=_cparams(("parallel", "arbitrary")),
        name="moe_glu",
    )(tile_expert, tile_valid, xs, w1, w3)
    return pl.pallas_call(
        _moe_down_kernel,
        grid_spec=pltpu.PrefetchScalarGridSpec(
            num_scalar_prefetch=2, grid=(d // tn_down, n_tiles),
            in_specs=[pl.BlockSpec((tile, f), lambda j, i, te, tv: (i, 0)),
                      pl.BlockSpec((None, f, tn_down), lambda j, i, te, tv: (te[i], 0, j))],
            out_specs=pl.BlockSpec((tile, tn_down), lambda j, i, te, tv: (i, j))),
        out_shape=jax.ShapeDtypeStruct((r, d), F32),
        compiler_params=_cparams(("parallel", "arbitrary")),
        name="moe_down",
    )(tile_expert, tile_valid, hid, w2)


def _moe_out_kernel(x_ref, y1_ref, y2_ref, route_ref, mod_ref, gb_ref, o_ref, *, alpha, k_gate):
    b = pl.program_id(0)
    d = x_ref.shape[-1]
    route = route_ref[...]
    y = route[:, _R_G1:_R_G1 + 1] * y1_ref[...] + route[:, _R_G2:_R_G2 + 1] * y2_ref[...]
    gate = _mod_row(mod_ref, b, k_gate, d)
    o_ref[...] = _layer_norm_rows(alpha * x_ref[...] + gate * y) * gb_ref[0:1, :] + gb_ref[1:2, :]


def _moe_residual_ln(x, yg, route, mod, gain, bias, *, alpha, tr, k_gate):
    bsz, rows, d = x.shape
    row_spec = pl.BlockSpec((None, tr, d), lambda b, j: (b, j, 0))
    return pl.pallas_call(
        functools.partial(_moe_out_kernel, alpha=alpha, k_gate=k_gate),
        grid=(bsz, rows // tr),
        in_specs=[row_spec,
                  pl.BlockSpec((None, None, tr, d), lambda b, j: (0, b, j, 0)),
                  pl.BlockSpec((None, None, tr, d), lambda b, j: (1, b, j, 0)),
                  pl.BlockSpec((None, tr, LANES), lambda b, j: (b, j, 0)),
                  pl.BlockSpec(mod.shape, lambda b, j: (0, 0)),
                  pl.BlockSpec((2, d), lambda b, j: (0, 0))],
        out_specs=row_spec,
        out_shape=jax.ShapeDtypeStruct((bsz, rows, d), F32),
        compiler_params=_cparams(("parallel", "parallel")),
        name="moe_residual_ln",
    )(x, yg, yg, route, mod, jnp.stack([gain, bias]))


def _pad_rows(w, rows):
    return jnp.zeros((rows,) + w.shape[1:], w.dtype).at[:w.shape[0]].set(w)


def _layer_layout(p, pool_w, rw):
    lr_raw = p["w2_f"].shape[0]
    gl_raw = p["g_up"].shape[0]
    vr_raw = p["v2"].shape[0] if "v2" in p else 0
    lr = -(-max(lr_raw, vr_raw, 1) // LANES) * LANES
    gl = -(-gl_raw // LANES) * LANES
    wl = 5 * lr + gl
    wl_pad = -(-wl // 1024) * 1024 if wl > 512 else wl
    core = pool_w + 3 * rw
    sizes = [lr_raw] * 4 + [gl_raw] + ([vr_raw] if vr_raw else [])
    slots = [lr] * 4 + [gl] + [lr]
    d = p["w_in"].shape[0]

    def relayout(src, rows_shape):
        dst = jnp.zeros(rows_shape + (core + wl_pad,), src.dtype)
        dst = dst.at[..., :core].set(src[..., :core])
        s_off, d_off = core, core
        for size, slot in zip(sizes, slots):
            dst = dst.at[..., d_off:d_off + size].set(src[..., s_off:s_off + size])
            s_off += size
            d_off += slot
        return dst

    w_in = relayout(p["w_in"], (d,)).astype(BF16)
    zero_pool = jnp.zeros((pool_w,), F32)
    mu_p = relayout(jnp.concatenate([zero_pool, p["mu_prev"]]), ())
    mu_n = relayout(jnp.concatenate([zero_pool, p["mu_next"]]), ())
    mu_lora = jnp.zeros((SUBLANES, wl_pad), F32).at[0].set(mu_p[core:]).at[1].set(mu_n[core:])
    seg = lambda a, i: a[pool_w + i * rw:pool_w + (i + 1) * rw]
    zeros = jnp.zeros((rw,), F32)
    vec = jnp.stack([seg(mu_p, 0), seg(mu_n, 0), seg(mu_p, 1), seg(mu_n, 1), seg(mu_p, 2), seg(mu_n, 2),
                     p["w0_f"], p["w0_b"], p["a0_f"], p["a0_b"], p.get("v0", zeros),
                     p["k_k"], p["k_a"], p["r_k"].reshape(-1), p["lnx_g"], p["lnx_b"]])
    lw = {n: _pad_rows(p[n], lr).astype(BF16) for n in ("w2_f", "w2_b", "a2_f", "a2_b")}
    lw["g_up"] = _pad_rows(p["g_up"], gl).astype(BF16)
    if vr_raw:
        lw["v2"] = _pad_rows(p["v2"], lr).astype(BF16)
    return dict(w_in=w_in, mu_lora=mu_lora, vec=vec, lw=lw, lr=lr, gl=gl, wl=wl_pad, core=core)


def _mixer(x_all, mod, p, lay, v_first, pool_consts, *, lc, tr, tc, grid_w, tb, ctx_row):
    bsz, t, d = x_all.shape
    mix_w = p["w_out"].shape[0]
    pool_w = mix_w // 4
    pg = pool_w // len(POOL_WINDOWS)
    rw = mix_w - pool_w
    h = _ln_mod(x_all, mod, lc=lc, tr=tr, ctx_row=ctx_row, k_shift=0, k_scale=1)
    u = _matmul(h.reshape(bsz * t, d), lay["w_in"], out_dtype=BF16, tm=1024, tn=1024, tk=d,
                name="in_proj").reshape(bsz, t, -1)
    pool_wts = p["pool_w"].astype(BF16)
    pool_scale = p["pool_scale"].reshape(1, pool_w)
    pools = [_pool_group(u, pool_consts, pool_wts, pool_scale, i, lc=lc, grid_w=grid_w, tb=tb, pg=pg)
             for i in range(len(POOL_WINDOWS))]
    act = _lora_act(u, lay["mu_lora"], lc=lc, tr=tr, col0=lay["core"], wl=lay["wl"], lr=lay["lr"],
                    gl=lay["gl"])
    tm = _rwkv_terms(u, act, lay["vec"], lay["lw"], v_first, lc=lc, tr=tr, tc=tc, pool_w=pool_w,
                     rw=rw, lr=lay["lr"], gl=lay["gl"])
    y_f, y_b = _wkv_scan(tm, lc=lc, chunk=CHUNK, pairs=_pick(rw // LANES, SCAN_PAIRS, 1))
    out = _rwkv_out(y_f, y_b, tm["bonus"], tm["gate"], lay["vec"], tr=tr, tc=tc)
    mix_in = jnp.concatenate(pools + [out], axis=-1)
    mix = _matmul(mix_in.reshape(bsz * t, mix_w), p["w_out"].astype(BF16), out_dtype=F32,
                  tm=1024, tn=1024, tk=mix_w, name="out_proj").reshape(bsz, t, d)
    return mix, tm["v"]


def _forward(x, c, ctx, c_ctx, layers, grid_w):
    bsz, seq, d = x.shape
    lc = ctx.shape[1]
    depth = len(layers)
    alpha = (2 * depth) ** 0.25
    assert bsz < COND_ROWS and lc % CHUNK == 0 and seq % CHUNK == 0 and seq % grid_w == 0
    tr = _pick(math.gcd(lc, seq), 256, SUBLANES)
    tb = tr if tr % grid_w == 0 else grid_w
    assert seq % tb == 0 and tb % grid_w == 0
    ctx_row = bsz
    cond = jnp.zeros((COND_ROWS, d), F32).at[:bsz].set(c).at[ctx_row].set(c_ctx)
    x_all = jnp.concatenate([ctx, x], axis=1)
    pool_consts = _pool_consts(lc, seq, grid_w, tb)
    common = dict(tr=tr, lc=lc, ctx_row=ctx_row)
    v_first = None
    for i, p in enumerate(layers):
        last = i == depth - 1
        mix_w = p["w_out"].shape[0]
        pool_w = mix_w // 4
        rw = mix_w - pool_w
        assert pool_w % (len(POOL_WINDOWS) * LANES) == 0 and rw % LANES == 0
        tc = _pick(rw, 512, LANES)
        lay = _layer_layout(p, pool_w, rw)
        mod = _modulation(cond, p["w_ada"], p["b_ada"])
        mix, v_cur = _mixer(x_all, mod, p, lay, v_first, pool_consts, tc=tc, grid_w=grid_w, tb=tb,
                            **common)
        if v_first is None:
            v_first = v_cur
        if not last:
            x1, h2 = _residual_ln(x_all, mix, mod, p["ln1_g"], p["ln1_b"], alpha=alpha, k_gate=2,
                                  mod2=(3, 4), **common)
            t = x_all.shape[1]
            hid = _glu(h2.reshape(bsz * t, d), p["ffn_w1"].astype(BF16), p["ffn_w3"].astype(BF16),
                       tm=1024, tn=256, name="ffn_glu")
            ffn = _matmul(hid, p["ffn_w2"].astype(BF16), out_dtype=F32, tm=512, tn=1024, tk=5504,
                          name="ffn_down").reshape(bsz, t, d)
            x_all, = _residual_ln(x1, ffn, mod, p["ln2_g"], p["ln2_b"], alpha=alpha, k_gate=5, **common)
        else:
            x1, h2, route = _residual_ln(x_all, mix, mod, p["ln1_g"], p["ln1_b"], alpha=alpha, k_gate=2,
                                         mod2=(3, 4), router=p["router"], h_dtype=F32, x_off=lc,
                                         y_off=lc, rows=seq, **common)
            n_exp = p["router"].shape[1]
            ntok = bsz * seq
            tile = _pick(2 * ntok, 512, SUBLANES)
            src_token, pos, tile_expert, tile_valid = _route_plan(route.reshape(ntok, LANES), n_exp, tile)
            xs = _gather_rows(h2.reshape(ntok, d), src_token, rows=256)
            ys = _moe_experts(xs, p["exp_w1"].astype(BF16), p["exp_w3"].astype(BF16),
                              p["exp_w2"].astype(BF16), tile_expert, tile_valid, tile=tile,
                              tn_up=512, tn_down=1024)
            yg = _gather_rows(ys, pos, rows=256).reshape(2, bsz, seq, d)
            return _moe_residual_ln(x1, yg, route, mod, p["ln2_g"], p["ln2_b"], alpha=alpha, tr=tr,
                                    k_gate=5)
    return x_all[:, lc:]


_LAYER0 = ("w_ada", "b_ada", "w_in", "mu_prev", "mu_next", "pool_w", "pool_scale", "w0_f", "w2_f",
           "w0_b", "w2_b", "a0_f", "a2_f", "a0_b", "a2_b", "g_up", "k_k", "k_a", "r_k", "lnx_g",
           "lnx_b", "w_out", "ln1_g", "ln1_b", "ln2_g", "ln2_b", "ffn_w1", "ffn_w3", "ffn_w2")
_LAYER1 = _LAYER0[:26] + ("v0", "v2", "router", "exp_w1", "exp_w3", "exp_w2")


def kernel(x, c, ctx, c_ctx, l0_w_ada, l0_b_ada, l0_w_in, l0_mu_prev, l0_mu_next, l0_pool_w, l0_pool_scale, l0_w0_f, l0_w2_f, l0_w0_b, l0_w2_b, l0_a0_f, l0_a2_f, l0_a0_b, l0_a2_b, l0_g_up, l0_k_k, l0_k_a, l0_r_k, l0_lnx_g, l0_lnx_b, l0_w_out, l0_ln1_g, l0_ln1_b, l0_ln2_g, l0_ln2_b, l0_ffn_w1, l0_ffn_w3, l0_ffn_w2, l1_w_ada, l1_b_ada, l1_w_in, l1_mu_prev, l1_mu_next, l1_pool_w, l1_pool_scale, l1_w0_f, l1_w2_f, l1_w0_b, l1_w2_b, l1_a0_f, l1_a2_f, l1_a0_b, l1_a2_b, l1_g_up, l1_k_k, l1_k_a, l1_r_k, l1_lnx_g, l1_lnx_b, l1_w_out, l1_ln1_g, l1_ln1_b, l1_ln2_g, l1_ln2_b, l1_v0, l1_v2, l1_router, l1_exp_w1, l1_exp_w3, l1_exp_w2):
    l0 = dict(zip(_LAYER0, (l0_w_ada, l0_b_ada, l0_w_in, l0_mu_prev, l0_mu_next, l0_pool_w, l0_pool_scale, l0_w0_f, l0_w2_f, l0_w0_b, l0_w2_b, l0_a0_f, l0_a2_f, l0_a0_b, l0_a2_b, l0_g_up, l0_k_k, l0_k_a, l0_r_k, l0_lnx_g, l0_lnx_b, l0_w_out, l0_ln1_g, l0_ln1_b, l0_ln2_g, l0_ln2_b, l0_ffn_w1, l0_ffn_w3, l0_ffn_w2)))
    l1 = dict(zip(_LAYER1, (l1_w_ada, l1_b_ada, l1_w_in, l1_mu_prev, l1_mu_next, l1_pool_w, l1_pool_scale, l1_w0_f, l1_w2_f, l1_w0_b, l1_w2_b, l1_a0_f, l1_a2_f, l1_a0_b, l1_a2_b, l1_g_up, l1_k_k, l1_k_a, l1_r_k, l1_lnx_g, l1_lnx_b, l1_w_out, l1_ln1_g, l1_ln1_b, l1_ln2_g, l1_ln2_b, l1_v0, l1_v2, l1_router, l1_exp_w1, l1_exp_w3, l1_exp_w2)))
    return _forward(x, c, ctx, c_ctx, [l0, l1], GRID_W)
```

```python
import functools
import math

import numpy as np
import jax
import jax.numpy as jnp
from jax import lax
from jax.experimental import pallas as pl
from jax.experimental.pallas import tpu as pltpu

F32 = jnp.float32
BF16 = jnp.bfloat16

GRID_W = 64
POOL_WINDOWS = (2, 4, 8, 16)
HEAD = 64
LANES = 128
SUBLANES = 8
HALO = 16
N_ADA = 6
LN_EPS = 1e-5
GN_EPS = 64e-5
NORM_EPS = 1e-12
CHUNK = 64
SCAN_PAIRS = 8
SCAN_SUB = 2
INV_PASSES = 1
VMEM_LIMIT = 56 * 1024 * 1024
COND_ROWS = 16
PACK_GROUP = 1024


def _cparams(sem):
    return pltpu.CompilerParams(dimension_semantics=sem, vmem_limit_bytes=VMEM_LIMIT)


def _pick(n, target, mult):
    best = None
    for d in range(mult, min(n, target) + 1, mult):
        if n % d == 0:
            best = d
    return best if best is not None else n


def _dot(a, b):
    return jnp.dot(a, b, preferred_element_type=F32)


def _dot_nt(a, b):
    return lax.dot_general(a, b, (((1,), (1,)), ((), ())), preferred_element_type=F32)


def _split2(x):
    hi = x.astype(BF16)
    lo = (x - hi.astype(F32)).astype(BF16)
    return hi, lo


def _split3(x):
    hi = x.astype(BF16)
    r1 = x - hi.astype(F32)
    mid = r1.astype(BF16)
    lo = (r1 - mid.astype(F32)).astype(BF16)
    return hi, mid, lo


def _mm(a, b, passes=1):
    if passes == 1:
        return _dot(a.astype(BF16), b.astype(BF16))
    a_hi, a_lo = _split2(a)
    b_hi, b_lo = _split2(b)
    return _dot(a_hi, b_hi) + _dot(a_lo, b_hi) + _dot(a_hi, b_lo)


def _sigmoid(x):
    return 1.0 / (1.0 + jnp.exp(-x))


def _pack_pairs(x, group):
    half = group // 2
    words = []
    for g in range(x.shape[1] // group):
        lo = lax.bitcast_convert_type(x[:, g * group:g * group + half].astype(BF16).astype(F32), jnp.uint32)
        hi = lax.bitcast_convert_type(x[:, g * group + half:(g + 1) * group].astype(BF16).astype(F32), jnp.uint32)
        words.append((lo >> 16) | hi)
    return words[0] if len(words) == 1 else jnp.concatenate(words, axis=1)


def _unpack_pairs(p, group):
    half = group // 2
    cols = []
    for g in range(p.shape[1] // half):
        word = p[:, g * half:(g + 1) * half]
        cols.append(lax.bitcast_convert_type(word << 16, F32))
        cols.append(lax.bitcast_convert_type(word & jnp.uint32(0xFFFF0000), F32))
    return jnp.concatenate(cols, axis=1)


def _layer_norm_rows(x):
    mu = jnp.mean(x, axis=-1, keepdims=True)
    xc = x - mu
    var = jnp.mean(xc * xc, axis=-1, keepdims=True)
    return xc * lax.rsqrt(var + LN_EPS)


def _mod_kernel(c_ref, w_ref, b_ref, o_ref):
    c = c_ref[...]
    s = c * _sigmoid(c)
    o_ref[...] = _mm(s, w_ref[...], passes=3) + b_ref[...]


def _modulation(cond, w_ada, b_ada):
    d, n = w_ada.shape
    tn = _pick(n, 512, LANES)
    return pl.pallas_call(
        _mod_kernel,
        grid=(n // tn,),
        in_specs=[pl.BlockSpec((COND_ROWS, d), lambda j: (0, 0)),
                  pl.BlockSpec((d, tn), lambda j: (0, j)),
                  pl.BlockSpec((1, tn), lambda j: (0, j))],
        out_specs=pl.BlockSpec((COND_ROWS, tn), lambda j: (0, j)),
        out_shape=jax.ShapeDtypeStruct((COND_ROWS, n), F32),
        compiler_params=_cparams(("parallel",)),
        name="modulation",
    )(cond, w_ada, b_ada.reshape(1, n))


def _mod_row(mod_ref, row, chunk, d):
    return mod_ref[pl.ds(row, 1), chunk * d:(chunk + 1) * d]


def _ln_mod_kernel(x_ref, mod_ref, h_ref, *, nctx_blk, ctx_row, k_shift, k_scale):
    b, j = pl.program_id(0), pl.program_id(1)
    d = x_ref.shape[-1]
    row = jnp.where(j < nctx_blk, ctx_row, b)
    shift = _mod_row(mod_ref, row, k_shift, d)
    scale = _mod_row(mod_ref, row, k_scale, d)
    h_ref[...] = (_layer_norm_rows(x_ref[...]) * (1.0 + scale) + shift).astype(h_ref.dtype)


def _ln_mod(x_all, mod, *, lc, tr, ctx_row, k_shift, k_scale):
    bsz, t, d = x_all.shape
    kern = functools.partial(_ln_mod_kernel, nctx_blk=lc // tr, ctx_row=ctx_row,
                             k_shift=k_shift, k_scale=k_scale)
    return pl.pallas_call(
        kern,
        grid=(bsz, t // tr),
        in_specs=[pl.BlockSpec((None, tr, d), lambda b, j: (b, j, 0)),
                  pl.BlockSpec(mod.shape, lambda b, j: (0, 0))],
        out_specs=pl.BlockSpec((None, tr, d), lambda b, j: (b, j, 0)),
        out_shape=jax.ShapeDtypeStruct((bsz, t, d), BF16),
        compiler_params=_cparams(("parallel", "parallel")),
        name="ln_modulate",
    )(x_all, mod)


def _mm_kernel(a_ref, w_ref, o_ref, *scratch, nk):
    if nk == 1:
        o_ref[...] = _dot(a_ref[...], w_ref[...]).astype(o_ref.dtype)
        return
    acc_ref, = scratch
    k = pl.program_id(2)

    @pl.when(k == 0)
    def _():
        acc_ref[...] = jnp.zeros_like(acc_ref)

    acc_ref[...] += _dot(a_ref[...], w_ref[...])

    @pl.when(k == nk - 1)
    def _():
        o_ref[...] = acc_ref[...].astype(o_ref.dtype)


def _matmul(a, w, *, out_dtype, tm, tn, tk, name):
    m, kdim = a.shape
    n = w.shape[1]
    tm, tn, tk = _pick(m, tm, SUBLANES), _pick(n, tn, LANES), _pick(kdim, tk, LANES)
    nk = kdim // tk
    scratch = [pltpu.VMEM((tm, tn), F32)] if nk > 1 else []
    return pl.pallas_call(
        functools.partial(_mm_kernel, nk=nk),
        grid=(m // tm, n // tn, nk),
        in_specs=[pl.BlockSpec((tm, tk), lambda i, j, k: (i, k)),
                  pl.BlockSpec((tk, tn), lambda i, j, k: (k, j))],
        out_specs=pl.BlockSpec((tm, tn), lambda i, j, k: (i, j)),
        out_shape=jax.ShapeDtypeStruct((m, n), out_dtype),
        scratch_shapes=scratch,
        compiler_params=_cparams(("parallel", "parallel", "arbitrary")),
        name=name,
    )(a, w)


def _glu_kernel(a_ref, w1_ref, w3_ref, o_ref):
    a = a_ref[...]
    p1 = _dot(a, w1_ref[...])
    p3 = _dot(a, w3_ref[...])
    o_ref[...] = (p1 * _sigmoid(p1) * p3).astype(o_ref.dtype)


def _glu(a, w1, w3, *, tm, tn, name):
    m, kdim = a.shape
    f = w1.shape[1]
    tm, tn = _pick(m, tm, SUBLANES), _pick(f, tn, LANES)
    w_spec = pl.BlockSpec((kdim, tn), lambda i, j: (0, j))
    return pl.pallas_call(
        _glu_kernel,
        grid=(m // tm, f // tn),
        in_specs=[pl.BlockSpec((tm, kdim), lambda i, j: (i, 0)), w_spec, w_spec],
        out_specs=pl.BlockSpec((tm, tn), lambda i, j: (i, j)),
        out_shape=jax.ShapeDtypeStruct((m, f), BF16),
        compiler_params=_cparams(("parallel", "parallel")),
        name=name,
    )(a, w1, w3)


def _pool_consts(lc, tl, grid_w, tb):
    rows = tl // grid_w
    cb, c1, inv = [], [], []
    for win in POOL_WINDOWS:
        half = win // 2
        t = np.arange(tb)
        same_row = (t[:, None] // grid_w) == (t[None, :] // grid_w)
        dc = (t[None, :] % grid_w) - (t[:, None] % grid_w)
        cb.append((same_row & (dc >= -half) & (dc < half)).astype(np.float32))
        tc = np.arange(lc)
        d1 = tc[None, :] - tc[:, None]
        c1.append(((d1 >= -half) & (d1 < half)).astype(np.float32))
        cnt1 = np.minimum(tc + half, lc) - np.maximum(tc - half, 0)
        g = np.arange(grid_w)
        cntc = np.minimum(g + half, grid_w) - np.maximum(g - half, 0)
        r = np.arange(rows)
        cntr = np.minimum(r + half, rows) - np.maximum(r - half, 0)
        cnt2 = (cntr[:, None] * cntc[None, :]).reshape(-1)
        iv = 1.0 / np.concatenate([cnt1, cnt2]).astype(np.float64)
        inv.append(np.broadcast_to(iv[:, None], (lc + tl, LANES)).astype(np.float32))
    return (jnp.asarray(np.stack(cb), BF16), jnp.asarray(np.stack(c1), BF16),
            jnp.asarray(np.stack(inv), F32))


def _pool_kernel(u_ref, cb_ref, c1_ref, inv_ref, pw_ref, ps_ref, o_ref, s1_ref, acc_ref,
                 *, lc, tb, grid_w, half):
    t = u_ref.shape[0]
    tl = t - lc
    acc_ref[0:lc, :] = _dot(c1_ref[...], u_ref[0:lc, :])
    cb = cb_ref[...]
    for blk in range(tl // tb):
        lo, hi = lc + blk * tb, lc + (blk + 1) * tb
        s1_ref[lo:hi, :] = _dot(cb, u_ref[lo:hi, :])
    acc_ref[lc:t, :] = s1_ref[lc:t, :]
    for dr in range(-half, half):
        if dr == 0:
            continue
        sh = abs(dr) * grid_w
        if sh >= tl:
            continue
        if dr > 0:
            acc_ref[lc:t - sh, :] += s1_ref[lc + sh:t, :]
        else:
            acc_ref[lc + sh:t, :] += s1_ref[lc:t - sh, :]
    n_rep = u_ref.shape[1] // LANES
    diff = acc_ref[...] * jnp.concatenate([inv_ref[...]] * n_rep, axis=1) - u_ref[...].astype(F32)
    o_ref[...] = (_dot(diff.astype(BF16), pw_ref[...]) * ps_ref[...]).astype(o_ref.dtype)


def _pool_group(u_all, consts, pool_w, pool_scale, idx, *, lc, grid_w, tb, pg):
    bsz, t, _ = u_all.shape
    cb, c1, inv = consts
    half = POOL_WINDOWS[idx] // 2
    kern = functools.partial(_pool_kernel, lc=lc, tb=tb, grid_w=grid_w, half=half)
    return pl.pallas_call(
        kern,
        grid=(bsz,),
        in_specs=[pl.BlockSpec((None, t, pg), lambda b: (b, 0, idx)),
                  pl.BlockSpec((None, tb, tb), lambda b: (idx, 0, 0)),
                  pl.BlockSpec((None, lc, lc), lambda b: (idx, 0, 0)),
                  pl.BlockSpec((None, t, LANES), lambda b: (idx, 0, 0)),
                  pl.BlockSpec((None, pg, pg), lambda b: (idx, 0, 0)),
                  pl.BlockSpec((1, pg), lambda b: (0, idx))],
        out_specs=pl.BlockSpec((None, t, pg), lambda b: (b, 0, 0)),
        out_shape=jax.ShapeDtypeStruct((bsz, t, pg), BF16),
        scratch_shapes=[pltpu.VMEM((t, pg), F32), pltpu.VMEM((t, pg), F32)],
        compiler_params=_cparams(("parallel",)),
        name="pool_mixer_%d" % idx,
    )(u_all, cb, c1, inv, pool_w, pool_scale)


def _token_shift(f_ref, prev_ref, next_ref, mu_p, mu_n, seg_first, seg_last):
    f = f_ref[...].astype(F32)
    tr = f.shape[0]
    row = lax.broadcasted_iota(jnp.int32, f.shape, 0)
    prev_row = jnp.where(seg_first, 0.0, prev_ref[...].astype(F32)[HALO - 1:HALO, :])
    next_row = jnp.where(seg_last, 0.0, next_ref[...].astype(F32)[0:1, :])
    prev = jnp.where(row == 0, prev_row, pltpu.roll(f, 1, 0))
    nxt = jnp.where(row == tr - 1, next_row, pltpu.roll(f, tr - 1, 0))
    return f + mu_p * (prev - f) + mu_n * (nxt - f)


def _segment_flags(j, nctx_blk, n_blk):
    seg_first = jnp.logical_or(j == 0, j == nctx_blk)
    seg_last = jnp.logical_or(j == nctx_blk - 1, j == n_blk - 1)
    return seg_first, seg_last


def _shift_specs(tr, tc, col_blk, t):
    per_blk = tr // HALO
    last = t // HALO - 1
    cur = pl.BlockSpec((None, tr, tc), lambda b, j, c: (b, j, col_blk(c)))
    prv = pl.BlockSpec((None, HALO, tc),
                       lambda b, j, c: (b, jnp.maximum(j * per_blk - 1, 0), col_blk(c)))
    nxt = pl.BlockSpec((None, HALO, tc),
                       lambda b, j, c: (b, jnp.minimum((j + 1) * per_blk, last), col_blk(c)))
    return [cur, prv, nxt]


def _lora_kernel(u_ref, up_ref, un_ref, mu_ref, o_ref, *, nctx_blk, n_blk, lr, gl):
    j = pl.program_id(1)
    seg_first, seg_last = _segment_flags(j, nctx_blk, n_blk)
    z = _token_shift(u_ref, up_ref, un_ref, mu_ref[0:1, :], mu_ref[1:2, :], seg_first, seg_last)
    col = lax.broadcasted_iota(jnp.int32, z.shape, 1)
    act = jnp.where(col < 2 * lr, jnp.tanh(z),
                    jnp.where(jnp.logical_and(col >= 4 * lr, col < 4 * lr + gl), _sigmoid(z), z))
    o_ref[...] = act.astype(o_ref.dtype)


def _lora_act(u_all, mu_lora, *, lc, tr, col0, wl, lr, gl):
    bsz, t, _ = u_all.shape
    specs = _shift_specs(tr, wl, lambda c: col0 // wl, t)
    kern = functools.partial(_lora_kernel, nctx_blk=lc // tr, n_blk=t // tr, lr=lr, gl=gl)
    return pl.pallas_call(
        kern,
        grid=(bsz, t // tr, 1),
        in_specs=specs + [pl.BlockSpec((SUBLANES, wl), lambda b, j, c: (0, 0))],
        out_specs=pl.BlockSpec((None, tr, wl), lambda b, j, c: (b, j, 0)),
        out_shape=jax.ShapeDtypeStruct((bsz, t, wl), BF16),
        compiler_params=_cparams(("parallel", "parallel", "arbitrary")),
        name="lora_act",
    )(u_all, u_all, u_all, mu_lora)


def _head_ones():
    r = lax.broadcasted_iota(jnp.int32, (LANES, LANES), 0)
    c = lax.broadcasted_iota(jnp.int32, (LANES, LANES), 1)
    return jnp.where((r ^ c) < HEAD, 1.0, 0.0).astype(BF16)


def _head_sum(x, ones):
    outs = []
    for s in range(x.shape[1] // LANES):
        hi, mid, lo = _split3(x[:, s * LANES:(s + 1) * LANES])
        outs.append(_dot(hi, ones) + _dot(mid, ones) + _dot(lo, ones))
    return outs[0] if len(outs) == 1 else jnp.concatenate(outs, axis=1)


(_V_MPR, _V_MNR, _V_MPK, _V_MNK, _V_MPV, _V_MNV, _V_W0F, _V_W0B, _V_A0F, _V_A0B,
 _V_V0, _V_KK, _V_KA, _V_RK, _V_LG, _V_LB) = range(16)
_DECAY_SCALE = math.exp(-0.5)


def _terms_kernel(*refs, nctx_blk, n_blk, lr, gl, has_vres):
    (ur, urp, urn, uk, ukp, ukn, uv, uvp, uvn, act_ref, vec_ref,
     w2f_ref, w2b_ref, a2f_ref, a2b_ref, gup_ref) = refs[:16]
    pos = 16
    if has_vres:
        v2_ref, vfirst_ref = refs[pos:pos + 2]
        pos += 2
    (r_o, v_o, kk_o, lwf_o, lwb_o, kf_o, kb_o, af_o, ab_o, gate_o, bonus_o) = refs[pos:]

    j = pl.program_id(1)
    seg_first, seg_last = _segment_flags(j, nctx_blk, n_blk)
    vec = lambda i: vec_ref[i:i + 1, :]
    r = _token_shift(ur, urp, urn, vec(_V_MPR), vec(_V_MNR), seg_first, seg_last)
    k = _token_shift(uk, ukp, ukn, vec(_V_MPK), vec(_V_MNK), seg_first, seg_last)
    v = _token_shift(uv, uvp, uvn, vec(_V_MPV), vec(_V_MNV), seg_first, seg_last)

    act = act_ref[...]
    a_wf, a_wb = act[:, 0:lr], act[:, lr:2 * lr]
    a_af, a_ab = act[:, 2 * lr:3 * lr], act[:, 3 * lr:4 * lr]
    a_g = act[:, 4 * lr:4 * lr + gl]
    if has_vres:
        a_v = act[:, 4 * lr + gl:5 * lr + gl]
        v = v + (vfirst_ref[...].astype(F32) - v) * _sigmoid(vec(_V_V0) + _dot(a_v, v2_ref[...]))

    lwf_o[...] = -_DECAY_SCALE * _sigmoid(vec(_V_W0F) + _dot(a_wf, w2f_ref[...]))
    lwb_o[...] = -_DECAY_SCALE * _sigmoid(vec(_V_W0B) + _dot(a_wb, w2b_ref[...]))
    a_f = _sigmoid(vec(_V_A0F) + _dot(a_af, a2f_ref[...]))
    a_b = _sigmoid(vec(_V_A0B) + _dot(a_ab, a2b_ref[...]))

    ones = _head_ones()
    kkr = k * vec(_V_KK)
    norm = jnp.sqrt(_head_sum(kkr * kkr, ones))
    kk_o[...] = (kkr / jnp.maximum(norm, NORM_EPS)).astype(kk_o.dtype)
    k_f = k * (1.0 + (a_f - 1.0) * vec(_V_KA))
    k_b = k * (1.0 + (a_b - 1.0) * vec(_V_KA))
    bonus_o[...] = (_head_sum(r * (k_f + k_b) * vec(_V_RK), ones) * v).astype(bonus_o.dtype)
    gate_o[...] = _dot(a_g, gup_ref[...]).astype(gate_o.dtype)
    r_o[...] = r.astype(r_o.dtype)
    v_o[...] = v.astype(v_o.dtype)
    kf_o[...] = k_f.astype(kf_o.dtype)
    kb_o[...] = k_b.astype(kb_o.dtype)
    af_o[...] = a_f.astype(af_o.dtype)
    ab_o[...] = a_b.astype(ab_o.dtype)


def _rwkv_terms(u_all, act, vec, lw, v_first, *, lc, tr, tc, pool_w, rw, lr, gl):
    bsz, t, _ = u_all.shape
    has_vres = v_first is not None
    nb = rw // tc
    specs = []
    for part in range(3):
        base = (pool_w + part * rw) // tc
        specs += _shift_specs(tr, tc, (lambda c, base=base: base + c), t)
    specs.append(pl.BlockSpec((None, tr, act.shape[-1]), lambda b, j, c: (b, j, 0)))
    specs.append(pl.BlockSpec((vec.shape[0], tc), lambda b, j, c: (0, c)))
    wspec = lambda rows: pl.BlockSpec((rows, tc), lambda b, j, c: (0, c))
    specs += [wspec(lr), wspec(lr), wspec(lr), wspec(lr), wspec(gl)]
    args = [u_all] * 9 + [act, vec, lw["w2_f"], lw["w2_b"], lw["a2_f"], lw["a2_b"], lw["g_up"]]
    if has_vres:
        specs += [wspec(lr), pl.BlockSpec((None, tr, tc), lambda b, j, c: (b, j, c))]
        args += [lw["v2"], v_first]
    out_spec = pl.BlockSpec((None, tr, tc), lambda b, j, c: (b, j, c))
    kern = functools.partial(_terms_kernel, nctx_blk=lc // tr, n_blk=t // tr, lr=lr, gl=gl,
                             has_vres=has_vres)
    names = ("r", "v", "kk", "lw_f", "lw_b", "k_f", "k_b", "a_f", "a_b", "gate", "bonus")
    dtypes = [F32 if n.startswith("lw") else BF16 for n in names]
    outs = pl.pallas_call(
        kern,
        grid=(bsz, t // tr, nb),
        in_specs=specs,
        out_specs=[out_spec] * len(names),
        out_shape=[jax.ShapeDtypeStruct((bsz, t, rw), dt) for dt in dtypes],
        compiler_params=_cparams(("parallel", "parallel", "parallel")),
        name="rwkv_terms",
    )(*args)
    return dict(zip(names, outs))


def _cumsum_rows(x, rev):
    n = x.shape[0]
    row = lax.broadcasted_iota(jnp.int32, x.shape, 0)
    s = 1
    while s < n:
        if rev:
            x = x + jnp.where(row < n - s, pltpu.roll(x, n - s, 0), 0.0)
        else:
            x = x + jnp.where(row >= s, pltpu.roll(x, s, 0), 0.0)
        s *= 2
    return x


def _wkv_chunks(chains):
    n = chains[0][0].shape[0]
    hp = 2 * n
    lane = lax.broadcasted_iota(jnp.int32, (n, LANES), 1)
    first_head = lane < HEAD
    ri = lax.broadcasted_iota(jnp.int32, (hp, hp), 0)
    ci = lax.broadcasted_iota(jnp.int32, (hp, hp), 1)
    blk = ri ^ ci
    eye = ri == ci
    tdiff = (ci & (n - 1)) - (ri & (n - 1))
    dist = {False: jnp.where(blk < n, tdiff, hp), True: jnp.where(blk < n, -tdiff, hp)}

    def stack(x):
        return jnp.concatenate([jnp.where(first_head, x, 0.0), jnp.where(first_head, 0.0, x)], axis=0)

    pre = []
    for r, v, kk, lw, k, a, state, rev in chains:
        c = _cumsum_rows(lw, rev)
        ctot = c[0:1, :] if rev else c[n - 1:n, :]
        e_pos, e_neg = jnp.exp(c), jnp.exp(-c)
        e_prev, e_rem = jnp.exp(c - lw), jnp.exp(ctot - c)
        kka = kk * a
        bt, kt = kka * e_neg, k * e_neg
        pre.append(dict(s_at=stack(-kk * e_prev), s_rt=stack(r * e_pos), s_v=stack(v), rt=r * e_pos,
                        rhs1=jnp.concatenate([bt, bt, kt, kt], axis=0).astype(BF16),
                        lhs_t=jnp.concatenate([stack(kka * e_rem), stack(k * e_rem)], axis=0),
                        wtot=jnp.exp(ctot), strict=dist[rev] < 0, incl=dist[rev] <= 0))

    s1 = [_dot_nt(jnp.concatenate([p["s_at"], p["s_rt"]], axis=0).astype(BF16), p["rhs1"]) for p in pre]
    nmat = [jnp.where(p["strict"], s[:hp, :hp], 0.0) for p, s in zip(pre, s1)]
    m_ak = [jnp.where(p["strict"], s[:hp, hp:], 0.0) for p, s in zip(pre, s1)]
    m_rbk = [jnp.concatenate([jnp.where(p["incl"], s[hp:, :hp], 0.0), jnp.where(p["incl"], s[hp:, hp:], 0.0)],
                             axis=1).astype(BF16) for p, s in zip(pre, s1)]
    z = [_mm(m, p["s_v"]) for m, p in zip(m_ak, pre)]

    n8 = [jnp.where(blk < SUBLANES, m, 0.0) for m in nmat]
    tmat = [jnp.where(eye, 1.0, m) for m in n8]
    n2 = [_mm(m, m, INV_PASSES) for m in n8]
    st = [_mm(jnp.concatenate([t, m], axis=0), m, INV_PASSES) for t, m in zip(tmat, n2)]
    tmat = [t + s[:hp] for t, s in zip(tmat, st)]
    tmat = [t + _mm(t, s[hp:], INV_PASSES) for t, s in zip(tmat, st)]
    size = SUBLANES
    while size < n:
        sel = jnp.logical_and(blk >= size, blk < 2 * size)
        pieces = [[(g * 2 * size + (0 if chain[7] else size), g * 2 * size + (size if chain[7] else 2 * size))
                   for g in range(hp // (2 * size))] for chain in chains]

        def take(mat, rows):
            return jnp.concatenate([mat[a:b] for a, b in rows], axis=0)

        def put(mat, rows, part, keep):
            out, at, prev = [], 0, 0
            for a, b in rows:
                if a > prev:
                    out.append(mat[prev:a] if keep else jnp.zeros((a - prev, mat.shape[1]), F32))
                out.append(part[at:at + b - a])
                at, prev = at + b - a, b
            if prev < mat.shape[0]:
                out.append(mat[prev:] if keep else jnp.zeros((mat.shape[0] - prev, mat.shape[1]), F32))
            return jnp.concatenate(out, axis=0)

        x = [_mm(take(jnp.where(sel, m, 0.0), rows), t, INV_PASSES) for m, t, rows in zip(nmat, tmat, pieces)]
        x = [put(t, rows, xx, keep=False) for t, rows, xx in zip(tmat, pieces, x)]
        th = [take(t, rows) for t, rows in zip(tmat, pieces)]
        th = [h + _mm(h, xx, INV_PASSES) for h, xx in zip(th, x)]
        tmat = [put(t, rows, h, keep=True) for t, rows, h in zip(tmat, pieces, th)]
        size *= 2

    gu = [_mm(t, jnp.concatenate([p["s_at"], zz], axis=1)) for t, p, zz in zip(tmat, pre, z)]
    zeros = jnp.zeros((hp, LANES), F32)
    rhs4 = [jnp.concatenate([g, jnp.concatenate([zeros, p["s_v"]], axis=1)], axis=0).astype(BF16)
            for g, p in zip(gu, pre)]
    tb = [_dot(jnp.concatenate([m, p["lhs_t"].T.astype(BF16)], axis=0), rr)
          for m, p, rr in zip(m_rbk, pre, rhs4)]
    top = [x_[:hp] for x_ in tb]
    bot = [x_[hp:] for x_ in tb]

    outs = []
    for p, tp, bt_, chain in zip(pre, top, bot, chains):
        q = p["rt"] + tp[:n, :LANES] + tp[n:, :LANES]
        amat = jnp.where(eye, jnp.broadcast_to(p["wtot"], (LANES, LANES)), 0.0) + bt_[:, :LANES]
        qa = jnp.concatenate([q, amat], axis=0).astype(BF16)
        s_hi, s_lo = _split2(chain[6])
        outs.append(_dot(qa, s_hi) + _dot(qa, s_lo))
    return [(o[:n] + tp[:n, LANES:] + tp[n:, LANES:], o[n:] + bt_[:, LANES:])
            for o, tp, bt_ in zip(outs, top, bot)]


def _scan_kernel(*refs, pairs, chunk):
    ins, (yf_ref, yb_ref, state_ref) = refs[:12], refs[12:]
    s = pl.program_id(1)
    n_sub = yf_ref.shape[0] // chunk

    @pl.when(s == 0)
    def _():
        state_ref[...] = jnp.zeros_like(state_ref)

    chains = [(d, p, slice(p * LANES, (p + 1) * LANES)) for d in range(2) for p in range(pairs)]

    def sub_step(i, carry):
        rows = [pl.ds(pl.multiple_of((n_sub - 1 - i if d else i) * chunk, chunk), chunk) for d in range(2)]
        loaded = [tuple(ref[rows[d], cols].astype(F32) for ref in ins[6 * d:6 * d + 6])
                  + (state_ref[d, p], bool(d)) for d, p, cols in chains]
        for (d, p, cols), (y, new_state) in zip(chains, _wkv_chunks(loaded)):
            y_ref = yb_ref if d else yf_ref
            y_ref[rows[d], cols] = y.astype(y_ref.dtype)
            state_ref[d, p] = new_state
        return carry

    lax.fori_loop(0, n_sub, sub_step, 0)


def _wkv_scan(tm, *, lc, chunk, pairs):
    bsz, t, rw = tm["r"].shape
    width = pairs * LANES
    ngrp = rw // width
    n_sub = math.gcd(math.gcd(lc // chunk, (t - lc) // chunk), SCAN_SUB)
    step_rows = n_sub * chunk
    nctx, ntot = lc // step_rows, t // step_rows

    def fwd(g, s):
        return (g // ngrp, s, g % ngrp)

    def bwd(g, s):
        return (g // ngrp, jnp.where(s < nctx, nctx - 1 - s, ntot - 1 - (s - nctx)), g % ngrp)

    blk = (None, step_rows, width)
    names_f = ("r", "v", "kk", "lw_f", "k_f", "a_f")
    names_b = ("r", "v", "kk", "lw_b", "k_b", "a_b")
    in_specs = [pl.BlockSpec(blk, fwd)] * 6 + [pl.BlockSpec(blk, bwd)] * 6
    args = [tm[n] for n in names_f] + [tm[n] for n in names_b]
    return pl.pallas_call(
        functools.partial(_scan_kernel, pairs=pairs, chunk=chunk),
        grid=(bsz * ngrp, ntot),
        in_specs=in_specs,
        out_specs=[pl.BlockSpec(blk, fwd), pl.BlockSpec(blk, bwd)],
        out_shape=[jax.ShapeDtypeStruct((bsz, t, rw), BF16)] * 2,
        scratch_shapes=[pltpu.VMEM((2, pairs, LANES, LANES), F32)],
        compiler_params=_cparams(("parallel", "arbitrary")),
        name="wkv7_scan",
    )(*args)


def _rwkv_out_kernel(yf_ref, yb_ref, bonus_ref, gate_ref, vec_ref, o_ref):
    y = yf_ref[...].astype(F32) + yb_ref[...].astype(F32)
    ones = _head_ones()
    inv_n = 1.0 / HEAD
    mu = _head_sum(y, ones) * inv_n
    yc = y - mu
    var = _head_sum(yc * yc, ones) * inv_n
    yn = yc * lax.rsqrt(var + GN_EPS)
    yn = yn * vec_ref[_V_LG:_V_LG + 1, :] + vec_ref[_V_LB:_V_LB + 1, :]
    o_ref[...] = ((yn + bonus_ref[...].astype(F32)) * gate_ref[...].astype(F32)).astype(o_ref.dtype)


def _rwkv_out(y_f, y_b, bonus, gate, vec, *, tr, tc):
    bsz, t, rw = y_f.shape
    spec = pl.BlockSpec((None, tr, tc), lambda b, j, c: (b, j, c))
    return pl.pallas_call(
        _rwkv_out_kernel,
        grid=(bsz, t // tr, rw // tc),
        in_specs=[spec] * 4 + [pl.BlockSpec((vec.shape[0], tc), lambda b, j, c: (0, c))],
        out_specs=spec,
        out_shape=jax.ShapeDtypeStruct((bsz, t, rw), BF16),
        compiler_params=_cparams(("parallel", "parallel", "parallel")),
        name="rwkv_out",
    )(y_f, y_b, bonus, gate, vec)


_R_E1, _R_E2, _R_G1, _R_G2 = range(4)

def _res_kernel(*refs, alpha, row_off_blk, nctx_blk, ctx_row, k_gate, k_shift, k_scale,
                with_mod, n_exp):
    x_ref, y_ref, mod_ref, gb_ref = refs[:4]
    pos = 4
    if n_exp:
        router_ref = refs[pos]
        pos += 1
    outs = refs[pos:]
    b, j = pl.program_id(0), pl.program_id(1)
    d = x_ref.shape[-1]
    row = jnp.where(j + row_off_blk < nctx_blk, ctx_row, b)
    gate = _mod_row(mod_ref, row, k_gate, d)
    xn = _layer_norm_rows(alpha * x_ref[...] + gate * y_ref[...]) * gb_ref[0:1, :] + gb_ref[1:2, :]
    outs[0][...] = xn
    if not with_mod:
        return
    h = _layer_norm_rows(xn) * (1.0 + _mod_row(mod_ref, row, k_scale, d)) + _mod_row(mod_ref, row, k_shift, d)
    if outs[1].dtype == jnp.uint32:
        outs[1][...] = _pack_pairs(h, min(PACK_GROUP, d))
    else:
        outs[1][...] = h.astype(outs[1].dtype)
    if not n_exp:
        return
    logits = _mm(h, router_ref[...], passes=3)
    lane = lax.broadcasted_iota(jnp.int32, logits.shape, 1).astype(F32)
    neg = -jnp.inf
    logits = jnp.where(lane < n_exp, logits, neg)
    m1 = jnp.max(logits, axis=-1, keepdims=True)
    i1 = jnp.min(jnp.where(logits == m1, lane, float(LANES)), axis=-1, keepdims=True)
    rest = jnp.where(lane == i1, neg, logits)
    m2 = jnp.max(rest, axis=-1, keepdims=True)
    i2 = jnp.min(jnp.where(rest == m2, lane, float(LANES)), axis=-1, keepdims=True)
    e2 = jnp.exp(m2 - m1)
    g1 = 1.0 / (1.0 + e2)
    g2 = e2 / (1.0 + e2)
    outs[2][...] = jnp.where(lane == _R_E1, i1, jnp.where(lane == _R_E2, i2,
                             jnp.where(lane == _R_G1, g1, jnp.where(lane == _R_G2, g2, 0.0))))


def _residual_ln(x, y, mod, gain, bias, *, alpha, tr, lc, ctx_row, k_gate, x_off=0, y_off=0,
                 rows=None, mod2=None, router=None, pack_h=False):
    bsz, _, d = x.shape
    rows = x.shape[1] if rows is None else rows
    n_exp = 0 if router is None else router.shape[1]
    gb = jnp.stack([gain, bias])
    in_specs = [pl.BlockSpec((None, tr, d), lambda b, j: (b, j + x_off // tr, 0)),
                pl.BlockSpec((None, tr, d), lambda b, j: (b, j + y_off // tr, 0)),
                pl.BlockSpec(mod.shape, lambda b, j: (0, 0)),
                pl.BlockSpec((2, d), lambda b, j: (0, 0))]
    args = [x, y, mod, gb]
    out_specs = [pl.BlockSpec((None, tr, d), lambda b, j: (b, j, 0))]
    out_shape = [jax.ShapeDtypeStruct((bsz, rows, d), F32)]
    if mod2 is not None:
        h_cols, h_dtype = (d // 2, jnp.uint32) if pack_h else (d, BF16)
        out_specs.append(pl.BlockSpec((None, tr, h_cols), lambda b, j: (b, j, 0)))
        out_shape.append(jax.ShapeDtypeStruct((bsz, rows, h_cols), h_dtype))
    if router is not None:
        rpad = jnp.zeros((d, LANES), F32).at[:, :n_exp].set(router)
        in_specs.append(pl.BlockSpec((d, LANES), lambda b, j: (0, 0)))
        args.append(rpad)
        out_specs.append(pl.BlockSpec((None, tr, LANES), lambda b, j: (b, j, 0)))
        out_shape.append(jax.ShapeDtypeStruct((bsz, rows, LANES), F32))
    k_shift, k_scale = mod2 if mod2 is not None else (0, 0)
    kern = functools.partial(_res_kernel, alpha=alpha, row_off_blk=x_off // tr, nctx_blk=lc // tr,
                             ctx_row=ctx_row, k_gate=k_gate, k_shift=k_shift, k_scale=k_scale,
                             with_mod=mod2 is not None, n_exp=n_exp)
    return pl.pallas_call(
        kern,
        grid=(bsz, rows // tr),
        in_specs=in_specs,
        out_specs=out_specs,
        out_shape=out_shape,
        compiler_params=_cparams(("parallel", "parallel")),
        name="residual_ln",
    )(*args)


def _route_plan(route, n_exp, tile):
    n = route.shape[0]
    e_flat = jnp.concatenate([route[:, _R_E1], route[:, _R_E2]]).astype(jnp.int32)
    experts = jnp.arange(n_exp, dtype=jnp.int32)
    onehot = (e_flat[:, None] == experts[None, :]).astype(jnp.int32)
    csum = jnp.cumsum(onehot, axis=0)
    rank = jnp.sum((csum - onehot) * onehot, axis=1)
    tiles_e = (csum[-1] + tile - 1) // tile
    tile_end = jnp.cumsum(tiles_e)
    tile_start = tile_end - tiles_e
    pos = jnp.sum(onehot * tile_start[None, :], axis=1) * tile + rank
    n_tiles = (2 * n) // tile + n_exp
    tidx = jnp.arange(n_tiles, dtype=jnp.int32)
    valid = tidx < tile_end[-1]
    last_e = jnp.max(jnp.where(tiles_e > 0, experts, 0))
    tile_expert = jnp.sum((tidx[:, None] >= tile_end[None, :]).astype(jnp.int32), axis=1)
    tile_expert = jnp.where(valid, tile_expert, last_e)
    token = jnp.arange(2 * n, dtype=jnp.int32) % n
    src_token = jnp.zeros((n_tiles * tile,), jnp.int32).at[pos].set(token)
    return src_token, pos, tile_expert, valid.astype(jnp.int32)


def _gather_kernel(idx_ref, src_ref, out_ref, sem):
    rows = out_ref.shape[0]

    def row_copy(j, src_row):
        return pltpu.make_async_copy(src_ref.at[pl.ds(src_row, 1)], out_ref.at[pl.ds(j, 1)], sem)

    def start(j, carry):
        row_copy(j, idx_ref[0, j]).start()
        return carry

    def wait(j, carry):
        row_copy(j, 0).wait()
        return carry

    lax.fori_loop(0, rows, start, 0, unroll=8)
    lax.fori_loop(0, rows, wait, 0, unroll=8)


def _gather_rows(src, idx, *, rows):
    n_out = idx.shape[0]
    d = src.shape[1]
    rows = _pick(n_out, rows, SUBLANES)
    return pl.pallas_call(
        _gather_kernel,
        grid=(n_out // rows,),
        in_specs=[pl.BlockSpec((None, 1, rows), lambda i: (i, 0, 0), memory_space=pltpu.SMEM),
                  pl.BlockSpec(memory_space=pl.ANY)],
        out_specs=pl.BlockSpec((rows, d), lambda i: (i, 0)),
        out_shape=jax.ShapeDtypeStruct((n_out, d), src.dtype),
        scratch_shapes=[pltpu.SemaphoreType.DMA(())],
        compiler_params=_cparams(("arbitrary",)),
        name="row_gather",
    )(idx.reshape(n_out // rows, 1, rows), src)


def _moe_glu_kernel(te_ref, tv_ref, a_ref, w1_ref, w3_ref, o_ref):
    i = pl.program_id(1)

    @pl.when(tv_ref[i] != 0)
    def _():
        a = _unpack_pairs(a_ref[...], min(PACK_GROUP, 2 * a_ref.shape[1])).astype(BF16)
        p1 = _dot(a, w1_ref[...])
        p3 = _dot(a, w3_ref[...])
        o_ref[...] = (p1 * _sigmoid(p1) * p3).astype(o_ref.dtype)

    @pl.when(tv_ref[i] == 0)
    def _():
        o_ref[...] = jnp.zeros_like(o_ref)


def _moe_down_kernel(te_ref, tv_ref, h_ref, w2_ref, o_ref):
    i = pl.program_id(1)

    @pl.when(tv_ref[i] != 0)
    def _():
        o_ref[...] = _pack_pairs(_dot(h_ref[...], w2_ref[...]), min(PACK_GROUP, 2 * o_ref.shape[1]))

    @pl.when(tv_ref[i] == 0)
    def _():
        o_ref[...] = jnp.zeros_like(o_ref)


def _moe_experts(xs, w1, w3, w2, tile_expert, tile_valid, *, tile, tn_up, tn_down):
    r, d_half = xs.shape
    d = 2 * d_half
    _, _, f = w1.shape
    n_tiles = r // tile
    tn_up, tn_down = _pick(f, tn_up, LANES), _pick(d, tn_down, min(PACK_GROUP, d))
    up_spec = pl.BlockSpec((None, d, tn_up), lambda j, i, te, tv: (te[i], 0, j))
    hid = pl.pallas_call(
        _moe_glu_kernel,
        grid_spec=pltpu.PrefetchScalarGridSpec(
            num_scalar_prefetch=2, grid=(f // tn_up, n_tiles),
            in_specs=[pl.BlockSpec((tile, d_half), lambda j, i, te, tv: (i, 0)), up_spec, up_spec],
            out_specs=pl.BlockSpec((tile, tn_up), lambda j, i, te, tv: (i, j))),
        out_shape=jax.ShapeDtypeStruct((r, f), BF16),
        compiler_params=_cparams(("parallel", "arbitrary")),
        name="moe_glu",
    )(tile_expert, tile_valid, xs, w1, w3)
    return pl.pallas_call(
        _moe_down_kernel,
        grid_spec=pltpu.PrefetchScalarGridSpec(
            num_scalar_prefetch=2, grid=(d // tn_down, n_tiles),
            in_specs=[pl.BlockSpec((tile, f), lambda j, i, te, tv: (i, 0)),
                      pl.BlockSpec((None, f, tn_down), lambda j, i, te, tv: (te[i], 0, j))],
            out_specs=pl.BlockSpec((tile, tn_down // 2), lambda j, i, te, tv: (i, j))),
        out_shape=jax.ShapeDtypeStruct((r, d_half), jnp.uint32),
        compiler_params=_cparams(("parallel", "arbitrary")),
        name="moe_down",
    )(tile_expert, tile_valid, hid, w2)


def _moe_out_kernel(x_ref, y1_ref, y2_ref, route_ref, mod_ref, gb_ref, o_ref, *, alpha, k_gate):
    b = pl.program_id(0)
    d = x_ref.shape[-1]
    route = route_ref[...]
    group = min(PACK_GROUP, d)
    y = (route[:, _R_G1:_R_G1 + 1] * _unpack_pairs(y1_ref[...], group)
         + route[:, _R_G2:_R_G2 + 1] * _unpack_pairs(y2_ref[...], group))
    gate = _mod_row(mod_ref, b, k_gate, d)
    o_ref[...] = _layer_norm_rows(alpha * x_ref[...] + gate * y) * gb_ref[0:1, :] + gb_ref[1:2, :]


def _moe_residual_ln(x, yg, route, mod, gain, bias, *, alpha, tr, k_gate):
    bsz, rows, d = x.shape
    row_spec = pl.BlockSpec((None, tr, d), lambda b, j: (b, j, 0))
    return pl.pallas_call(
        functools.partial(_moe_out_kernel, alpha=alpha, k_gate=k_gate),
        grid=(bsz, rows // tr),
        in_specs=[row_spec,
                  pl.BlockSpec((None, None, tr, d // 2), lambda b, j: (0, b, j, 0)),
                  pl.BlockSpec((None, None, tr, d // 2), lambda b, j: (1, b, j, 0)),
                  pl.BlockSpec((None, tr, LANES), lambda b, j: (b, j, 0)),
                  pl.BlockSpec(mod.shape, lambda b, j: (0, 0)),
                  pl.BlockSpec((2, d), lambda b, j: (0, 0))],
        out_specs=row_spec,
        out_shape=jax.ShapeDtypeStruct((bsz, rows, d), F32),
        compiler_params=_cparams(("parallel", "parallel")),
        name="moe_residual_ln",
    )(x, yg, yg, route, mod, jnp.stack([gain, bias]))


def _pad_rows(w, rows):
    return jnp.zeros((rows,) + w.shape[1:], w.dtype).at[:w.shape[0]].set(w)


def _layer_layout(p, pool_w, rw):
    lr_raw = p["w2_f"].shape[0]
    gl_raw = p["g_up"].shape[0]
    vr_raw = p["v2"].shape[0] if "v2" in p else 0
    lr = -(-max(lr_raw, vr_raw, 1) // LANES) * LANES
    gl = -(-gl_raw // LANES) * LANES
    wl = 5 * lr + gl
    wl_pad = -(-wl // 1024) * 1024 if wl > 512 else wl
    core = pool_w + 3 * rw
    sizes = [lr_raw] * 4 + [gl_raw] + ([vr_raw] if vr_raw else [])
    slots = [lr] * 4 + [gl] + [lr]
    d = p["w_in"].shape[0]

    def relayout(src, rows_shape):
        dst = jnp.zeros(rows_shape + (core + wl_pad,), src.dtype)
        dst = dst.at[..., :core].set(src[..., :core])
        s_off, d_off = core, core
        for size, slot in zip(sizes, slots):
            dst = dst.at[..., d_off:d_off + size].set(src[..., s_off:s_off + size])
            s_off += size
            d_off += slot
        return dst

    w_in = relayout(p["w_in"], (d,)).astype(BF16)
    zero_pool = jnp.zeros((pool_w,), F32)
    mu_p = relayout(jnp.concatenate([zero_pool, p["mu_prev"]]), ())
    mu_n = relayout(jnp.concatenate([zero_pool, p["mu_next"]]), ())
    mu_lora = jnp.zeros((SUBLANES, wl_pad), F32).at[0].set(mu_p[core:]).at[1].set(mu_n[core:])
    seg = lambda a, i: a[pool_w + i * rw:pool_w + (i + 1) * rw]
    zeros = jnp.zeros((rw,), F32)
    vec = jnp.stack([seg(mu_p, 0), seg(mu_n, 0), seg(mu_p, 1), seg(mu_n, 1), seg(mu_p, 2), seg(mu_n, 2),
                     p["w0_f"], p["w0_b"], p["a0_f"], p["a0_b"], p.get("v0", zeros),
                     p["k_k"], p["k_a"], p["r_k"].reshape(-1), p["lnx_g"], p["lnx_b"]])
    lw = {n: _pad_rows(p[n], lr).astype(BF16) for n in ("w2_f", "w2_b", "a2_f", "a2_b")}
    lw["g_up"] = _pad_rows(p["g_up"], gl).astype(BF16)
    if vr_raw:
        lw["v2"] = _pad_rows(p["v2"], lr).astype(BF16)
    return dict(w_in=w_in, mu_lora=mu_lora, vec=vec, lw=lw, lr=lr, gl=gl, wl=wl_pad, core=core)


def _mixer(x_all, mod, p, lay, v_first, pool_consts, *, lc, tr, tc, grid_w, tb, ctx_row):
    bsz, t, d = x_all.shape
    mix_w = p["w_out"].shape[0]
    pool_w = mix_w // 4
    pg = pool_w // len(POOL_WINDOWS)
    rw = mix_w - pool_w
    h = _ln_mod(x_all, mod, lc=lc, tr=tr, ctx_row=ctx_row, k_shift=0, k_scale=1)
    u = _matmul(h.reshape(bsz * t, d), lay["w_in"], out_dtype=BF16, tm=1024, tn=1024, tk=d,
                name="in_proj").reshape(bsz, t, -1)
    pool_wts = p["pool_w"].astype(BF16)
    pool_scale = p["pool_scale"].reshape(1, pool_w)
    pools = [_pool_group(u, pool_consts, pool_wts, pool_scale, i, lc=lc, grid_w=grid_w, tb=tb, pg=pg)
             for i in range(len(POOL_WINDOWS))]
    act = _lora_act(u, lay["mu_lora"], lc=lc, tr=tr, col0=lay["core"], wl=lay["wl"], lr=lay["lr"],
                    gl=lay["gl"])
    tm = _rwkv_terms(u, act, lay["vec"], lay["lw"], v_first, lc=lc, tr=tr, tc=tc, pool_w=pool_w,
                     rw=rw, lr=lay["lr"], gl=lay["gl"])
    y_f, y_b = _wkv_scan(tm, lc=lc, chunk=CHUNK, pairs=_pick(rw // LANES, SCAN_PAIRS, 1))
    out = _rwkv_out(y_f, y_b, tm["bonus"], tm["gate"], lay["vec"], tr=tr, tc=tc)
    mix_in = jnp.concatenate(pools + [out], axis=-1)
    mix = _matmul(mix_in.reshape(bsz * t, mix_w), p["w_out"].astype(BF16), out_dtype=BF16,
                  tm=1024, tn=1024, tk=mix_w, name="out_proj").reshape(bsz, t, d)
    return mix, tm["v"]


def _forward(x, c, ctx, c_ctx, layers, grid_w):
    bsz, seq, d = x.shape
    lc = ctx.shape[1]
    depth = len(layers)
    alpha = (2 * depth) ** 0.25
    assert bsz < COND_ROWS and lc % CHUNK == 0 and seq % CHUNK == 0 and seq % grid_w == 0
    tr = _pick(math.gcd(lc, seq), 256, SUBLANES)
    tb = tr if tr % grid_w == 0 else grid_w
    assert seq % tb == 0 and tb % grid_w == 0
    ctx_row = bsz
    cond = jnp.zeros((COND_ROWS, d), F32).at[:bsz].set(c).at[ctx_row].set(c_ctx)
    x_all = jnp.concatenate([ctx, x], axis=1)
    pool_consts = _pool_consts(lc, seq, grid_w, tb)
    common = dict(tr=tr, lc=lc, ctx_row=ctx_row)
    v_first = None
    for i, p in enumerate(layers):
        last = i == depth - 1
        mix_w = p["w_out"].shape[0]
        pool_w = mix_w // 4
        rw = mix_w - pool_w
        assert pool_w % (len(POOL_WINDOWS) * LANES) == 0 and rw % LANES == 0
        tc = _pick(rw, 512, LANES)
        lay = _layer_layout(p, pool_w, rw)
        mod = _modulation(cond, p["w_ada"], p["b_ada"])
        mix, v_cur = _mixer(x_all, mod, p, lay, v_first, pool_consts, tc=tc, grid_w=grid_w, tb=tb,
                            **common)
        if v_first is None:
            v_first = v_cur
        if not last:
            x1, h2 = _residual_ln(x_all, mix, mod, p["ln1_g"], p["ln1_b"], alpha=alpha, k_gate=2,
                                  mod2=(3, 4), **common)
            t = x_all.shape[1]
            hid = _glu(h2.reshape(bsz * t, d), p["ffn_w1"].astype(BF16), p["ffn_w3"].astype(BF16),
                       tm=1024, tn=256, name="ffn_glu")
            ffn = _matmul(hid, p["ffn_w2"].astype(BF16), out_dtype=BF16, tm=512, tn=1024, tk=5504,
                          name="ffn_down").reshape(bsz, t, d)
            x_all, = _residual_ln(x1, ffn, mod, p["ln2_g"], p["ln2_b"], alpha=alpha, k_gate=5, **common)
        else:
            x1, h2, route = _residual_ln(x_all, mix, mod, p["ln1_g"], p["ln1_b"], alpha=alpha, k_gate=2,
                                         mod2=(3, 4), router=p["router"], pack_h=True, x_off=lc,
                                         y_off=lc, rows=seq, **common)
            n_exp = p["router"].shape[1]
            ntok = bsz * seq
            tile = _pick(2 * ntok, 512, SUBLANES)
            src_token, pos, tile_expert, tile_valid = _route_plan(route.reshape(ntok, LANES), n_exp, tile)
            xs = _gather_rows(h2.reshape(ntok, d // 2), src_token, rows=256)
            ys = _moe_experts(xs, p["exp_w1"].astype(BF16), p["exp_w3"].astype(BF16),
                              p["exp_w2"].astype(BF16), tile_expert, tile_valid, tile=tile,
                              tn_up=512, tn_down=1024)
            yg = _gather_rows(ys, pos, rows=256).reshape(2, bsz, seq, d // 2)
            return _moe_residual_ln(x1, yg, route, mod, p["ln2_g"], p["ln2_b"], alpha=alpha, tr=tr,
                                    k_gate=5)
    return x_all[:, lc:]


_LAYER0 = ("w_ada", "b_ada", "w_in", "mu_prev", "mu_next", "pool_w", "pool_scale", "w0_f", "w2_f",
           "w0_b", "w2_b", "a0_f", "a2_f", "a0_b", "a2_b", "g_up", "k_k", "k_a", "r_k", "lnx_g",
           "lnx_b", "w_out", "ln1_g", "ln1_b", "ln2_g", "ln2_b", "ffn_w1", "ffn_w3", "ffn_w2")
_LAYER1 = _LAYER0[:26] + ("v0", "v2", "router", "exp_w1", "exp_w3", "exp_w2")


def kernel(x, c, ctx, c_ctx, l0_w_ada, l0_b_ada, l0_w_in, l0_mu_prev, l0_mu_next, l0_pool_w, l0_pool_scale, l0_w0_f, l0_w2_f, l0_w0_b, l0_w2_b, l0_a0_f, l0_a2_f, l0_a0_b, l0_a2_b, l0_g_up, l0_k_k, l0_k_a, l0_r_k, l0_lnx_g, l0_lnx_b, l0_w_out, l0_ln1_g, l0_ln1_b, l0_ln2_g, l0_ln2_b, l0_ffn_w1, l0_ffn_w3, l0_ffn_w2, l1_w_ada, l1_b_ada, l1_w_in, l1_mu_prev, l1_mu_next, l1_pool_w, l1_pool_scale, l1_w0_f, l1_w2_f, l1_w0_b, l1_w2_b, l1_a0_f, l1_a2_f, l1_a0_b, l1_a2_b, l1_g_up, l1_k_k, l1_k_a, l1_r_k, l1_lnx_g, l1_lnx_b, l1_w_out, l1_ln1_g, l1_ln1_b, l1_ln2_g, l1_ln2_b, l1_v0, l1_v2, l1_router, l1_exp_w1, l1_exp_w3, l1_exp_w2):
    l0 = dict(zip(_LAYER0, (l0_w_ada, l0_b_ada, l0_w_in, l0_mu_prev, l0_mu_next, l0_pool_w, l0_pool_scale, l0_w0_f, l0_w2_f, l0_w0_b, l0_w2_b, l0_a0_f, l0_a2_f, l0_a0_b, l0_a2_b, l0_g_up, l0_k_k, l0_k_a, l0_r_k, l0_lnx_g, l0_lnx_b, l0_w_out, l0_ln1_g, l0_ln1_b, l0_ln2_g, l0_ln2_b, l0_ffn_w1, l0_ffn_w3, l0_ffn_w2)))
    l1 = dict(zip(_LAYER1, (l1_w_ada, l1_b_ada, l1_w_in, l1_mu_prev, l1_mu_next, l1_pool_w, l1_pool_scale, l1_w0_f, l1_w2_f, l1_w0_b, l1_w2_b, l1_a0_f, l1_a2_f, l1_a0_b, l1_a2_b, l1_g_up, l1_k_k, l1_k_a, l1_r_k, l1_lnx_g, l1_lnx_b, l1_w_out, l1_ln1_g, l1_ln1_b, l1_ln2_g, l1_ln2_b, l1_v0, l1_v2, l1_router, l1_exp_w1, l1_exp_w3, l1_exp_w2)))
    return _forward(x, c, ctx, c_ctx, [l0, l1], GRID_W)
```

```python
import functools
import math

import numpy as np
import jax
import jax.numpy as jnp
from jax import lax
from jax.experimental import pallas as pl
from jax.experimental.pallas import tpu as pltpu

F32 = jnp.float32
BF16 = jnp.bfloat16

GRID_W = 64
POOL_WINDOWS = (2, 4, 8, 16)
HEAD = 64
LANES = 128
SUBLANES = 8
HALO = 16
N_ADA = 6
LN_EPS = 1e-5
GN_EPS = 64e-5
NORM_EPS = 1e-12
CHUNK = 64
SCAN_PAIRS = 8
SCAN_SUB = 4
INV_PASSES = 1
VMEM_LIMIT = 56 * 1024 * 1024
COND_ROWS = 16
PACK_GROUP = 1024


def _cparams(sem):
    return pltpu.CompilerParams(dimension_semantics=sem, vmem_limit_bytes=VMEM_LIMIT)


def _pick(n, target, mult):
    best = None
    for d in range(mult, min(n, target) + 1, mult):
        if n % d == 0:
            best = d
    return best if best is not None else n


def _dot(a, b):
    return jnp.dot(a, b, preferred_element_type=F32)


def _dot_nt(a, b):
    return lax.dot_general(a, b, (((1,), (1,)), ((), ())), preferred_element_type=F32)


def _split2(x):
    hi = x.astype(BF16)
    lo = (x - hi.astype(F32)).astype(BF16)
    return hi, lo


def _mm(a, b, passes=1):
    if passes == 1:
        return _dot(a.astype(BF16), b.astype(BF16))
    a_hi, a_lo = _split2(a)
    b_hi, b_lo = _split2(b)
    return _dot(a_hi, b_hi) + _dot(a_lo, b_hi) + _dot(a_hi, b_lo)


def _sigmoid(x):
    return 1.0 / (1.0 + jnp.exp(-x))


def _pack_pairs(x, group):
    half = group // 2
    words = []
    for g in range(x.shape[1] // group):
        lo = lax.bitcast_convert_type(x[:, g * group:g * group + half].astype(BF16).astype(F32), jnp.uint32)
        hi = lax.bitcast_convert_type(x[:, g * group + half:(g + 1) * group].astype(BF16).astype(F32), jnp.uint32)
        words.append((lo >> 16) | hi)
    return words[0] if len(words) == 1 else jnp.concatenate(words, axis=1)


def _unpack_pairs(p, group):
    half = group // 2
    cols = []
    for g in range(p.shape[1] // half):
        word = p[:, g * half:(g + 1) * half]
        cols.append(lax.bitcast_convert_type(word << 16, F32))
        cols.append(lax.bitcast_convert_type(word & jnp.uint32(0xFFFF0000), F32))
    return jnp.concatenate(cols, axis=1)


def _layer_norm_rows(x):
    mu = jnp.mean(x, axis=-1, keepdims=True)
    xc = x - mu
    var = jnp.mean(xc * xc, axis=-1, keepdims=True)
    return xc * lax.rsqrt(var + LN_EPS)


def _mod_kernel(c_ref, w_ref, b_ref, o_ref):
    c = c_ref[...]
    s = c * _sigmoid(c)
    o_ref[...] = _mm(s, w_ref[...], passes=3) + b_ref[...]


def _modulation(cond, w_ada, b_ada):
    d, n = w_ada.shape
    tn = _pick(n, 512, LANES)
    return pl.pallas_call(
        _mod_kernel,
        grid=(n // tn,),
        in_specs=[pl.BlockSpec((COND_ROWS, d), lambda j: (0, 0)),
                  pl.BlockSpec((d, tn), lambda j: (0, j)),
                  pl.BlockSpec((1, tn), lambda j: (0, j))],
        out_specs=pl.BlockSpec((COND_ROWS, tn), lambda j: (0, j)),
        out_shape=jax.ShapeDtypeStruct((COND_ROWS, n), F32),
        compiler_params=_cparams(("parallel",)),
        name="modulation",
    )(cond, w_ada, b_ada.reshape(1, n))


def _mod_row(mod_ref, row, chunk, d):
    return mod_ref[pl.ds(row, 1), chunk * d:(chunk + 1) * d]


def _ln_mod_kernel(x_ref, mod_ref, h_ref, *, nctx_blk, ctx_row, k_shift, k_scale):
    b, j = pl.program_id(0), pl.program_id(1)
    d = x_ref.shape[-1]
    row = jnp.where(j < nctx_blk, ctx_row, b)
    shift = _mod_row(mod_ref, row, k_shift, d)
    scale = _mod_row(mod_ref, row, k_scale, d)
    h_ref[...] = (_layer_norm_rows(x_ref[...]) * (1.0 + scale) + shift).astype(h_ref.dtype)


def _ln_mod(x_all, mod, *, lc, tr, ctx_row, k_shift, k_scale):
    bsz, t, d = x_all.shape
    kern = functools.partial(_ln_mod_kernel, nctx_blk=lc // tr, ctx_row=ctx_row,
                             k_shift=k_shift, k_scale=k_scale)
    return pl.pallas_call(
        kern,
        grid=(bsz, t // tr),
        in_specs=[pl.BlockSpec((None, tr, d), lambda b, j: (b, j, 0)),
                  pl.BlockSpec(mod.shape, lambda b, j: (0, 0))],
        out_specs=pl.BlockSpec((None, tr, d), lambda b, j: (b, j, 0)),
        out_shape=jax.ShapeDtypeStruct((bsz, t, d), BF16),
        compiler_params=_cparams(("parallel", "parallel")),
        name="ln_modulate",
    )(x_all, mod)


def _mm_kernel(a_ref, w_ref, o_ref, *scratch, nk):
    if nk == 1:
        o_ref[...] = _dot(a_ref[...], w_ref[...]).astype(o_ref.dtype)
        return
    acc_ref, = scratch
    k = pl.program_id(2)

    @pl.when(k == 0)
    def _():
        acc_ref[...] = jnp.zeros_like(acc_ref)

    acc_ref[...] += _dot(a_ref[...], w_ref[...])

    @pl.when(k == nk - 1)
    def _():
        o_ref[...] = acc_ref[...].astype(o_ref.dtype)


def _matmul(a, w, *, out_dtype, tm, tn, tk, name):
    m, kdim = a.shape
    n = w.shape[1]
    tm, tn, tk = _pick(m, tm, SUBLANES), _pick(n, tn, LANES), _pick(kdim, tk, LANES)
    nk = kdim // tk
    scratch = [pltpu.VMEM((tm, tn), F32)] if nk > 1 else []
    return pl.pallas_call(
        functools.partial(_mm_kernel, nk=nk),
        grid=(m // tm, n // tn, nk),
        in_specs=[pl.BlockSpec((tm, tk), lambda i, j, k: (i, k)),
                  pl.BlockSpec((tk, tn), lambda i, j, k: (k, j))],
        out_specs=pl.BlockSpec((tm, tn), lambda i, j, k: (i, j)),
        out_shape=jax.ShapeDtypeStruct((m, n), out_dtype),
        scratch_shapes=scratch,
        compiler_params=_cparams(("parallel", "parallel", "arbitrary")),
        name=name,
    )(a, w)


def _glu_kernel(a_ref, w1_ref, w3_ref, o_ref):
    a = a_ref[...]
    p1 = _dot(a, w1_ref[...])
    p3 = _dot(a, w3_ref[...])
    o_ref[...] = (p1 * _sigmoid(p1) * p3).astype(o_ref.dtype)


def _glu(a, w1, w3, *, tm, tn, name):
    m, kdim = a.shape
    f = w1.shape[1]
    tm, tn = _pick(m, tm, SUBLANES), _pick(f, tn, LANES)
    w_spec = pl.BlockSpec((kdim, tn), lambda i, j: (0, j))
    return pl.pallas_call(
        _glu_kernel,
        grid=(m // tm, f // tn),
        in_specs=[pl.BlockSpec((tm, kdim), lambda i, j: (i, 0)), w_spec, w_spec],
        out_specs=pl.BlockSpec((tm, tn), lambda i, j: (i, j)),
        out_shape=jax.ShapeDtypeStruct((m, f), BF16),
        compiler_params=_cparams(("parallel", "parallel")),
        name=name,
    )(a, w1, w3)


def _pool_consts(lc, tl, grid_w, tb):
    rows = tl // grid_w
    cb, c1, inv = [], [], []
    for win in POOL_WINDOWS:
        half = win // 2
        t = np.arange(tb)
        same_row = (t[:, None] // grid_w) == (t[None, :] // grid_w)
        dc = (t[None, :] % grid_w) - (t[:, None] % grid_w)
        cb.append((same_row & (dc >= -half) & (dc < half)).astype(np.float32))
        tc = np.arange(lc)
        d1 = tc[None, :] - tc[:, None]
        c1.append(((d1 >= -half) & (d1 < half)).astype(np.float32))
        cnt1 = np.minimum(tc + half, lc) - np.maximum(tc - half, 0)
        g = np.arange(grid_w)
        cntc = np.minimum(g + half, grid_w) - np.maximum(g - half, 0)
        r = np.arange(rows)
        cntr = np.minimum(r + half, rows) - np.maximum(r - half, 0)
        cnt2 = (cntr[:, None] * cntc[None, :]).reshape(-1)
        iv = 1.0 / np.concatenate([cnt1, cnt2]).astype(np.float64)
        inv.append(np.broadcast_to(iv[:, None], (lc + tl, LANES)).astype(np.float32))
    return (jnp.asarray(np.stack(cb), BF16), jnp.asarray(np.stack(c1), BF16),
            jnp.asarray(np.stack(inv), F32))


def _pool_kernel(u_ref, cb_ref, c1_ref, inv_ref, pw_ref, ps_ref, o_ref, s1_ref, acc_ref,
                 *, lc, tb, grid_w, half):
    t = u_ref.shape[0]
    tl = t - lc
    acc_ref[0:lc, :] = _dot(c1_ref[...], u_ref[0:lc, :])
    cb = cb_ref[...]
    for blk in range(tl // tb):
        lo, hi = lc + blk * tb, lc + (blk + 1) * tb
        s1_ref[lo:hi, :] = _dot(cb, u_ref[lo:hi, :])
    acc_ref[lc:t, :] = s1_ref[lc:t, :]
    for dr in range(-half, half):
        if dr == 0:
            continue
        sh = abs(dr) * grid_w
        if sh >= tl:
            continue
        if dr > 0:
            acc_ref[lc:t - sh, :] += s1_ref[lc + sh:t, :]
        else:
            acc_ref[lc + sh:t, :] += s1_ref[lc:t - sh, :]
    n_rep = u_ref.shape[1] // LANES
    diff = acc_ref[...] * jnp.concatenate([inv_ref[...]] * n_rep, axis=1) - u_ref[...].astype(F32)
    o_ref[...] = (_dot(diff.astype(BF16), pw_ref[...]) * ps_ref[...]).astype(o_ref.dtype)


def _pool_group(u_all, consts, pool_w, pool_scale, idx, *, lc, grid_w, tb, pg):
    bsz, t, _ = u_all.shape
    cb, c1, inv = consts
    half = POOL_WINDOWS[idx] // 2
    kern = functools.partial(_pool_kernel, lc=lc, tb=tb, grid_w=grid_w, half=half)
    return pl.pallas_call(
        kern,
        grid=(bsz,),
        in_specs=[pl.BlockSpec((None, t, pg), lambda b: (b, 0, idx)),
                  pl.BlockSpec((None, tb, tb), lambda b: (idx, 0, 0)),
                  pl.BlockSpec((None, lc, lc), lambda b: (idx, 0, 0)),
                  pl.BlockSpec((None, t, LANES), lambda b: (idx, 0, 0)),
                  pl.BlockSpec((None, pg, pg), lambda b: (idx, 0, 0)),
                  pl.BlockSpec((1, pg), lambda b: (0, idx))],
        out_specs=pl.BlockSpec((None, t, pg), lambda b: (b, 0, 0)),
        out_shape=jax.ShapeDtypeStruct((bsz, t, pg), BF16),
        scratch_shapes=[pltpu.VMEM((t, pg), F32), pltpu.VMEM((t, pg), F32)],
        compiler_params=_cparams(("parallel",)),
        name="pool_mixer_%d" % idx,
    )(u_all, cb, c1, inv, pool_w, pool_scale)


def _token_shift(f_ref, prev_ref, next_ref, mu_p, mu_n, seg_first, seg_last, shift_ref=None):
    f = f_ref[...].astype(F32)
    tr = f.shape[0]
    row = lax.broadcasted_iota(jnp.int32, f.shape, 0)
    prev_row = jnp.where(seg_first, 0.0, prev_ref[...].astype(F32)[HALO - 1:HALO, :])
    next_row = jnp.where(seg_last, 0.0, next_ref[...].astype(F32)[0:1, :])
    if shift_ref is None:
        prev = jnp.where(row == 0, prev_row, pltpu.roll(f, 1, 0))
        nxt = jnp.where(row == tr - 1, next_row, pltpu.roll(f, tr - 1, 0))
    else:
        moved = _dot(shift_ref[...], f_ref[...])
        prev = jnp.where(row == 0, prev_row, moved[:tr])
        nxt = jnp.where(row == tr - 1, next_row, moved[tr:])
    return f + mu_p * (prev - f) + mu_n * (nxt - f)


def _shift_matrix(tr):
    t = np.arange(tr)
    prev = (t[:, None] - 1 == t[None, :]).astype(np.float32)
    nxt = (t[:, None] + 1 == t[None, :]).astype(np.float32)
    return jnp.asarray(np.concatenate([prev, nxt], axis=0), BF16)


def _segment_flags(j, nctx_blk, n_blk):
    seg_first = jnp.logical_or(j == 0, j == nctx_blk)
    seg_last = jnp.logical_or(j == nctx_blk - 1, j == n_blk - 1)
    return seg_first, seg_last


def _shift_specs(tr, tc, col_blk, t):
    per_blk = tr // HALO
    last = t // HALO - 1
    cur = pl.BlockSpec((None, tr, tc), lambda b, j, c: (b, j, col_blk(c)))
    prv = pl.BlockSpec((None, HALO, tc),
                       lambda b, j, c: (b, jnp.maximum(j * per_blk - 1, 0), col_blk(c)))
    nxt = pl.BlockSpec((None, HALO, tc),
                       lambda b, j, c: (b, jnp.minimum((j + 1) * per_blk, last), col_blk(c)))
    return [cur, prv, nxt]


def _lora_kernel(u_ref, up_ref, un_ref, mu_ref, o_ref, *, nctx_blk, n_blk, lr, gl):
    j = pl.program_id(1)
    seg_first, seg_last = _segment_flags(j, nctx_blk, n_blk)
    z = _token_shift(u_ref, up_ref, un_ref, mu_ref[0:1, :], mu_ref[1:2, :], seg_first, seg_last)
    col = lax.broadcasted_iota(jnp.int32, z.shape, 1)
    act = jnp.where(col < 2 * lr, jnp.tanh(z),
                    jnp.where(jnp.logical_and(col >= 4 * lr, col < 4 * lr + gl), _sigmoid(z), z))
    o_ref[...] = act.astype(o_ref.dtype)


def _lora_act(u_all, mu_lora, *, lc, tr, col0, wl, lr, gl):
    bsz, t, _ = u_all.shape
    specs = _shift_specs(tr, wl, lambda c: col0 // wl, t)
    kern = functools.partial(_lora_kernel, nctx_blk=lc // tr, n_blk=t // tr, lr=lr, gl=gl)
    return pl.pallas_call(
        kern,
        grid=(bsz, t // tr, 1),
        in_specs=specs + [pl.BlockSpec((SUBLANES, wl), lambda b, j, c: (0, 0))],
        out_specs=pl.BlockSpec((None, tr, wl), lambda b, j, c: (b, j, 0)),
        out_shape=jax.ShapeDtypeStruct((bsz, t, wl), BF16),
        compiler_params=_cparams(("parallel", "parallel", "arbitrary")),
        name="lora_act",
    )(u_all, u_all, u_all, mu_lora)


def _head_ones():
    r = lax.broadcasted_iota(jnp.int32, (LANES, LANES), 0)
    c = lax.broadcasted_iota(jnp.int32, (LANES, LANES), 1)
    return jnp.where((r ^ c) < HEAD, 1.0, 0.0).astype(BF16)


def _head_sum(x, ones):
    outs = []
    for s in range(x.shape[1] // LANES):
        hi, lo = _split2(x[:, s * LANES:(s + 1) * LANES])
        outs.append(_dot(hi, ones) + _dot(lo, ones))
    return outs[0] if len(outs) == 1 else jnp.concatenate(outs, axis=1)


(_V_MPR, _V_MNR, _V_MPK, _V_MNK, _V_MPV, _V_MNV, _V_W0F, _V_W0B, _V_A0F, _V_A0B,
 _V_V0, _V_KK, _V_KA, _V_RK, _V_LG, _V_LB) = range(16)
_DECAY_SCALE = math.exp(-0.5)


def _terms_kernel(*refs, nctx_blk, n_blk, lr, gl, has_vres):
    (ur, urp, urn, uk, ukp, ukn, uv, uvp, uvn, act_ref, vec_ref,
     w2f_ref, w2b_ref, a2f_ref, a2b_ref, gup_ref, shift_ref) = refs[:17]
    pos = 17
    if has_vres:
        v2_ref, vfirst_ref = refs[pos:pos + 2]
        pos += 2
    (r_o, v_o, kk_o, lwf_o, lwb_o, kf_o, kb_o, af_o, ab_o, gate_o, bonus_o) = refs[pos:]

    j = pl.program_id(1)
    seg_first, seg_last = _segment_flags(j, nctx_blk, n_blk)
    vec = lambda i: vec_ref[i:i + 1, :]
    r = _token_shift(ur, urp, urn, vec(_V_MPR), vec(_V_MNR), seg_first, seg_last, shift_ref)
    k = _token_shift(uk, ukp, ukn, vec(_V_MPK), vec(_V_MNK), seg_first, seg_last, shift_ref)
    v = _token_shift(uv, uvp, uvn, vec(_V_MPV), vec(_V_MNV), seg_first, seg_last, shift_ref)

    act = act_ref[...]
    a_wf, a_wb = act[:, 0:lr], act[:, lr:2 * lr]
    a_af, a_ab = act[:, 2 * lr:3 * lr], act[:, 3 * lr:4 * lr]
    a_g = act[:, 4 * lr:4 * lr + gl]
    if has_vres:
        a_v = act[:, 4 * lr + gl:5 * lr + gl]
        v = v + (vfirst_ref[...].astype(F32) - v) * _sigmoid(vec(_V_V0) + _dot(a_v, v2_ref[...]))

    lwf_o[...] = -_DECAY_SCALE * _sigmoid(vec(_V_W0F) + _dot(a_wf, w2f_ref[...]))
    lwb_o[...] = -_DECAY_SCALE * _sigmoid(vec(_V_W0B) + _dot(a_wb, w2b_ref[...]))
    a_f = _sigmoid(vec(_V_A0F) + _dot(a_af, a2f_ref[...]))
    a_b = _sigmoid(vec(_V_A0B) + _dot(a_ab, a2b_ref[...]))

    ones = _head_ones()
    kkr = k * vec(_V_KK)
    kk_o[...] = (kkr * lax.rsqrt(jnp.maximum(_head_sum(kkr * kkr, ones), NORM_EPS * NORM_EPS))
                 ).astype(kk_o.dtype)
    k_f = k * (1.0 + (a_f - 1.0) * vec(_V_KA))
    k_b = k * (1.0 + (a_b - 1.0) * vec(_V_KA))
    bonus_o[...] = (_head_sum(r * (k_f + k_b) * vec(_V_RK), ones) * v).astype(bonus_o.dtype)
    gate_o[...] = _dot(a_g, gup_ref[...]).astype(gate_o.dtype)
    r_o[...] = r.astype(r_o.dtype)
    v_o[...] = v.astype(v_o.dtype)
    kf_o[...] = k_f.astype(kf_o.dtype)
    kb_o[...] = k_b.astype(kb_o.dtype)
    af_o[...] = a_f.astype(af_o.dtype)
    ab_o[...] = a_b.astype(ab_o.dtype)


def _rwkv_terms(u_all, act, vec, lw, v_first, *, lc, tr, tc, pool_w, rw, lr, gl):
    bsz, t, _ = u_all.shape
    has_vres = v_first is not None
    nb = rw // tc
    specs = []
    for part in range(3):
        base = (pool_w + part * rw) // tc
        specs += _shift_specs(tr, tc, (lambda c, base=base: base + c), t)
    specs.append(pl.BlockSpec((None, tr, act.shape[-1]), lambda b, j, c: (b, j, 0)))
    specs.append(pl.BlockSpec((vec.shape[0], tc), lambda b, j, c: (0, c)))
    wspec = lambda rows: pl.BlockSpec((rows, tc), lambda b, j, c: (0, c))
    specs += [wspec(lr), wspec(lr), wspec(lr), wspec(lr), wspec(gl)]
    specs.append(pl.BlockSpec((2 * tr, tr), lambda b, j, c: (0, 0)))
    args = [u_all] * 9 + [act, vec, lw["w2_f"], lw["w2_b"], lw["a2_f"], lw["a2_b"], lw["g_up"],
                          _shift_matrix(tr)]
    if has_vres:
        specs += [wspec(lr), pl.BlockSpec((None, tr, tc), lambda b, j, c: (b, j, c))]
        args += [lw["v2"], v_first]
    out_spec = pl.BlockSpec((None, tr, tc), lambda b, j, c: (b, j, c))
    kern = functools.partial(_terms_kernel, nctx_blk=lc // tr, n_blk=t // tr, lr=lr, gl=gl,
                             has_vres=has_vres)
    names = ("r", "v", "kk", "lw_f", "lw_b", "k_f", "k_b", "a_f", "a_b", "gate", "bonus")
    dtypes = [F32 if n.startswith("lw") else BF16 for n in names]
    outs = pl.pallas_call(
        kern,
        grid=(bsz, t // tr, nb),
        in_specs=specs,
        out_specs=[out_spec] * len(names),
        out_shape=[jax.ShapeDtypeStruct((bsz, t, rw), dt) for dt in dtypes],
        compiler_params=_cparams(("parallel", "parallel", "parallel")),
        name="rwkv_terms",
    )(*args)
    return dict(zip(names, outs))


def _cumsum_rows(x, rev):
    n = x.shape[0]
    row = lax.broadcasted_iota(jnp.int32, x.shape, 0)
    s = 1
    while s < n:
        if rev:
            x = x + jnp.where(row < n - s, pltpu.roll(x, n - s, 0), 0.0)
        else:
            x = x + jnp.where(row >= s, pltpu.roll(x, s, 0), 0.0)
        s *= 2
    return x


def _wkv_chunks(chains):
    n = chains[0][0].shape[0]
    hp = 2 * n
    lane = lax.broadcasted_iota(jnp.int32, (n, LANES), 1)
    first_head = lane < HEAD
    ri = lax.broadcasted_iota(jnp.int32, (hp, hp), 0)
    ci = lax.broadcasted_iota(jnp.int32, (hp, hp), 1)
    blk = ri ^ ci
    eye = ri == ci
    tdiff = (ci & (n - 1)) - (ri & (n - 1))
    dist = {False: jnp.where(blk < n, tdiff, hp), True: jnp.where(blk < n, -tdiff, hp)}

    def stack(x):
        return jnp.concatenate([jnp.where(first_head, x, 0.0), jnp.where(first_head, 0.0, x)], axis=0)

    pre = []
    for r, v, kk, lw, k, a, state, rev in chains:
        c = _cumsum_rows(lw, rev)
        ctot = c[0:1, :] if rev else c[n - 1:n, :]
        e_pos, e_neg = jnp.exp(c), jnp.exp(-c)
        e_prev, e_rem = jnp.exp(c - lw), jnp.exp(ctot - c)
        kka = kk * a
        bt, kt = kka * e_neg, k * e_neg
        pre.append(dict(s_at=stack(-kk * e_prev), s_rt=stack(r * e_pos), s_v=stack(v), rt=r * e_pos,
                        rhs1=jnp.concatenate([bt, bt, kt, kt], axis=0).astype(BF16),
                        lhs_t=jnp.concatenate([stack(kka * e_rem), stack(k * e_rem)], axis=0),
                        wtot=jnp.exp(ctot), strict=dist[rev] < 0, incl=dist[rev] <= 0))

    s1 = [_dot_nt(jnp.concatenate([p["s_at"], p["s_rt"]], axis=0).astype(BF16), p["rhs1"]) for p in pre]
    nmat = [jnp.where(p["strict"], s[:hp, :hp], 0.0) for p, s in zip(pre, s1)]
    m_ak = [jnp.where(p["strict"], s[:hp, hp:], 0.0) for p, s in zip(pre, s1)]
    m_rbk = [jnp.concatenate([jnp.where(p["incl"], s[hp:, :hp], 0.0), jnp.where(p["incl"], s[hp:, hp:], 0.0)],
                             axis=1).astype(BF16) for p, s in zip(pre, s1)]
    z = [_mm(m, p["s_v"]) for m, p in zip(m_ak, pre)]

    n8 = [jnp.where(blk < SUBLANES, m, 0.0) for m in nmat]
    tmat = [jnp.where(eye, 1.0, m) for m in n8]
    n2 = [_mm(m, m, INV_PASSES) for m in n8]
    st = [_mm(jnp.concatenate([t, m], axis=0), m, INV_PASSES) for t, m in zip(tmat, n2)]
    tmat = [t + s[:hp] for t, s in zip(tmat, st)]
    tmat = [t + _mm(t, s[hp:], INV_PASSES) for t, s in zip(tmat, st)]
    size = SUBLANES
    while size < n:
        sel = jnp.logical_and(blk >= size, blk < 2 * size)
        pieces = [[(g * 2 * size + (0 if chain[7] else size), g * 2 * size + (size if chain[7] else 2 * size))
                   for g in range(hp // (2 * size))] for chain in chains]

        def take(mat, rows):
            return jnp.concatenate([mat[a:b] for a, b in rows], axis=0)

        def put(mat, rows, part, keep):
            out, at, prev = [], 0, 0
            for a, b in rows:
                if a > prev:
                    out.append(mat[prev:a] if keep else jnp.zeros((a - prev, mat.shape[1]), F32))
                out.append(part[at:at + b - a])
                at, prev = at + b - a, b
            if prev < mat.shape[0]:
                out.append(mat[prev:] if keep else jnp.zeros((mat.shape[0] - prev, mat.shape[1]), F32))
            return jnp.concatenate(out, axis=0)

        x = [_mm(take(jnp.where(sel, m, 0.0), rows), t, INV_PASSES) for m, t, rows in zip(nmat, tmat, pieces)]
        x = [put(t, rows, xx, keep=False) for t, rows, xx in zip(tmat, pieces, x)]
        th = [take(t, rows) for t, rows in zip(tmat, pieces)]
        th = [h + _mm(h, xx, INV_PASSES) for h, xx in zip(th, x)]
        tmat = [put(t, rows, h, keep=True) for t, rows, h in zip(tmat, pieces, th)]
        size *= 2

    gu = [_mm(t, jnp.concatenate([p["s_at"], zz], axis=1)) for t, p, zz in zip(tmat, pre, z)]
    zeros = jnp.zeros((hp, LANES), F32)
    rhs4 = [jnp.concatenate([g, jnp.concatenate([zeros, p["s_v"]], axis=1)], axis=0).astype(BF16)
            for g, p in zip(gu, pre)]
    tb = [_dot(jnp.concatenate([m, p["lhs_t"].T.astype(BF16)], axis=0), rr)
          for m, p, rr in zip(m_rbk, pre, rhs4)]
    top = [x_[:hp] for x_ in tb]
    bot = [x_[hp:] for x_ in tb]

    outs = []
    for p, tp, bt_, chain in zip(pre, top, bot, chains):
        q = p["rt"] + tp[:n, :LANES] + tp[n:, :LANES]
        amat = jnp.where(eye, jnp.broadcast_to(p["wtot"], (LANES, LANES)), 0.0) + bt_[:, :LANES]
        qa = jnp.concatenate([q, amat], axis=0).astype(BF16)
        s_hi, s_lo = _split2(chain[6])
        outs.append(_dot(qa, s_hi) + _dot(qa, s_lo))
    return [(o[:n] + tp[:n, LANES:] + tp[n:, LANES:], o[n:] + bt_[:, LANES:])
            for o, tp, bt_ in zip(outs, top, bot)]


def _scan_kernel(*refs, pairs, chunk):
    ins, (yf_ref, yb_ref, state_ref) = refs[:12], refs[12:]
    s = pl.program_id(1)
    n_sub = yf_ref.shape[0] // chunk

    @pl.when(s == 0)
    def _():
        state_ref[...] = jnp.zeros_like(state_ref)

    chains = [(d, p, slice(p * LANES, (p + 1) * LANES)) for d in range(2) for p in range(pairs)]

    def sub_step(i, carry):
        rows = [pl.ds(pl.multiple_of((n_sub - 1 - i if d else i) * chunk, chunk), chunk) for d in range(2)]
        loaded = [tuple(ref[rows[d], cols].astype(F32) for ref in ins[6 * d:6 * d + 6])
                  + (state_ref[d, p], bool(d)) for d, p, cols in chains]
        for (d, p, cols), (y, new_state) in zip(chains, _wkv_chunks(loaded)):
            y_ref = yb_ref if d else yf_ref
            y_ref[rows[d], cols] = y.astype(y_ref.dtype)
            state_ref[d, p] = new_state
        return carry

    lax.fori_loop(0, n_sub, sub_step, 0)


def _wkv_scan(tm, *, lc, chunk, pairs):
    bsz, t, rw = tm["r"].shape
    width = pairs * LANES
    ngrp = rw // width
    n_sub = math.gcd(math.gcd(lc // chunk, (t - lc) // chunk), SCAN_SUB)
    step_rows = n_sub * chunk
    nctx, ntot = lc // step_rows, t // step_rows

    def fwd(g, s):
        return (g // ngrp, s, g % ngrp)

    def bwd(g, s):
        return (g // ngrp, jnp.where(s < nctx, nctx - 1 - s, ntot - 1 - (s - nctx)), g % ngrp)

    blk = (None, step_rows, width)
    names_f = ("r", "v", "kk", "lw_f", "k_f", "a_f")
    names_b = ("r", "v", "kk", "lw_b", "k_b", "a_b")
    in_specs = [pl.BlockSpec(blk, fwd)] * 6 + [pl.BlockSpec(blk, bwd)] * 6
    args = [tm[n] for n in names_f] + [tm[n] for n in names_b]
    return pl.pallas_call(
        functools.partial(_scan_kernel, pairs=pairs, chunk=chunk),
        grid=(bsz * ngrp, ntot),
        in_specs=in_specs,
        out_specs=[pl.BlockSpec(blk, fwd), pl.BlockSpec(blk, bwd)],
        out_shape=[jax.ShapeDtypeStruct((bsz, t, rw), BF16)] * 2,
        scratch_shapes=[pltpu.VMEM((2, pairs, LANES, LANES), F32)],
        compiler_params=_cparams(("parallel", "arbitrary")),
        name="wkv7_scan",
    )(*args)


def _rwkv_out_kernel(yf_ref, yb_ref, bonus_ref, gate_ref, vec_ref, o_ref):
    y = yf_ref[...].astype(F32) + yb_ref[...].astype(F32)
    ones = _head_ones()
    inv_n = 1.0 / HEAD
    mu = _head_sum(y, ones) * inv_n
    yc = y - mu
    var = _head_sum(yc * yc, ones) * inv_n
    yn = yc * lax.rsqrt(var + GN_EPS)
    yn = yn * vec_ref[_V_LG:_V_LG + 1, :] + vec_ref[_V_LB:_V_LB + 1, :]
    o_ref[...] = ((yn + bonus_ref[...].astype(F32)) * gate_ref[...].astype(F32)).astype(o_ref.dtype)


def _rwkv_out(y_f, y_b, bonus, gate, vec, *, tr, tc):
    bsz, t, rw = y_f.shape
    spec = pl.BlockSpec((None, tr, tc), lambda b, j, c: (b, j, c))
    return pl.pallas_call(
        _rwkv_out_kernel,
        grid=(bsz, t // tr, rw // tc),
        in_specs=[spec] * 4 + [pl.BlockSpec((vec.shape[0], tc), lambda b, j, c: (0, c))],
        out_specs=spec,
        out_shape=jax.ShapeDtypeStruct((bsz, t, rw), BF16),
        compiler_params=_cparams(("parallel", "parallel", "parallel")),
        name="rwkv_out",
    )(y_f, y_b, bonus, gate, vec)


_R_E1, _R_E2, _R_G1, _R_G2 = range(4)

def _res_kernel(*refs, alpha, row_off_blk, nctx_blk, ctx_row, k_gate, k_shift, k_scale,
                with_mod, n_exp):
    x_ref, y_ref, mod_ref, gb_ref = refs[:4]
    pos = 4
    if n_exp:
        router_ref = refs[pos]
        pos += 1
    outs = refs[pos:]
    b, j = pl.program_id(0), pl.program_id(1)
    d = x_ref.shape[-1]
    row = jnp.where(j + row_off_blk < nctx_blk, ctx_row, b)
    gate = _mod_row(mod_ref, row, k_gate, d)
    xn = _layer_norm_rows(alpha * x_ref[...] + gate * y_ref[...]) * gb_ref[0:1, :] + gb_ref[1:2, :]
    outs[0][...] = xn
    if not with_mod:
        return
    h = _layer_norm_rows(xn) * (1.0 + _mod_row(mod_ref, row, k_scale, d)) + _mod_row(mod_ref, row, k_shift, d)
    if outs[1].dtype == jnp.uint32:
        outs[1][...] = _pack_pairs(h, min(PACK_GROUP, d))
    else:
        outs[1][...] = h.astype(outs[1].dtype)
    if not n_exp:
        return
    logits = _mm(h, router_ref[...], passes=3)
    lane = lax.broadcasted_iota(jnp.int32, logits.shape, 1).astype(F32)
    neg = -jnp.inf
    logits = jnp.where(lane < n_exp, logits, neg)
    m1 = jnp.max(logits, axis=-1, keepdims=True)
    i1 = jnp.min(jnp.where(logits == m1, lane, float(LANES)), axis=-1, keepdims=True)
    rest = jnp.where(lane == i1, neg, logits)
    m2 = jnp.max(rest, axis=-1, keepdims=True)
    i2 = jnp.min(jnp.where(rest == m2, lane, float(LANES)), axis=-1, keepdims=True)
    e2 = jnp.exp(m2 - m1)
    g1 = 1.0 / (1.0 + e2)
    g2 = e2 / (1.0 + e2)
    outs[2][...] = jnp.where(lane == _R_E1, i1, jnp.where(lane == _R_E2, i2,
                             jnp.where(lane == _R_G1, g1, jnp.where(lane == _R_G2, g2, 0.0))))


def _residual_ln(x, y, mod, gain, bias, *, alpha, tr, lc, ctx_row, k_gate, x_off=0, y_off=0,
                 rows=None, mod2=None, router=None, pack_h=False):
    bsz, _, d = x.shape
    rows = x.shape[1] if rows is None else rows
    n_exp = 0 if router is None else router.shape[1]
    gb = jnp.stack([gain, bias])
    in_specs = [pl.BlockSpec((None, tr, d), lambda b, j: (b, j + x_off // tr, 0)),
                pl.BlockSpec((None, tr, d), lambda b, j: (b, j + y_off // tr, 0)),
                pl.BlockSpec(mod.shape, lambda b, j: (0, 0)),
                pl.BlockSpec((2, d), lambda b, j: (0, 0))]
    args = [x, y, mod, gb]
    out_specs = [pl.BlockSpec((None, tr, d), lambda b, j: (b, j, 0))]
    out_shape = [jax.ShapeDtypeStruct((bsz, rows, d), F32)]
    if mod2 is not None:
        h_cols, h_dtype = (d // 2, jnp.uint32) if pack_h else (d, BF16)
        out_specs.append(pl.BlockSpec((None, tr, h_cols), lambda b, j: (b, j, 0)))
        out_shape.append(jax.ShapeDtypeStruct((bsz, rows, h_cols), h_dtype))
    if router is not None:
        rpad = jnp.zeros((d, LANES), F32).at[:, :n_exp].set(router)
        in_specs.append(pl.BlockSpec((d, LANES), lambda b, j: (0, 0)))
        args.append(rpad)
        out_specs.append(pl.BlockSpec((None, tr, LANES), lambda b, j: (b, j, 0)))
        out_shape.append(jax.ShapeDtypeStruct((bsz, rows, LANES), F32))
    k_shift, k_scale = mod2 if mod2 is not None else (0, 0)
    kern = functools.partial(_res_kernel, alpha=alpha, row_off_blk=x_off // tr, nctx_blk=lc // tr,
                             ctx_row=ctx_row, k_gate=k_gate, k_shift=k_shift, k_scale=k_scale,
                             with_mod=mod2 is not None, n_exp=n_exp)
    return pl.pallas_call(
        kern,
        grid=(bsz, rows // tr),
        in_specs=in_specs,
        out_specs=out_specs,
        out_shape=out_shape,
        compiler_params=_cparams(("parallel", "parallel")),
        name="residual_ln",
    )(*args)


def _route_plan(route, n_exp, tile):
    n = route.shape[0]
    e_flat = jnp.concatenate([route[:, _R_E1], route[:, _R_E2]]).astype(jnp.int32)
    experts = jnp.arange(n_exp, dtype=jnp.int32)
    onehot = (e_flat[:, None] == experts[None, :]).astype(jnp.int32)
    csum = jnp.cumsum(onehot, axis=0)
    rank = jnp.sum((csum - onehot) * onehot, axis=1)
    tiles_e = (csum[-1] + tile - 1) // tile
    tile_end = jnp.cumsum(tiles_e)
    tile_start = tile_end - tiles_e
    pos = jnp.sum(onehot * tile_start[None, :], axis=1) * tile + rank
    n_tiles = (2 * n) // tile + n_exp
    tidx = jnp.arange(n_tiles, dtype=jnp.int32)
    valid = tidx < tile_end[-1]
    last_e = jnp.max(jnp.where(tiles_e > 0, experts, 0))
    tile_expert = jnp.sum((tidx[:, None] >= tile_end[None, :]).astype(jnp.int32), axis=1)
    tile_expert = jnp.where(valid, tile_expert, last_e)
    token = jnp.arange(2 * n, dtype=jnp.int32) % n
    src_token = jnp.zeros((n_tiles * tile,), jnp.int32).at[pos].set(token)
    return src_token, pos, tile_expert, valid.astype(jnp.int32)


def _gather_kernel(idx_ref, src_ref, out_ref, sem):
    rows = out_ref.shape[0]

    def row_copy(j, src_row):
        return pltpu.make_async_copy(src_ref.at[pl.ds(src_row, 1)], out_ref.at[pl.ds(j, 1)], sem)

    def start(j, carry):
        row_copy(j, idx_ref[0, j]).start()
        return carry

    def wait(j, carry):
        row_copy(j, 0).wait()
        return carry

    lax.fori_loop(0, rows, start, 0, unroll=8)
    lax.fori_loop(0, rows, wait, 0, unroll=8)


def _gather_rows(src, idx, *, rows):
    n_out = idx.shape[0]
    d = src.shape[1]
    rows = _pick(n_out, rows, SUBLANES)
    return pl.pallas_call(
        _gather_kernel,
        grid=(n_out // rows,),
        in_specs=[pl.BlockSpec((None, 1, rows), lambda i: (i, 0, 0), memory_space=pltpu.SMEM),
                  pl.BlockSpec(memory_space=pl.ANY)],
        out_specs=pl.BlockSpec((rows, d), lambda i: (i, 0)),
        out_shape=jax.ShapeDtypeStruct((n_out, d), src.dtype),
        scratch_shapes=[pltpu.SemaphoreType.DMA(())],
        compiler_params=_cparams(("arbitrary",)),
        name="row_gather",
    )(idx.reshape(n_out // rows, 1, rows), src)


def _moe_glu_kernel(te_ref, tv_ref, a_ref, w1_ref, w3_ref, o_ref):
    i = pl.program_id(1)

    @pl.when(tv_ref[i] != 0)
    def _():
        a = _unpack_pairs(a_ref[...], min(PACK_GROUP, 2 * a_ref.shape[1])).astype(BF16)
        p1 = _dot(a, w1_ref[...])
        p3 = _dot(a, w3_ref[...])
        o_ref[...] = (p1 * _sigmoid(p1) * p3).astype(o_ref.dtype)

    @pl.when(tv_ref[i] == 0)
    def _():
        o_ref[...] = jnp.zeros_like(o_ref)


def _moe_down_kernel(te_ref, tv_ref, h_ref, w2_ref, o_ref):
    i = pl.program_id(1)

    @pl.when(tv_ref[i] != 0)
    def _():
        o_ref[...] = _pack_pairs(_dot(h_ref[...], w2_ref[...]), min(PACK_GROUP, 2 * o_ref.shape[1]))

    @pl.when(tv_ref[i] == 0)
    def _():
        o_ref[...] = jnp.zeros_like(o_ref)


def _moe_experts(xs, w1, w3, w2, tile_expert, tile_valid, *, tile, tn_up, tn_down):
    r, d_half = xs.shape
    d = 2 * d_half
    _, _, f = w1.shape
    n_tiles = r // tile
    tn_up, tn_down = _pick(f, tn_up, LANES), _pick(d, tn_down, min(PACK_GROUP, d))
    up_spec = pl.BlockSpec((None, d, tn_up), lambda j, i, te, tv: (te[i], 0, j))
    hid = pl.pallas_call(
        _moe_glu_kernel,
        grid_spec=pltpu.PrefetchScalarGridSpec(
            num_scalar_prefetch=2, grid=(f // tn_up, n_tiles),
            in_specs=[pl.BlockSpec((tile, d_half), lambda j, i, te, tv: (i, 0)), up_spec, up_spec],
            out_specs=pl.BlockSpec((tile, tn_up), lambda j, i, te, tv: (i, j))),
        out_shape=jax.ShapeDtypeStruct((r, f), BF16),
        compiler_params=_cparams(("parallel", "arbitrary")),
        name="moe_glu",
    )(tile_expert, tile_valid, xs, w1, w3)
    return pl.pallas_call(
        _moe_down_kernel,
        grid_spec=pltpu.PrefetchScalarGridSpec(
            num_scalar_prefetch=2, grid=(d // tn_down, n_tiles),
            in_specs=[pl.BlockSpec((tile, f), lambda j, i, te, tv: (i, 0)),
                      pl.BlockSpec((None, f, tn_down), lambda j, i, te, tv: (te[i], 0, j))],
            out_specs=pl.BlockSpec((tile, tn_down // 2), lambda j, i, te, tv: (i, j))),
        out_shape=jax.ShapeDtypeStruct((r, d_half), jnp.uint32),
        compiler_params=_cparams(("parallel", "arbitrary")),
        name="moe_down",
    )(tile_expert, tile_valid, hid, w2)


def _moe_out_kernel(x_ref, y1_ref, y2_ref, route_ref, mod_ref, gb_ref, o_ref, *, alpha, k_gate):
    b = pl.program_id(0)
    d = x_ref.shape[-1]
    route = route_ref[...]
    group = min(PACK_GROUP, d)
    y = (route[:, _R_G1:_R_G1 + 1] * _unpack_pairs(y1_ref[...], group)
         + route[:, _R_G2:_R_G2 + 1] * _unpack_pairs(y2_ref[...], group))
    gate = _mod_row(mod_ref, b, k_gate, d)
    o_ref[...] = _layer_norm_rows(alpha * x_ref[...] + gate * y) * gb_ref[0:1, :] + gb_ref[1:2, :]


def _moe_residual_ln(x, yg, route, mod, gain, bias, *, alpha, tr, k_gate):
    bsz, rows, d = x.shape
    row_spec = pl.BlockSpec((None, tr, d), lambda b, j: (b, j, 0))
    return pl.pallas_call(
        functools.partial(_moe_out_kernel, alpha=alpha, k_gate=k_gate),
        grid=(bsz, rows // tr),
        in_specs=[row_spec,
                  pl.BlockSpec((None, None, tr, d // 2), lambda b, j: (0, b, j, 0)),
                  pl.BlockSpec((None, None, tr, d // 2), lambda b, j: (1, b, j, 0)),
                  pl.BlockSpec((None, tr, LANES), lambda b, j: (b, j, 0)),
                  pl.BlockSpec(mod.shape, lambda b, j: (0, 0)),
                  pl.BlockSpec((2, d), lambda b, j: (0, 0))],
        out_specs=row_spec,
        out_shape=jax.ShapeDtypeStruct((bsz, rows, d), F32),
        compiler_params=_cparams(("parallel", "parallel")),
        name="moe_residual_ln",
    )(x, yg, yg, route, mod, jnp.stack([gain, bias]))


def _pad_rows(w, rows):
    return jnp.zeros((rows,) + w.shape[1:], w.dtype).at[:w.shape[0]].set(w)


def _layer_layout(p, pool_w, rw):
    lr_raw = p["w2_f"].shape[0]
    gl_raw = p["g_up"].shape[0]
    vr_raw = p["v2"].shape[0] if "v2" in p else 0
    lr = -(-max(lr_raw, vr_raw, 1) // LANES) * LANES
    gl = -(-gl_raw // LANES) * LANES
    wl = 5 * lr + gl
    wl_pad = -(-wl // 1024) * 1024 if wl > 512 else wl
    core = pool_w + 3 * rw
    sizes = [lr_raw] * 4 + [gl_raw] + ([vr_raw] if vr_raw else [])
    slots = [lr] * 4 + [gl] + [lr]
    d = p["w_in"].shape[0]

    def relayout(src, rows_shape):
        dst = jnp.zeros(rows_shape + (core + wl_pad,), src.dtype)
        dst = dst.at[..., :core].set(src[..., :core])
        s_off, d_off = core, core
        for size, slot in zip(sizes, slots):
            dst = dst.at[..., d_off:d_off + size].set(src[..., s_off:s_off + size])
            s_off += size
            d_off += slot
        return dst

    w_in = relayout(p["w_in"].astype(BF16), (d,))
    zero_pool = jnp.zeros((pool_w,), F32)
    mu_p = relayout(jnp.concatenate([zero_pool, p["mu_prev"]]), ())
    mu_n = relayout(jnp.concatenate([zero_pool, p["mu_next"]]), ())
    mu_lora = jnp.zeros((SUBLANES, wl_pad), F32).at[0].set(mu_p[core:]).at[1].set(mu_n[core:])
    seg = lambda a, i: a[pool_w + i * rw:pool_w + (i + 1) * rw]
    zeros = jnp.zeros((rw,), F32)
    vec = jnp.stack([seg(mu_p, 0), seg(mu_n, 0), seg(mu_p, 1), seg(mu_n, 1), seg(mu_p, 2), seg(mu_n, 2),
                     p["w0_f"], p["w0_b"], p["a0_f"], p["a0_b"], p.get("v0", zeros),
                     p["k_k"], p["k_a"], p["r_k"].reshape(-1), p["lnx_g"], p["lnx_b"]])
    lw = {n: _pad_rows(p[n], lr).astype(BF16) for n in ("w2_f", "w2_b", "a2_f", "a2_b")}
    lw["g_up"] = _pad_rows(p["g_up"], gl).astype(BF16)
    if vr_raw:
        lw["v2"] = _pad_rows(p["v2"], lr).astype(BF16)
    return dict(w_in=w_in, mu_lora=mu_lora, vec=vec, lw=lw, lr=lr, gl=gl, wl=wl_pad, core=core)


def _mixer(x_all, mod, p, lay, v_first, pool_consts, *, lc, tr, tc, grid_w, tb, ctx_row):
    bsz, t, d = x_all.shape
    mix_w = p["w_out"].shape[0]
    pool_w = mix_w // 4
    pg = pool_w // len(POOL_WINDOWS)
    rw = mix_w - pool_w
    h = _ln_mod(x_all, mod, lc=lc, tr=tr, ctx_row=ctx_row, k_shift=0, k_scale=1)
    u = _matmul(h.reshape(bsz * t, d), lay["w_in"], out_dtype=BF16, tm=1024, tn=1024, tk=d,
                name="in_proj").reshape(bsz, t, -1)
    pool_wts = p["pool_w"].astype(BF16)
    pool_scale = p["pool_scale"].reshape(1, pool_w)
    pools = [_pool_group(u, pool_consts, pool_wts, pool_scale, i, lc=lc, grid_w=grid_w, tb=tb, pg=pg)
             for i in range(len(POOL_WINDOWS))]
    act = _lora_act(u, lay["mu_lora"], lc=lc, tr=tr, col0=lay["core"], wl=lay["wl"], lr=lay["lr"],
                    gl=lay["gl"])
    tm = _rwkv_terms(u, act, lay["vec"], lay["lw"], v_first, lc=lc, tr=tr, tc=tc, pool_w=pool_w,
                     rw=rw, lr=lay["lr"], gl=lay["gl"])
    y_f, y_b = _wkv_scan(tm, lc=lc, chunk=CHUNK, pairs=_pick(rw // LANES, SCAN_PAIRS, 1))
    out = _rwkv_out(y_f, y_b, tm["bonus"], tm["gate"], lay["vec"], tr=tr, tc=tc)
    mix_in = jnp.concatenate(pools + [out], axis=-1)
    mix = _matmul(mix_in.reshape(bsz * t, mix_w), p["w_out"].astype(BF16), out_dtype=BF16,
                  tm=1024, tn=1024, tk=mix_w, name="out_proj").reshape(bsz, t, d)
    return mix, tm["v"]


def _forward(x, c, ctx, c_ctx, layers, grid_w):
    bsz, seq, d = x.shape
    lc = ctx.shape[1]
    depth = len(layers)
    alpha = (2 * depth) ** 0.25
    assert bsz < COND_ROWS and lc % CHUNK == 0 and seq % CHUNK == 0 and seq % grid_w == 0
    tr = _pick(math.gcd(lc, seq), 256, SUBLANES)
    tb = tr if tr % grid_w == 0 else grid_w
    assert seq % tb == 0 and tb % grid_w == 0
    ctx_row = bsz
    cond = jnp.zeros((COND_ROWS, d), F32).at[:bsz].set(c).at[ctx_row].set(c_ctx)
    x_all = jnp.concatenate([ctx, x], axis=1)
    pool_consts = _pool_consts(lc, seq, grid_w, tb)
    common = dict(tr=tr, lc=lc, ctx_row=ctx_row)
    v_first = None
    for i, p in enumerate(layers):
        last = i == depth - 1
        mix_w = p["w_out"].shape[0]
        pool_w = mix_w // 4
        rw = mix_w - pool_w
        assert pool_w % (len(POOL_WINDOWS) * LANES) == 0 and rw % LANES == 0
        tc = _pick(rw, 512, LANES)
        lay = _layer_layout(p, pool_w, rw)
        mod = _modulation(cond, p["w_ada"], p["b_ada"])
        mix, v_cur = _mixer(x_all, mod, p, lay, v_first, pool_consts, tc=tc, grid_w=grid_w, tb=tb,
                            **common)
        if v_first is None:
            v_first = v_cur
        if not last:
            x1, h2 = _residual_ln(x_all, mix, mod, p["ln1_g"], p["ln1_b"], alpha=alpha, k_gate=2,
                                  mod2=(3, 4), **common)
            t = x_all.shape[1]
            hid = _glu(h2.reshape(bsz * t, d), p["ffn_w1"].astype(BF16), p["ffn_w3"].astype(BF16),
                       tm=1024, tn=256, name="ffn_glu")
            ffn = _matmul(hid, p["ffn_w2"].astype(BF16), out_dtype=BF16, tm=512, tn=1024, tk=5504,
                          name="ffn_down").reshape(bsz, t, d)
            x_all, = _residual_ln(x1, ffn, mod, p["ln2_g"], p["ln2_b"], alpha=alpha, k_gate=5, **common)
        else:
            x1, h2, route = _residual_ln(x_all, mix, mod, p["ln1_g"], p["ln1_b"], alpha=alpha, k_gate=2,
                                         mod2=(3, 4), router=p["router"], pack_h=True, x_off=lc,
                                         y_off=lc, rows=seq, **common)
            n_exp = p["router"].shape[1]
            ntok = bsz * seq
            tile = _pick(2 * ntok, 512, SUBLANES)
            src_token, pos, tile_expert, tile_valid = _route_plan(route.reshape(ntok, LANES), n_exp, tile)
            xs = _gather_rows(h2.reshape(ntok, d // 2), src_token, rows=256)
            ys = _moe_experts(xs, p["exp_w1"].astype(BF16), p["exp_w3"].astype(BF16),
                              p["exp_w2"].astype(BF16), tile_expert, tile_valid, tile=tile,
                              tn_up=512, tn_down=1024)
            yg = _gather_rows(ys, pos, rows=256).reshape(2, bsz, seq, d // 2)
            return _moe_residual_ln(x1, yg, route, mod, p["ln2_g"], p["ln2_b"], alpha=alpha, tr=tr,
                                    k_gate=5)
    return x_all[:, lc:]


_LAYER0 = ("w_ada", "b_ada", "w_in", "mu_prev", "mu_next", "pool_w", "pool_scale", "w0_f", "w2_f",
           "w0_b", "w2_b", "a0_f", "a2_f", "a0_b", "a2_b", "g_up", "k_k", "k_a", "r_k", "lnx_g",
           "lnx_b", "w_out", "ln1_g", "ln1_b", "ln2_g", "ln2_b", "ffn_w1", "ffn_w3", "ffn_w2")
_LAYER1 = _LAYER0[:26] + ("v0", "v2", "router", "exp_w1", "exp_w3", "exp_w2")


def kernel(x, c, ctx, c_ctx, l0_w_ada, l0_b_ada, l0_w_in, l0_mu_prev, l0_mu_next, l0_pool_w, l0_pool_scale, l0_w0_f, l0_w2_f, l0_w0_b, l0_w2_b, l0_a0_f, l0_a2_f, l0_a0_b, l0_a2_b, l0_g_up, l0_k_k, l0_k_a, l0_r_k, l0_lnx_g, l0_lnx_b, l0_w_out, l0_ln1_g, l0_ln1_b, l0_ln2_g, l0_ln2_b, l0_ffn_w1, l0_ffn_w3, l0_ffn_w2, l1_w_ada, l1_b_ada, l1_w_in, l1_mu_prev, l1_mu_next, l1_pool_w, l1_pool_scale, l1_w0_f, l1_w2_f, l1_w0_b, l1_w2_b, l1_a0_f, l1_a2_f, l1_a0_b, l1_a2_b, l1_g_up, l1_k_k, l1_k_a, l1_r_k, l1_lnx_g, l1_lnx_b, l1_w_out, l1_ln1_g, l1_ln1_b, l1_ln2_g, l1_ln2_b, l1_v0, l1_v2, l1_router, l1_exp_w1, l1_exp_w3, l1_exp_w2):
    l0 = dict(zip(_LAYER0, (l0_w_ada, l0_b_ada, l0_w_in, l0_mu_prev, l0_mu_next, l0_pool_w, l0_pool_scale, l0_w0_f, l0_w2_f, l0_w0_b, l0_w2_b, l0_a0_f, l0_a2_f, l0_a0_b, l0_a2_b, l0_g_up, l0_k_k, l0_k_a, l0_r_k, l0_lnx_g, l0_lnx_b, l0_w_out, l0_ln1_g, l0_ln1_b, l0_ln2_g, l0_ln2_b, l0_ffn_w1, l0_ffn_w3, l0_ffn_w2)))
    l1 = dict(zip(_LAYER1, (l1_w_ada, l1_b_ada, l1_w_in, l1_mu_prev, l1_mu_next, l1_pool_w, l1_pool_scale, l1_w0_f, l1_w2_f, l1_w0_b, l1_w2_b, l1_a0_f, l1_a2_f, l1_a0_b, l1_a2_b, l1_g_up, l1_k_k, l1_k_a, l1_r_k, l1_lnx_g, l1_lnx_b, l1_w_out, l1_ln1_g, l1_ln1_b, l1_ln2_g, l1_ln2_b, l1_v0, l1_v2, l1_router, l1_exp_w1, l1_exp_w3, l1_exp_w2)))
    return _forward(x, c, ctx, c_ctx, [l0, l1], GRID_W)
```

```python
import functools
import math

import numpy as np
import jax
import jax.numpy as jnp
from jax import lax
from jax.experimental import pallas as pl
from jax.experimental.pallas import tpu as pltpu

F32 = jnp.float32
BF16 = jnp.bfloat16

GRID_W = 64
POOL_WINDOWS = (2, 4, 8, 16)
HEAD = 64
LANES = 128
SUBLANES = 8
HALO = 16
N_ADA = 6
LN_EPS = 1e-5
GN_EPS = 64e-5
NORM_EPS = 1e-12
CHUNK = 64
SCAN_PAIRS = 8
SCAN_SUB = 4
INV_PASSES = 1
VMEM_LIMIT = 56 * 1024 * 1024
COND_ROWS = 16
PACK_GROUP = 1024


def _cparams(sem):
    return pltpu.CompilerParams(dimension_semantics=sem, vmem_limit_bytes=VMEM_LIMIT)


def _pick(n, target, mult):
    best = None
    for d in range(mult, min(n, target) + 1, mult):
        if n % d == 0:
            best = d
    return best if best is not None else n


def _dot(a, b):
    return jnp.dot(a, b, preferred_element_type=F32)


def _dot_nt(a, b):
    return lax.dot_general(a, b, (((1,), (1,)), ((), ())), preferred_element_type=F32)


def _split2(x):
    hi = x.astype(BF16)
    lo = (x - hi.astype(F32)).astype(BF16)
    return hi, lo


def _mm(a, b, passes=1):
    if passes == 1:
        return _dot(a.astype(BF16), b.astype(BF16))
    a_hi, a_lo = _split2(a)
    b_hi, b_lo = _split2(b)
    return _dot(a_hi, b_hi) + _dot(a_lo, b_hi) + _dot(a_hi, b_lo)


def _sigmoid(x):
    return 1.0 / (1.0 + jnp.exp(-x))


def _pack_pairs(x, group):
    half = group // 2
    words = []
    for g in range(x.shape[1] // group):
        lo = lax.bitcast_convert_type(x[:, g * group:g * group + half].astype(BF16).astype(F32), jnp.uint32)
        hi = lax.bitcast_convert_type(x[:, g * group + half:(g + 1) * group].astype(BF16).astype(F32), jnp.uint32)
        words.append((lo >> 16) | hi)
    return words[0] if len(words) == 1 else jnp.concatenate(words, axis=1)


def _unpack_pairs(p, group):
    half = group // 2
    cols = []
    for g in range(p.shape[1] // half):
        word = p[:, g * half:(g + 1) * half]
        cols.append(lax.bitcast_convert_type(word << 16, F32))
        cols.append(lax.bitcast_convert_type(word & jnp.uint32(0xFFFF0000), F32))
    return jnp.concatenate(cols, axis=1)


def _layer_norm_rows(x):
    mu = jnp.mean(x, axis=-1, keepdims=True)
    xc = x - mu
    var = jnp.mean(xc * xc, axis=-1, keepdims=True)
    return xc * lax.rsqrt(var + LN_EPS)


def _mod_kernel(c_ref, w_ref, b_ref, o_ref):
    c = c_ref[...]
    s = c * _sigmoid(c)
    o_ref[...] = _mm(s, w_ref[...], passes=3) + b_ref[...]


def _modulation(cond, w_ada, b_ada):
    d, n = w_ada.shape
    tn = _pick(n, 512, LANES)
    return pl.pallas_call(
        _mod_kernel,
        grid=(n // tn,),
        in_specs=[pl.BlockSpec((COND_ROWS, d), lambda j: (0, 0)),
                  pl.BlockSpec((d, tn), lambda j: (0, j)),
                  pl.BlockSpec((1, tn), lambda j: (0, j))],
        out_specs=pl.BlockSpec((COND_ROWS, tn), lambda j: (0, j)),
        out_shape=jax.ShapeDtypeStruct((COND_ROWS, n), F32),
        compiler_params=_cparams(("parallel",)),
        name="modulation",
    )(cond, w_ada, b_ada.reshape(1, n))


def _mod_row(mod_ref, row, chunk, d):
    return mod_ref[pl.ds(row, 1), chunk * d:(chunk + 1) * d]


def _ln_mod_kernel(x_ref, mod_ref, h_ref, *, nctx_blk, ctx_row, k_shift, k_scale):
    b, j = pl.program_id(0), pl.program_id(1)
    d = x_ref.shape[-1]
    row = jnp.where(j < nctx_blk, ctx_row, b)
    shift = _mod_row(mod_ref, row, k_shift, d)
    scale = _mod_row(mod_ref, row, k_scale, d)
    h_ref[...] = (_layer_norm_rows(x_ref[...]) * (1.0 + scale) + shift).astype(h_ref.dtype)


def _ln_mod(x_all, mod, *, lc, tr, ctx_row, k_shift, k_scale):
    bsz, t, d = x_all.shape
    kern = functools.partial(_ln_mod_kernel, nctx_blk=lc // tr, ctx_row=ctx_row,
                             k_shift=k_shift, k_scale=k_scale)
    return pl.pallas_call(
        kern,
        grid=(bsz, t // tr),
        in_specs=[pl.BlockSpec((None, tr, d), lambda b, j: (b, j, 0)),
                  pl.BlockSpec(mod.shape, lambda b, j: (0, 0))],
        out_specs=pl.BlockSpec((None, tr, d), lambda b, j: (b, j, 0)),
        out_shape=jax.ShapeDtypeStruct((bsz, t, d), BF16),
        compiler_params=_cparams(("parallel", "parallel")),
        name="ln_modulate",
    )(x_all, mod)


def _mm_kernel(a_ref, w_ref, o_ref, *scratch, nk):
    if nk == 1:
        o_ref[...] = _dot(a_ref[...], w_ref[...]).astype(o_ref.dtype)
        return
    acc_ref, = scratch
    k = pl.program_id(2)

    @pl.when(k == 0)
    def _():
        acc_ref[...] = jnp.zeros_like(acc_ref)

    acc_ref[...] += _dot(a_ref[...], w_ref[...])

    @pl.when(k == nk - 1)
    def _():
        o_ref[...] = acc_ref[...].astype(o_ref.dtype)


def _matmul(a, w, *, out_dtype, tm, tn, tk, name):
    m, kdim = a.shape
    n = w.shape[1]
    tm, tn, tk = _pick(m, tm, SUBLANES), _pick(n, tn, LANES), _pick(kdim, tk, LANES)
    nk = kdim // tk
    scratch = [pltpu.VMEM((tm, tn), F32)] if nk > 1 else []
    return pl.pallas_call(
        functools.partial(_mm_kernel, nk=nk),
        grid=(m // tm, n // tn, nk),
        in_specs=[pl.BlockSpec((tm, tk), lambda i, j, k: (i, k)),
                  pl.BlockSpec((tk, tn), lambda i, j, k: (k, j))],
        out_specs=pl.BlockSpec((tm, tn), lambda i, j, k: (i, j)),
        out_shape=jax.ShapeDtypeStruct((m, n), out_dtype),
        scratch_shapes=scratch,
        compiler_params=_cparams(("parallel", "parallel", "arbitrary")),
        name=name,
    )(a, w)


def _glu_kernel(a_ref, w1_ref, w3_ref, o_ref):
    a = a_ref[...]
    p1 = _dot(a, w1_ref[...])
    p3 = _dot(a, w3_ref[...])
    o_ref[...] = (p1 * _sigmoid(p1) * p3).astype(o_ref.dtype)


def _glu(a, w1, w3, *, tm, tn, name):
    m, kdim = a.shape
    f = w1.shape[1]
    tm, tn = _pick(m, tm, SUBLANES), _pick(f, tn, LANES)
    w_spec = pl.BlockSpec((kdim, tn), lambda i, j: (0, j))
    return pl.pallas_call(
        _glu_kernel,
        grid=(m // tm, f // tn),
        in_specs=[pl.BlockSpec((tm, kdim), lambda i, j: (i, 0)), w_spec, w_spec],
        out_specs=pl.BlockSpec((tm, tn), lambda i, j: (i, j)),
        out_shape=jax.ShapeDtypeStruct((m, f), BF16),
        compiler_params=_cparams(("parallel", "parallel")),
        name=name,
    )(a, w1, w3)


def _pool_consts(lc, tl, grid_w, tb):
    rows = tl // grid_w
    cb, c1, inv = [], [], []
    for win in POOL_WINDOWS:
        half = win // 2
        t = np.arange(tb)
        same_row = (t[:, None] // grid_w) == (t[None, :] // grid_w)
        dc = (t[None, :] % grid_w) - (t[:, None] % grid_w)
        cb.append((same_row & (dc >= -half) & (dc < half)).astype(np.float32))
        tc = np.arange(lc)
        d1 = tc[None, :] - tc[:, None]
        c1.append(((d1 >= -half) & (d1 < half)).astype(np.float32))
        cnt1 = np.minimum(tc + half, lc) - np.maximum(tc - half, 0)
        g = np.arange(grid_w)
        cntc = np.minimum(g + half, grid_w) - np.maximum(g - half, 0)
        r = np.arange(rows)
        cntr = np.minimum(r + half, rows) - np.maximum(r - half, 0)
        cnt2 = (cntr[:, None] * cntc[None, :]).reshape(-1)
        iv = 1.0 / np.concatenate([cnt1, cnt2]).astype(np.float64)
        inv.append(np.broadcast_to(iv[:, None], (lc + tl, LANES)).astype(np.float32))
    return (jnp.asarray(np.stack(cb), BF16), jnp.asarray(np.stack(c1), BF16),
            jnp.asarray(np.stack(inv), F32))


def _pool_kernel(u_ref, cb_ref, c1_ref, inv_ref, pw_ref, ps_ref, o_ref, s1_ref, acc_ref,
                 *, lc, tb, grid_w, half):
    t = u_ref.shape[0]
    tl = t - lc
    acc_ref[0:lc, :] = _dot(c1_ref[...], u_ref[0:lc, :])
    cb = cb_ref[...]
    for blk in range(tl // tb):
        lo, hi = lc + blk * tb, lc + (blk + 1) * tb
        s1_ref[lo:hi, :] = _dot(cb, u_ref[lo:hi, :])
    acc_ref[lc:t, :] = s1_ref[lc:t, :]
    for dr in range(-half, half):
        if dr == 0:
            continue
        sh = abs(dr) * grid_w
        if sh >= tl:
            continue
        if dr > 0:
            acc_ref[lc:t - sh, :] += s1_ref[lc + sh:t, :]
        else:
            acc_ref[lc + sh:t, :] += s1_ref[lc:t - sh, :]
    n_rep = u_ref.shape[1] // LANES
    diff = acc_ref[...] * jnp.concatenate([inv_ref[...]] * n_rep, axis=1) - u_ref[...].astype(F32)
    o_ref[...] = (_dot(diff.astype(BF16), pw_ref[...]) * ps_ref[...]).astype(o_ref.dtype)


def _pool_group(u_all, consts, pool_w, pool_scale, idx, *, lc, grid_w, tb, pg):
    bsz, t, _ = u_all.shape
    cb, c1, inv = consts
    half = POOL_WINDOWS[idx] // 2
    kern = functools.partial(_pool_kernel, lc=lc, tb=tb, grid_w=grid_w, half=half)
    return pl.pallas_call(
        kern,
        grid=(bsz,),
        in_specs=[pl.BlockSpec((None, t, pg), lambda b: (b, 0, idx)),
                  pl.BlockSpec((None, tb, tb), lambda b: (idx, 0, 0)),
                  pl.BlockSpec((None, lc, lc), lambda b: (idx, 0, 0)),
                  pl.BlockSpec((None, t, LANES), lambda b: (idx, 0, 0)),
                  pl.BlockSpec((None, pg, pg), lambda b: (idx, 0, 0)),
                  pl.BlockSpec((1, pg), lambda b: (0, idx))],
        out_specs=pl.BlockSpec((None, t, pg), lambda b: (b, 0, 0)),
        out_shape=jax.ShapeDtypeStruct((bsz, t, pg), BF16),
        scratch_shapes=[pltpu.VMEM((t, pg), F32), pltpu.VMEM((t, pg), F32)],
        compiler_params=_cparams(("parallel",)),
        name="pool_mixer_%d" % idx,
    )(u_all, cb, c1, inv, pool_w, pool_scale)


def _token_shift(f_ref, prev_ref, next_ref, mu_p, mu_n, seg_first, seg_last, shift_ref=None):
    f = f_ref[...].astype(F32)
    tr = f.shape[0]
    row = lax.broadcasted_iota(jnp.int32, f.shape, 0)
    prev_row = jnp.where(seg_first, 0.0, prev_ref[...].astype(F32)[HALO - 1:HALO, :])
    next_row = jnp.where(seg_last, 0.0, next_ref[...].astype(F32)[0:1, :])
    if shift_ref is None:
        prev = jnp.where(row == 0, prev_row, pltpu.roll(f, 1, 0))
        nxt = jnp.where(row == tr - 1, next_row, pltpu.roll(f, tr - 1, 0))
    else:
        moved = _dot(shift_ref[...], f_ref[...])
        prev = jnp.where(row == 0, prev_row, moved[:tr])
        nxt = jnp.where(row == tr - 1, next_row, moved[tr:])
    return f + mu_p * (prev - f) + mu_n * (nxt - f)


def _shift_matrix(tr):
    t = np.arange(tr)
    prev = (t[:, None] - 1 == t[None, :]).astype(np.float32)
    nxt = (t[:, None] + 1 == t[None, :]).astype(np.float32)
    return jnp.asarray(np.concatenate([prev, nxt], axis=0), BF16)


def _segment_flags(j, nctx_blk, n_blk):
    seg_first = jnp.logical_or(j == 0, j == nctx_blk)
    seg_last = jnp.logical_or(j == nctx_blk - 1, j == n_blk - 1)
    return seg_first, seg_last


def _shift_specs(tr, tc, col_blk, t):
    per_blk = tr // HALO
    last = t // HALO - 1
    cur = pl.BlockSpec((None, tr, tc), lambda b, j, c: (b, j, col_blk(c)))
    prv = pl.BlockSpec((None, HALO, tc),
                       lambda b, j, c: (b, jnp.maximum(j * per_blk - 1, 0), col_blk(c)))
    nxt = pl.BlockSpec((None, HALO, tc),
                       lambda b, j, c: (b, jnp.minimum((j + 1) * per_blk, last), col_blk(c)))
    return [cur, prv, nxt]


def _lora_kernel(u_ref, up_ref, un_ref, mu_ref, o_ref, *, nctx_blk, n_blk, lr, gl):
    j = pl.program_id(1)
    seg_first, seg_last = _segment_flags(j, nctx_blk, n_blk)
    z = _token_shift(u_ref, up_ref, un_ref, mu_ref[0:1, :], mu_ref[1:2, :], seg_first, seg_last)
    col = lax.broadcasted_iota(jnp.int32, z.shape, 1)
    act = jnp.where(col < 2 * lr, jnp.tanh(z),
                    jnp.where(jnp.logical_and(col >= 4 * lr, col < 4 * lr + gl), _sigmoid(z), z))
    o_ref[...] = act.astype(o_ref.dtype)


def _lora_act(u_all, mu_lora, *, lc, tr, col0, wl, lr, gl):
    bsz, t, _ = u_all.shape
    specs = _shift_specs(tr, wl, lambda c: col0 // wl, t)
    kern = functools.partial(_lora_kernel, nctx_blk=lc // tr, n_blk=t // tr, lr=lr, gl=gl)
    return pl.pallas_call(
        kern,
        grid=(bsz, t // tr, 1),
        in_specs=specs + [pl.BlockSpec((SUBLANES, wl), lambda b, j, c: (0, 0))],
        out_specs=pl.BlockSpec((None, tr, wl), lambda b, j, c: (b, j, 0)),
        out_shape=jax.ShapeDtypeStruct((bsz, t, wl), BF16),
        compiler_params=_cparams(("parallel", "parallel", "arbitrary")),
        name="lora_act",
    )(u_all, u_all, u_all, mu_lora)


def _head_ones():
    r = lax.broadcasted_iota(jnp.int32, (LANES, LANES), 0)
    c = lax.broadcasted_iota(jnp.int32, (LANES, LANES), 1)
    return jnp.where((r ^ c) < HEAD, 1.0, 0.0).astype(BF16)


def _head_sum(x, ones, passes=2):
    outs = []
    for s in range(x.shape[1] // LANES):
        part = x[:, s * LANES:(s + 1) * LANES]
        if passes == 1:
            outs.append(_dot(part.astype(BF16), ones))
            continue
        hi, lo = _split2(part)
        outs.append(_dot(hi, ones) + _dot(lo, ones))
    return outs[0] if len(outs) == 1 else jnp.concatenate(outs, axis=1)


(_V_MPR, _V_MNR, _V_MPK, _V_MNK, _V_MPV, _V_MNV, _V_W0F, _V_W0B, _V_A0F, _V_A0B,
 _V_V0, _V_KK, _V_KA, _V_RK, _V_LG, _V_LB) = range(16)
_DECAY_SCALE = math.exp(-0.5)


def _terms_kernel(*refs, nctx_blk, n_blk, lr, gl, has_vres):
    (ur, urp, urn, uk, ukp, ukn, uv, uvp, uvn, act_ref, vec_ref,
     w2f_ref, w2b_ref, a2f_ref, a2b_ref, gup_ref, shift_ref) = refs[:17]
    pos = 17
    if has_vres:
        v2_ref, vfirst_ref = refs[pos:pos + 2]
        pos += 2
    (r_o, v_o, kk_o, lwf_o, lwb_o, kf_o, kb_o, af_o, ab_o, gate_o, bonus_o) = refs[pos:]

    j = pl.program_id(1)
    seg_first, seg_last = _segment_flags(j, nctx_blk, n_blk)
    vec = lambda i: vec_ref[i:i + 1, :]
    r = _token_shift(ur, urp, urn, vec(_V_MPR), vec(_V_MNR), seg_first, seg_last, shift_ref)
    k = _token_shift(uk, ukp, ukn, vec(_V_MPK), vec(_V_MNK), seg_first, seg_last, shift_ref)
    v = _token_shift(uv, uvp, uvn, vec(_V_MPV), vec(_V_MNV), seg_first, seg_last, shift_ref)

    act = act_ref[...]
    a_wf, a_wb = act[:, 0:lr], act[:, lr:2 * lr]
    a_af, a_ab = act[:, 2 * lr:3 * lr], act[:, 3 * lr:4 * lr]
    a_g = act[:, 4 * lr:4 * lr + gl]
    if has_vres:
        a_v = act[:, 4 * lr + gl:5 * lr + gl]
        v = v + (vfirst_ref[...].astype(F32) - v) * _sigmoid(vec(_V_V0) + _dot(a_v, v2_ref[...]))

    lwf_o[...] = -_DECAY_SCALE * _sigmoid(vec(_V_W0F) + _dot(a_wf, w2f_ref[...]))
    lwb_o[...] = -_DECAY_SCALE * _sigmoid(vec(_V_W0B) + _dot(a_wb, w2b_ref[...]))
    a_f = _sigmoid(vec(_V_A0F) + _dot(a_af, a2f_ref[...]))
    a_b = _sigmoid(vec(_V_A0B) + _dot(a_ab, a2b_ref[...]))

    ones = _head_ones()
    kkr = k * vec(_V_KK)
    kk_o[...] = (kkr * lax.rsqrt(jnp.maximum(_head_sum(kkr * kkr, ones, 1), NORM_EPS * NORM_EPS))
                 ).astype(kk_o.dtype)
    k_f = k * (1.0 + (a_f - 1.0) * vec(_V_KA))
    k_b = k * (1.0 + (a_b - 1.0) * vec(_V_KA))
    bonus_o[...] = (_head_sum(r * (k_f + k_b) * vec(_V_RK), ones) * v).astype(bonus_o.dtype)
    gate_o[...] = _dot(a_g, gup_ref[...]).astype(gate_o.dtype)
    r_o[...] = r.astype(r_o.dtype)
    v_o[...] = v.astype(v_o.dtype)
    kf_o[...] = k_f.astype(kf_o.dtype)
    kb_o[...] = k_b.astype(kb_o.dtype)
    af_o[...] = a_f.astype(af_o.dtype)
    ab_o[...] = a_b.astype(ab_o.dtype)


def _rwkv_terms(u_all, act, vec, lw, v_first, *, lc, tr, tc, pool_w, rw, lr, gl):
    bsz, t, _ = u_all.shape
    has_vres = v_first is not None
    nb = rw // tc
    specs = []
    for part in range(3):
        base = (pool_w + part * rw) // tc
        specs += _shift_specs(tr, tc, (lambda c, base=base: base + c), t)
    specs.append(pl.BlockSpec((None, tr, act.shape[-1]), lambda b, j, c: (b, j, 0)))
    specs.append(pl.BlockSpec((vec.shape[0], tc), lambda b, j, c: (0, c)))
    wspec = lambda rows: pl.BlockSpec((rows, tc), lambda b, j, c: (0, c))
    specs += [wspec(lr), wspec(lr), wspec(lr), wspec(lr), wspec(gl)]
    specs.append(pl.BlockSpec((2 * tr, tr), lambda b, j, c: (0, 0)))
    args = [u_all] * 9 + [act, vec, lw["w2_f"], lw["w2_b"], lw["a2_f"], lw["a2_b"], lw["g_up"],
                          _shift_matrix(tr)]
    if has_vres:
        specs += [wspec(lr), pl.BlockSpec((None, tr, tc), lambda b, j, c: (b, j, c))]
        args += [lw["v2"], v_first]
    out_spec = pl.BlockSpec((None, tr, tc), lambda b, j, c: (b, j, c))
    kern = functools.partial(_terms_kernel, nctx_blk=lc // tr, n_blk=t // tr, lr=lr, gl=gl,
                             has_vres=has_vres)
    names = ("r", "v", "kk", "lw_f", "lw_b", "k_f", "k_b", "a_f", "a_b", "gate", "bonus")
    dtypes = [F32 if n.startswith("lw") else BF16 for n in names]
    outs = pl.pallas_call(
        kern,
        grid=(bsz, t // tr, nb),
        in_specs=specs,
        out_specs=[out_spec] * len(names),
        out_shape=[jax.ShapeDtypeStruct((bsz, t, rw), dt) for dt in dtypes],
        compiler_params=_cparams(("parallel", "parallel", "parallel")),
        name="rwkv_terms",
    )(*args)
    return dict(zip(names, outs))


def _cumsum_rows(x, rev):
    n = x.shape[0]
    row = lax.broadcasted_iota(jnp.int32, x.shape, 0)
    s = 1
    while s < n:
        if rev:
            x = x + jnp.where(row < n - s, pltpu.roll(x, n - s, 0), 0.0)
        else:
            x = x + jnp.where(row >= s, pltpu.roll(x, s, 0), 0.0)
        s *= 2
    return x


def _wkv_chunks(chains):
    n = chains[0][0].shape[0]
    hp = 2 * n
    lane = lax.broadcasted_iota(jnp.int32, (n, LANES), 1)
    first_head = lane < HEAD
    ri = lax.broadcasted_iota(jnp.int32, (hp, hp), 0)
    ci = lax.broadcasted_iota(jnp.int32, (hp, hp), 1)
    blk = ri ^ ci
    eye = ri == ci
    tdiff = (ci & (n - 1)) - (ri & (n - 1))
    dist = {False: jnp.where(blk < n, tdiff, hp), True: jnp.where(blk < n, -tdiff, hp)}

    def stack(x):
        return jnp.concatenate([jnp.where(first_head, x, 0.0), jnp.where(first_head, 0.0, x)], axis=0)

    pre = []
    for r, v, kk, lw, k, a, state, rev in chains:
        c = _cumsum_rows(lw, rev)
        ctot = c[0:1, :] if rev else c[n - 1:n, :]
        e_pos, e_neg = jnp.exp(c), jnp.exp(-c)
        e_prev, e_rem = jnp.exp(c - lw), jnp.exp(ctot - c)
        kka = kk * a
        bt, kt = kka * e_neg, k * e_neg
        pre.append(dict(s_at=stack(-kk * e_prev), s_rt=stack(r * e_pos), s_v=stack(v), rt=r * e_pos,
                        rhs1=jnp.concatenate([bt, bt, kt, kt], axis=0).astype(BF16),
                        lhs_t=jnp.concatenate([stack(kka * e_rem), stack(k * e_rem)], axis=0),
                        wtot=jnp.exp(ctot), strict=dist[rev] < 0, incl=dist[rev] <= 0))

    s1 = [_dot_nt(jnp.concatenate([p["s_at"], p["s_rt"]], axis=0).astype(BF16), p["rhs1"]) for p in pre]
    nmat = [jnp.where(p["strict"], s[:hp, :hp], 0.0) for p, s in zip(pre, s1)]
    m_ak = [jnp.where(p["strict"], s[:hp, hp:], 0.0) for p, s in zip(pre, s1)]
    m_rbk = [jnp.concatenate([jnp.where(p["incl"], s[hp:, :hp], 0.0), jnp.where(p["incl"], s[hp:, hp:], 0.0)],
                             axis=1).astype(BF16) for p, s in zip(pre, s1)]
    z = [_mm(m, p["s_v"]) for m, p in zip(m_ak, pre)]

    n8 = [jnp.where(blk < SUBLANES, m, 0.0) for m in nmat]
    tmat = [jnp.where(eye, 1.0, m) for m in n8]
    n2 = [_mm(m, m, INV_PASSES) for m in n8]
    st = [_mm(jnp.concatenate([t, m], axis=0), m, INV_PASSES) for t, m in zip(tmat, n2)]
    tmat = [t + s[:hp] for t, s in zip(tmat, st)]
    tmat = [t + _mm(t, s[hp:], INV_PASSES) for t, s in zip(tmat, st)]
    size = SUBLANES
    while size < n:
        sel = jnp.logical_and(blk >= size, blk < 2 * size)
        pieces = [[(g * 2 * size + (0 if chain[7] else size), g * 2 * size + (size if chain[7] else 2 * size))
                   for g in range(hp // (2 * size))] for chain in chains]

        def take(mat, rows):
            return jnp.concatenate([mat[a:b] for a, b in rows], axis=0)

        def put(mat, rows, part, keep):
            out, at, prev = [], 0, 0
            for a, b in rows:
                if a > prev:
                    out.append(mat[prev:a] if keep else jnp.zeros((a - prev, mat.shape[1]), F32))
                out.append(part[at:at + b - a])
                at, prev = at + b - a, b
            if prev < mat.shape[0]:
                out.append(mat[prev:] if keep else jnp.zeros((mat.shape[0] - prev, mat.shape[1]), F32))
            return jnp.concatenate(out, axis=0)

        x = [_mm(take(jnp.where(sel, m, 0.0), rows), t, INV_PASSES) for m, t, rows in zip(nmat, tmat, pieces)]
        x = [put(t, rows, xx, keep=False) for t, rows, xx in zip(tmat, pieces, x)]
        th = [take(t, rows) for t, rows in zip(tmat, pieces)]
        th = [h + _mm(h, xx, INV_PASSES) for h, xx in zip(th, x)]
        tmat = [put(t, rows, h, keep=True) for t, rows, h in zip(tmat, pieces, th)]
        size *= 2

    gu = [_mm(t, jnp.concatenate([p["s_at"], zz], axis=1)) for t, p, zz in zip(tmat, pre, z)]
    zeros = jnp.zeros((hp, LANES), F32)
    rhs4 = [jnp.concatenate([g, jnp.concatenate([zeros, p["s_v"]], axis=1)], axis=0).astype(BF16)
            for g, p in zip(gu, pre)]
    tb = [_dot(jnp.concatenate([m, p["lhs_t"].T.astype(BF16)], axis=0), rr)
          for m, p, rr in zip(m_rbk, pre, rhs4)]
    top = [x_[:hp] for x_ in tb]
    bot = [x_[hp:] for x_ in tb]

    outs = []
    for p, tp, bt_, chain in zip(pre, top, bot, chains):
        q = p["rt"] + tp[:n, :LANES] + tp[n:, :LANES]
        amat = jnp.where(eye, jnp.broadcast_to(p["wtot"], (LANES, LANES)), 0.0) + bt_[:, :LANES]
        qa = jnp.concatenate([q, amat], axis=0).astype(BF16)
        s_hi, s_lo = _split2(chain[6])
        outs.append(_dot(qa, s_hi) + _dot(qa, s_lo))
    return [(o[:n] + tp[:n, LANES:] + tp[n:, LANES:], o[n:] + bt_[:, LANES:])
            for o, tp, bt_ in zip(outs, top, bot)]


def _scan_kernel(*refs, pairs, chunk):
    ins, (yf_ref, yb_ref, state_ref) = refs[:12], refs[12:]
    s = pl.program_id(1)
    n_sub = yf_ref.shape[0] // chunk

    @pl.when(s == 0)
    def _():
        state_ref[...] = jnp.zeros_like(state_ref)

    chains = [(d, p, slice(p * LANES, (p + 1) * LANES)) for d in range(2) for p in range(pairs)]

    def sub_step(i, carry):
        rows = [pl.ds(pl.multiple_of((n_sub - 1 - i if d else i) * chunk, chunk), chunk) for d in range(2)]
        loaded = [tuple(ref[rows[d], cols].astype(F32) for ref in ins[6 * d:6 * d + 6])
                  + (state_ref[d, p], bool(d)) for d, p, cols in chains]
        for (d, p, cols), (y, new_state) in zip(chains, _wkv_chunks(loaded)):
            y_ref = yb_ref if d else yf_ref
            y_ref[rows[d], cols] = y.astype(y_ref.dtype)
            state_ref[d, p] = new_state
        return carry

    lax.fori_loop(0, n_sub, sub_step, 0)


def _wkv_scan(tm, *, lc, chunk, pairs):
    bsz, t, rw = tm["r"].shape
    width = pairs * LANES
    ngrp = rw // width
    n_sub = math.gcd(math.gcd(lc // chunk, (t - lc) // chunk), SCAN_SUB)
    step_rows = n_sub * chunk
    nctx, ntot = lc // step_rows, t // step_rows

    def fwd(g, s):
        return (g // ngrp, s, g % ngrp)

    def bwd(g, s):
        return (g // ngrp, jnp.where(s < nctx, nctx - 1 - s, ntot - 1 - (s - nctx)), g % ngrp)

    blk = (None, step_rows, width)
    names_f = ("r", "v", "kk", "lw_f", "k_f", "a_f")
    names_b = ("r", "v", "kk", "lw_b", "k_b", "a_b")
    in_specs = [pl.BlockSpec(blk, fwd)] * 6 + [pl.BlockSpec(blk, bwd)] * 6
    args = [tm[n] for n in names_f] + [tm[n] for n in names_b]
    return pl.pallas_call(
        functools.partial(_scan_kernel, pairs=pairs, chunk=chunk),
        grid=(bsz * ngrp, ntot),
        in_specs=in_specs,
        out_specs=[pl.BlockSpec(blk, fwd), pl.BlockSpec(blk, bwd)],
        out_shape=[jax.ShapeDtypeStruct((bsz, t, rw), BF16)] * 2,
        scratch_shapes=[pltpu.VMEM((2, pairs, LANES, LANES), F32)],
        compiler_params=_cparams(("parallel", "arbitrary")),
        name="wkv7_scan",
    )(*args)


def _rwkv_out_kernel(yf_ref, yb_ref, bonus_ref, gate_ref, vec_ref, o_ref):
    y = yf_ref[...].astype(F32) + yb_ref[...].astype(F32)
    ones = _head_ones()
    inv_n = 1.0 / HEAD
    mu = _head_sum(y, ones, 1) * inv_n
    yc = y - mu
    var = _head_sum(yc * yc, ones, 1) * inv_n
    yn = yc * lax.rsqrt(var + GN_EPS)
    yn = yn * vec_ref[_V_LG:_V_LG + 1, :] + vec_ref[_V_LB:_V_LB + 1, :]
    o_ref[...] = ((yn + bonus_ref[...].astype(F32)) * gate_ref[...].astype(F32)).astype(o_ref.dtype)


def _rwkv_out(y_f, y_b, bonus, gate, vec, *, tr, tc):
    bsz, t, rw = y_f.shape
    spec = pl.BlockSpec((None, tr, tc), lambda b, j, c: (b, j, c))
    return pl.pallas_call(
        _rwkv_out_kernel,
        grid=(bsz, t // tr, rw // tc),
        in_specs=[spec] * 4 + [pl.BlockSpec((vec.shape[0], tc), lambda b, j, c: (0, c))],
        out_specs=spec,
        out_shape=jax.ShapeDtypeStruct((bsz, t, rw), BF16),
        compiler_params=_cparams(("parallel", "parallel", "parallel")),
        name="rwkv_out",
    )(y_f, y_b, bonus, gate, vec)


_R_E1, _R_E2, _R_G1, _R_G2 = range(4)

def _res_kernel(*refs, alpha, row_off_blk, nctx_blk, ctx_row, k_gate, k_shift, k_scale,
                with_mod, n_exp):
    x_ref, y_ref, mod_ref, gb_ref = refs[:4]
    pos = 4
    if n_exp:
        router_ref = refs[pos]
        pos += 1
    outs = refs[pos:]
    b, j = pl.program_id(0), pl.program_id(1)
    d = x_ref.shape[-1]
    row = jnp.where(j + row_off_blk < nctx_blk, ctx_row, b)
    gate = _mod_row(mod_ref, row, k_gate, d)
    xn = _layer_norm_rows(alpha * x_ref[...] + gate * y_ref[...]) * gb_ref[0:1, :] + gb_ref[1:2, :]
    outs[0][...] = xn
    if not with_mod:
        return
    h = _layer_norm_rows(xn) * (1.0 + _mod_row(mod_ref, row, k_scale, d)) + _mod_row(mod_ref, row, k_shift, d)
    if outs[1].dtype == jnp.uint32:
        outs[1][...] = _pack_pairs(h, min(PACK_GROUP, d))
    else:
        outs[1][...] = h.astype(outs[1].dtype)
    if not n_exp:
        return
    logits = _mm(h, router_ref[...], passes=3)
    lane = lax.broadcasted_iota(jnp.int32, logits.shape, 1).astype(F32)
    neg = -jnp.inf
    logits = jnp.where(lane < n_exp, logits, neg)
    m1 = jnp.max(logits, axis=-1, keepdims=True)
    i1 = jnp.min(jnp.where(logits == m1, lane, float(LANES)), axis=-1, keepdims=True)
    rest = jnp.where(lane == i1, neg, logits)
    m2 = jnp.max(rest, axis=-1, keepdims=True)
    i2 = jnp.min(jnp.where(rest == m2, lane, float(LANES)), axis=-1, keepdims=True)
    e2 = jnp.exp(m2 - m1)
    g1 = 1.0 / (1.0 + e2)
    g2 = e2 / (1.0 + e2)
    outs[2][...] = jnp.where(lane == _R_E1, i1, jnp.where(lane == _R_E2, i2,
                             jnp.where(lane == _R_G1, g1, jnp.where(lane == _R_G2, g2, 0.0))))


def _residual_ln(x, y, mod, gain, bias, *, alpha, tr, lc, ctx_row, k_gate, x_off=0, y_off=0,
                 rows=None, mod2=None, router=None, pack_h=False):
    bsz, _, d = x.shape
    rows = x.shape[1] if rows is None else rows
    n_exp = 0 if router is None else router.shape[1]
    gb = jnp.stack([gain, bias])
    in_specs = [pl.BlockSpec((None, tr, d), lambda b, j: (b, j + x_off // tr, 0)),
                pl.BlockSpec((None, tr, d), lambda b, j: (b, j + y_off // tr, 0)),
                pl.BlockSpec(mod.shape, lambda b, j: (0, 0)),
                pl.BlockSpec((2, d), lambda b, j: (0, 0))]
    args = [x, y, mod, gb]
    out_specs = [pl.BlockSpec((None, tr, d), lambda b, j: (b, j, 0))]
    out_shape = [jax.ShapeDtypeStruct((bsz, rows, d), F32)]
    if mod2 is not None:
        h_cols, h_dtype = (d // 2, jnp.uint32) if pack_h else (d, BF16)
        out_specs.append(pl.BlockSpec((None, tr, h_cols), lambda b, j: (b, j, 0)))
        out_shape.append(jax.ShapeDtypeStruct((bsz, rows, h_cols), h_dtype))
    if router is not None:
        rpad = jnp.zeros((d, LANES), F32).at[:, :n_exp].set(router)
        in_specs.append(pl.BlockSpec((d, LANES), lambda b, j: (0, 0)))
        args.append(rpad)
        out_specs.append(pl.BlockSpec((None, tr, LANES), lambda b, j: (b, j, 0)))
        out_shape.append(jax.ShapeDtypeStruct((bsz, rows, LANES), F32))
    k_shift, k_scale = mod2 if mod2 is not None else (0, 0)
    kern = functools.partial(_res_kernel, alpha=alpha, row_off_blk=x_off // tr, nctx_blk=lc // tr,
                             ctx_row=ctx_row, k_gate=k_gate, k_shift=k_shift, k_scale=k_scale,
                             with_mod=mod2 is not None, n_exp=n_exp)
    return pl.pallas_call(
        kern,
        grid=(bsz, rows // tr),
        in_specs=in_specs,
        out_specs=out_specs,
        out_shape=out_shape,
        compiler_params=_cparams(("parallel", "parallel")),
        name="residual_ln",
    )(*args)


def _route_plan(route, n_exp, tile):
    n = route.shape[0]
    e_flat = jnp.concatenate([route[:, _R_E1], route[:, _R_E2]]).astype(jnp.int32)
    experts = jnp.arange(n_exp, dtype=jnp.int32)
    onehot = (e_flat[:, None] == experts[None, :]).astype(jnp.int32)
    csum = jnp.cumsum(onehot, axis=0)
    rank = jnp.sum((csum - onehot) * onehot, axis=1)
    tiles_e = (csum[-1] + tile - 1) // tile
    tile_end = jnp.cumsum(tiles_e)
    tile_start = tile_end - tiles_e
    pos = jnp.sum(onehot * tile_start[None, :], axis=1) * tile + rank
    n_tiles = (2 * n) // tile + n_exp
    tidx = jnp.arange(n_tiles, dtype=jnp.int32)
    valid = tidx < tile_end[-1]
    last_e = jnp.max(jnp.where(tiles_e > 0, experts, 0))
    tile_expert = jnp.sum((tidx[:, None] >= tile_end[None, :]).astype(jnp.int32), axis=1)
    tile_expert = jnp.where(valid, tile_expert, last_e)
    token = jnp.arange(2 * n, dtype=jnp.int32) % n
    src_token = jnp.zeros((n_tiles * tile,), jnp.int32).at[pos].set(token)
    return src_token, pos, tile_expert, valid.astype(jnp.int32)


def _gather_kernel(idx_ref, src_ref, out_ref, sem):
    rows = out_ref.shape[0]

    def row_copy(j, src_row):
        return pltpu.make_async_copy(src_ref.at[pl.ds(src_row, 1)], out_ref.at[pl.ds(j, 1)], sem)

    def start(j, carry):
        row_copy(j, idx_ref[0, j]).start()
        return carry

    def wait(j, carry):
        row_copy(j, 0).wait()
        return carry

    lax.fori_loop(0, rows, start, 0, unroll=8)
    lax.fori_loop(0, rows, wait, 0, unroll=8)


def _gather_rows(src, idx, *, rows):
    n_out = idx.shape[0]
    d = src.shape[1]
    rows = _pick(n_out, rows, SUBLANES)
    return pl.pallas_call(
        _gather_kernel,
        grid=(n_out // rows,),
        in_specs=[pl.BlockSpec((None, 1, rows), lambda i: (i, 0, 0), memory_space=pltpu.SMEM),
                  pl.BlockSpec(memory_space=pl.ANY)],
        out_specs=pl.BlockSpec((rows, d), lambda i: (i, 0)),
        out_shape=jax.ShapeDtypeStruct((n_out, d), src.dtype),
        scratch_shapes=[pltpu.SemaphoreType.DMA(())],
        compiler_params=_cparams(("arbitrary",)),
        name="row_gather",
    )(idx.reshape(n_out // rows, 1, rows), src)


def _moe_glu_kernel(te_ref, tv_ref, a_ref, w1_ref, w3_ref, o_ref):
    i = pl.program_id(1)

    @pl.when(tv_ref[i] != 0)
    def _():
        a = _unpack_pairs(a_ref[...], min(PACK_GROUP, 2 * a_ref.shape[1])).astype(BF16)
        p1 = _dot(a, w1_ref[...])
        p3 = _dot(a, w3_ref[...])
        o_ref[...] = (p1 * _sigmoid(p1) * p3).astype(o_ref.dtype)

    @pl.when(tv_ref[i] == 0)
    def _():
        o_ref[...] = jnp.zeros_like(o_ref)


def _moe_down_kernel(te_ref, tv_ref, h_ref, w2_ref, o_ref):
    i = pl.program_id(1)

    @pl.when(tv_ref[i] != 0)
    def _():
        o_ref[...] = _pack_pairs(_dot(h_ref[...], w2_ref[...]), min(PACK_GROUP, 2 * o_ref.shape[1]))

    @pl.when(tv_ref[i] == 0)
    def _():
        o_ref[...] = jnp.zeros_like(o_ref)


def _moe_experts(xs, w1, w3, w2, tile_expert, tile_valid, *, tile, tn_up, tn_down):
    r, d_half = xs.shape
    d = 2 * d_half
    _, _, f = w1.shape
    n_tiles = r // tile
    tn_up, tn_down = _pick(f, tn_up, LANES), _pick(d, tn_down, min(PACK_GROUP, d))
    up_spec = pl.BlockSpec((None, d, tn_up), lambda j, i, te, tv: (te[i], 0, j))
    hid = pl.pallas_call(
        _moe_glu_kernel,
        grid_spec=pltpu.PrefetchScalarGridSpec(
            num_scalar_prefetch=2, grid=(f // tn_up, n_tiles),
            in_specs=[pl.BlockSpec((tile, d_half), lambda j, i, te, tv: (i, 0)), up_spec, up_spec],
            out_specs=pl.BlockSpec((tile, tn_up), lambda j, i, te, tv: (i, j))),
        out_shape=jax.ShapeDtypeStruct((r, f), BF16),
        compiler_params=_cparams(("parallel", "arbitrary")),
        name="moe_glu",
    )(tile_expert, tile_valid, xs, w1, w3)
    return pl.pallas_call(
        _moe_down_kernel,
        grid_spec=pltpu.PrefetchScalarGridSpec(
            num_scalar_prefetch=2, grid=(d // tn_down, n_tiles),
            in_specs=[pl.BlockSpec((tile, f), lambda j, i, te, tv: (i, 0)),
                      pl.BlockSpec((None, f, tn_down), lambda j, i, te, tv: (te[i], 0, j))],
            out_specs=pl.BlockSpec((tile, tn_down // 2), lambda j, i, te, tv: (i, j))),
        out_shape=jax.ShapeDtypeStruct((r, d_half), jnp.uint32),
        compiler_params=_cparams(("parallel", "arbitrary")),
        name="moe_down",
    )(tile_expert, tile_valid, hid, w2)


def _moe_out_kernel(x_ref, y1_ref, y2_ref, route_ref, mod_ref, gb_ref, o_ref, *, alpha, k_gate):
    b = pl.program_id(0)
    d = x_ref.shape[-1]
    route = route_ref[...]
    group = min(PACK_GROUP, d)
    y = (route[:, _R_G1:_R_G1 + 1] * _unpack_pairs(y1_ref[...], group)
         + route[:, _R_G2:_R_G2 + 1] * _unpack_pairs(y2_ref[...], group))
    gate = _mod_row(mod_ref, b, k_gate, d)
    o_ref[...] = _layer_norm_rows(alpha * x_ref[...] + gate * y) * gb_ref[0:1, :] + gb_ref[1:2, :]


def _moe_residual_ln(x, yg, route, mod, gain, bias, *, alpha, tr, k_gate):
    bsz, rows, d = x.shape
    row_spec = pl.BlockSpec((None, tr, d), lambda b, j: (b, j, 0))
    return pl.pallas_call(
        functools.partial(_moe_out_kernel, alpha=alpha, k_gate=k_gate),
        grid=(bsz, rows // tr),
        in_specs=[row_spec,
                  pl.BlockSpec((None, None, tr, d // 2), lambda b, j: (0, b, j, 0)),
                  pl.BlockSpec((None, None, tr, d // 2), lambda b, j: (1, b, j, 0)),
                  pl.BlockSpec((None, tr, LANES), lambda b, j: (b, j, 0)),
                  pl.BlockSpec(mod.shape, lambda b, j: (0, 0)),
                  pl.BlockSpec((2, d), lambda b, j: (0, 0))],
        out_specs=row_spec,
        out_shape=jax.ShapeDtypeStruct((bsz, rows, d), F32),
        compiler_params=_cparams(("parallel", "parallel")),
        name="moe_residual_ln",
    )(x, yg, yg, route, mod, jnp.stack([gain, bias]))


def _pad_rows(w, rows):
    return jnp.zeros((rows,) + w.shape[1:], w.dtype).at[:w.shape[0]].set(w)


def _layer_layout(p, pool_w, rw):
    lr_raw = p["w2_f"].shape[0]
    gl_raw = p["g_up"].shape[0]
    vr_raw = p["v2"].shape[0] if "v2" in p else 0
    lr = -(-max(lr_raw, vr_raw, 1) // LANES) * LANES
    gl = -(-gl_raw // LANES) * LANES
    wl = 5 * lr + gl
    wl_pad = -(-wl // 1024) * 1024 if wl > 512 else wl
    core = pool_w + 3 * rw
    sizes = [lr_raw] * 4 + [gl_raw] + ([vr_raw] if vr_raw else [])
    slots = [lr] * 4 + [gl] + [lr]
    d = p["w_in"].shape[0]

    def relayout(src, rows_shape):
        dst = jnp.zeros(rows_shape + (core + wl_pad,), src.dtype)
        dst = dst.at[..., :core].set(src[..., :core])
        s_off, d_off = core, core
        for size, slot in zip(sizes, slots):
            dst = dst.at[..., d_off:d_off + size].set(src[..., s_off:s_off + size])
            s_off += size
            d_off += slot
        return dst

    w_in = relayout(p["w_in"].astype(BF16), (d,))
    zero_pool = jnp.zeros((pool_w,), F32)
    mu_p = relayout(jnp.concatenate([zero_pool, p["mu_prev"]]), ())
    mu_n = relayout(jnp.concatenate([zero_pool, p["mu_next"]]), ())
    mu_lora = jnp.zeros((SUBLANES, wl_pad), F32).at[0].set(mu_p[core:]).at[1].set(mu_n[core:])
    seg = lambda a, i: a[pool_w + i * rw:pool_w + (i + 1) * rw]
    zeros = jnp.zeros((rw,), F32)
    vec = jnp.stack([seg(mu_p, 0), seg(mu_n, 0), seg(mu_p, 1), seg(mu_n, 1), seg(mu_p, 2), seg(mu_n, 2),
                     p["w0_f"], p["w0_b"], p["a0_f"], p["a0_b"], p.get("v0", zeros),
                     p["k_k"], p["k_a"], p["r_k"].reshape(-1), p["lnx_g"], p["lnx_b"]])
    lw = {n: _pad_rows(p[n], lr).astype(BF16) for n in ("w2_f", "w2_b", "a2_f", "a2_b")}
    lw["g_up"] = _pad_rows(p["g_up"], gl).astype(BF16)
    if vr_raw:
        lw["v2"] = _pad_rows(p["v2"], lr).astype(BF16)
    return dict(w_in=w_in, mu_lora=mu_lora, vec=vec, lw=lw, lr=lr, gl=gl, wl=wl_pad, core=core)


def _mixer(x_all, h, mod, p, lay, v_first, pool_consts, *, lc, tr, tc, grid_w, tb, ctx_row):
    bsz, t, d = x_all.shape
    mix_w = p["w_out"].shape[0]
    pool_w = mix_w // 4
    pg = pool_w // len(POOL_WINDOWS)
    rw = mix_w - pool_w
    if h is None:
        h = _ln_mod(x_all, mod, lc=lc, tr=tr, ctx_row=ctx_row, k_shift=0, k_scale=1)
    u = _matmul(h.reshape(bsz * t, d), lay["w_in"], out_dtype=BF16, tm=1024, tn=1024, tk=d,
                name="in_proj").reshape(bsz, t, -1)
    pool_wts = p["pool_w"].astype(BF16)
    pool_scale = p["pool_scale"].reshape(1, pool_w)
    pools = [_pool_group(u, pool_consts, pool_wts, pool_scale, i, lc=lc, grid_w=grid_w, tb=tb, pg=pg)
             for i in range(len(POOL_WINDOWS))]
    act = _lora_act(u, lay["mu_lora"], lc=lc, tr=tr, col0=lay["core"], wl=lay["wl"], lr=lay["lr"],
                    gl=lay["gl"])
    tm = _rwkv_terms(u, act, lay["vec"], lay["lw"], v_first, lc=lc, tr=tr, tc=tc, pool_w=pool_w,
                     rw=rw, lr=lay["lr"], gl=lay["gl"])
    y_f, y_b = _wkv_scan(tm, lc=lc, chunk=CHUNK, pairs=_pick(rw // LANES, SCAN_PAIRS, 1))
    out = _rwkv_out(y_f, y_b, tm["bonus"], tm["gate"], lay["vec"], tr=tr, tc=tc)
    mix_in = jnp.concatenate(pools + [out], axis=-1)
    mix = _matmul(mix_in.reshape(bsz * t, mix_w), p["w_out"].astype(BF16), out_dtype=BF16,
                  tm=1024, tn=1024, tk=mix_w, name="out_proj").reshape(bsz, t, d)
    return mix, tm["v"]


def _forward(x, c, ctx, c_ctx, layers, grid_w):
    bsz, seq, d = x.shape
    lc = ctx.shape[1]
    depth = len(layers)
    alpha = (2 * depth) ** 0.25
    assert bsz < COND_ROWS and lc % CHUNK == 0 and seq % CHUNK == 0 and seq % grid_w == 0
    tr = _pick(math.gcd(lc, seq), 256, SUBLANES)
    tb = tr if tr % grid_w == 0 else grid_w
    assert seq % tb == 0 and tb % grid_w == 0
    ctx_row = bsz
    cond = jnp.zeros((COND_ROWS, d), F32).at[:bsz].set(c).at[ctx_row].set(c_ctx)
    x_all = jnp.concatenate([ctx, x], axis=1)
    pool_consts = _pool_consts(lc, seq, grid_w, tb)
    common = dict(tr=tr, lc=lc, ctx_row=ctx_row)
    mods = [_modulation(cond, p["w_ada"], p["b_ada"]) for p in layers]
    v_first = None
    h_next = None
    for i, p in enumerate(layers):
        last = i == depth - 1
        mix_w = p["w_out"].shape[0]
        pool_w = mix_w // 4
        rw = mix_w - pool_w
        assert pool_w % (len(POOL_WINDOWS) * LANES) == 0 and rw % LANES == 0
        tc = _pick(rw, 512, LANES)
        lay = _layer_layout(p, pool_w, rw)
        mod = mods[i]
        mix, v_cur = _mixer(x_all, h_next, mod, p, lay, v_first, pool_consts, tc=tc, grid_w=grid_w,
                            tb=tb, **common)
        if v_first is None:
            v_first = v_cur
        if not last:
            x1, h2 = _residual_ln(x_all, mix, mod, p["ln1_g"], p["ln1_b"], alpha=alpha, k_gate=2,
                                  mod2=(3, 4), **common)
            t = x_all.shape[1]
            hid = _glu(h2.reshape(bsz * t, d), p["ffn_w1"].astype(BF16), p["ffn_w3"].astype(BF16),
                       tm=1024, tn=256, name="ffn_glu")
            ffn = _matmul(hid, p["ffn_w2"].astype(BF16), out_dtype=BF16, tm=512, tn=1024, tk=5504,
                          name="ffn_down").reshape(bsz, t, d)
            mod_pair = jnp.concatenate([mod, mods[i + 1]], axis=1)
            x_all, h_next = _residual_ln(x1, ffn, mod_pair, p["ln2_g"], p["ln2_b"], alpha=alpha,
                                         k_gate=5, mod2=(N_ADA, N_ADA + 1), **common)
        else:
            x1, h2, route = _residual_ln(x_all, mix, mod, p["ln1_g"], p["ln1_b"], alpha=alpha, k_gate=2,
                                         mod2=(3, 4), router=p["router"], pack_h=True, x_off=lc,
                                         y_off=lc, rows=seq, **common)
            n_exp = p["router"].shape[1]
            ntok = bsz * seq
            tile = _pick(2 * ntok, 512, SUBLANES)
            src_token, pos, tile_expert, tile_valid = _route_plan(route.reshape(ntok, LANES), n_exp, tile)
            xs = _gather_rows(h2.reshape(ntok, d // 2), src_token, rows=256)
            ys = _moe_experts(xs, p["exp_w1"].astype(BF16), p["exp_w3"].astype(BF16),
                              p["exp_w2"].astype(BF16), tile_expert, tile_valid, tile=tile,
                              tn_up=512, tn_down=1024)
            yg = _gather_rows(ys, pos, rows=256).reshape(2, bsz, seq, d // 2)
            return _moe_residual_ln(x1, yg, route, mod, p["ln2_g"], p["ln2_b"], alpha=alpha, tr=tr,
                                    k_gate=5)
    return x_all[:, lc:]


_LAYER0 = ("w_ada", "b_ada", "w_in", "mu_prev", "mu_next", "pool_w", "pool_scale", "w0_f", "w2_f",
           "w0_b", "w2_b", "a0_f", "a2_f", "a0_b", "a2_b", "g_up", "k_k", "k_a", "r_k", "lnx_g",
           "lnx_b", "w_out", "ln1_g", "ln1_b", "ln2_g", "ln2_b", "ffn_w1", "ffn_w3", "ffn_w2")
_LAYER1 = _LAYER0[:26] + ("v0", "v2", "router", "exp_w1", "exp_w3", "exp_w2")


def kernel(x, c, ctx, c_ctx, l0_w_ada, l0_b_ada, l0_w_in, l0_mu_prev, l0_mu_next, l0_pool_w, l0_pool_scale, l0_w0_f, l0_w2_f, l0_w0_b, l0_w2_b, l0_a0_f, l0_a2_f, l0_a0_b, l0_a2_b, l0_g_up, l0_k_k, l0_k_a, l0_r_k, l0_lnx_g, l0_lnx_b, l0_w_out, l0_ln1_g, l0_ln1_b, l0_ln2_g, l0_ln2_b, l0_ffn_w1, l0_ffn_w3, l0_ffn_w2, l1_w_ada, l1_b_ada, l1_w_in, l1_mu_prev, l1_mu_next, l1_pool_w, l1_pool_scale, l1_w0_f, l1_w2_f, l1_w0_b, l1_w2_b, l1_a0_f, l1_a2_f, l1_a0_b, l1_a2_b, l1_g_up, l1_k_k, l1_k_a, l1_r_k, l1_lnx_g, l1_lnx_b, l1_w_out, l1_ln1_g, l1_ln1_b, l1_ln2_g, l1_ln2_b, l1_v0, l1_v2, l1_router, l1_exp_w1, l1_exp_w3, l1_exp_w2):
    l0 = dict(zip(_LAYER0, (l0_w_ada, l0_b_ada, l0_w_in, l0_mu_prev, l0_mu_next, l0_pool_w, l0_pool_scale, l0_w0_f, l0_w2_f, l0_w0_b, l0_w2_b, l0_a0_f, l0_a2_f, l0_a0_b, l0_a2_b, l0_g_up, l0_k_k, l0_k_a, l0_r_k, l0_lnx_g, l0_lnx_b, l0_w_out, l0_ln1_g, l0_ln1_b, l0_ln2_g, l0_ln2_b, l0_ffn_w1, l0_ffn_w3, l0_ffn_w2)))
    l1 = dict(zip(_LAYER1, (l1_w_ada, l1_b_ada, l1_w_in, l1_mu_prev, l1_mu_next, l1_pool_w, l1_pool_scale, l1_w0_f, l1_w2_f, l1_w0_b, l1_w2_b, l1_a0_f, l1_a2_f, l1_a0_b, l1_a2_b, l1_g_up, l1_k_k, l1_k_a, l1_r_k, l1_lnx_g, l1_lnx_b, l1_w_out, l1_ln1_g, l1_ln1_b, l1_ln2_g, l1_ln2_b, l1_v0, l1_v2, l1_router, l1_exp_w1, l1_exp_w3, l1_exp_w2)))
    return _forward(x, c, ctx, c_ctx, [l0, l1], GRID_W)
```

```python
import functools
import math

import numpy as np
import jax
import jax.numpy as jnp
from jax import lax
from jax.experimental import pallas as pl
from jax.experimental.pallas import tpu as pltpu

F32 = jnp.float32
BF16 = jnp.bfloat16

GRID_W = 64
POOL_WINDOWS = (2, 4, 8, 16)
HEAD = 64
LANES = 128
SUBLANES = 8
HALO = 16
N_ADA = 6
LN_EPS = 1e-5
GN_EPS = 64e-5
NORM_EPS = 1e-12
CHUNK = 64
SCAN_PAIRS = 8
SCAN_SUB = 4
INV_PASSES = 1
VMEM_LIMIT = 56 * 1024 * 1024
COND_ROWS = 16
PACK_GROUP = 1024


def _cparams(sem):
    return pltpu.CompilerParams(dimension_semantics=sem, vmem_limit_bytes=VMEM_LIMIT)


def _pick(n, target, mult):
    best = None
    for d in range(mult, min(n, target) + 1, mult):
        if n % d == 0:
            best = d
    return best if best is not None else n


def _dot(a, b):
    return jnp.dot(a, b, preferred_element_type=F32)


def _dot_nt(a, b):
    return lax.dot_general(a, b, (((1,), (1,)), ((), ())), preferred_element_type=F32)


def _split2(x):
    hi = x.astype(BF16)
    lo = (x - hi.astype(F32)).astype(BF16)
    return hi, lo


def _mm(a, b, passes=1):
    if passes == 1:
        return _dot(a.astype(BF16), b.astype(BF16))
    a_hi, a_lo = _split2(a)
    b_hi, b_lo = _split2(b)
    return _dot(a_hi, b_hi) + _dot(a_lo, b_hi) + _dot(a_hi, b_lo)


def _sigmoid(x):
    return 1.0 / (1.0 + jnp.exp(-x))


def _pack_pairs(x, group):
    half = group // 2
    words = []
    for g in range(x.shape[1] // group):
        lo = lax.bitcast_convert_type(x[:, g * group:g * group + half].astype(BF16).astype(F32), jnp.uint32)
        hi = lax.bitcast_convert_type(x[:, g * group + half:(g + 1) * group].astype(BF16).astype(F32), jnp.uint32)
        words.append((lo >> 16) | hi)
    return words[0] if len(words) == 1 else jnp.concatenate(words, axis=1)


def _unpack_pairs(p, group):
    half = group // 2
    cols = []
    for g in range(p.shape[1] // half):
        word = p[:, g * half:(g + 1) * half]
        cols.append(lax.bitcast_convert_type(word << 16, F32))
        cols.append(lax.bitcast_convert_type(word & jnp.uint32(0xFFFF0000), F32))
    return jnp.concatenate(cols, axis=1)


def _layer_norm_rows(x):
    mu = jnp.mean(x, axis=-1, keepdims=True)
    xc = x - mu
    var = jnp.mean(xc * xc, axis=-1, keepdims=True)
    return xc * lax.rsqrt(var + LN_EPS)


def _mod_kernel(c_ref, w_ref, b_ref, o_ref):
    c = c_ref[...]
    s = c * _sigmoid(c)
    o_ref[...] = _mm(s, w_ref[...], passes=3) + b_ref[...]


def _modulation(cond, w_ada, b_ada):
    d, n = w_ada.shape
    tn = _pick(n, 512, LANES)
    return pl.pallas_call(
        _mod_kernel,
        grid=(n // tn,),
        in_specs=[pl.BlockSpec((COND_ROWS, d), lambda j: (0, 0)),
                  pl.BlockSpec((d, tn), lambda j: (0, j)),
                  pl.BlockSpec((1, tn), lambda j: (0, j))],
        out_specs=pl.BlockSpec((COND_ROWS, tn), lambda j: (0, j)),
        out_shape=jax.ShapeDtypeStruct((COND_ROWS, n), F32),
        compiler_params=_cparams(("parallel",)),
        name="modulation",
    )(cond, w_ada, b_ada.reshape(1, n))


def _mod_row(mod_ref, row, chunk, d):
    return mod_ref[pl.ds(row, 1), chunk * d:(chunk + 1) * d]


def _ln_mod_kernel(x_ref, mod_ref, h_ref, *, nctx_blk, ctx_row, k_shift, k_scale):
    b, j = pl.program_id(0), pl.program_id(1)
    d = x_ref.shape[-1]
    row = jnp.where(j < nctx_blk, ctx_row, b)
    shift = _mod_row(mod_ref, row, k_shift, d)
    scale = _mod_row(mod_ref, row, k_scale, d)
    h_ref[...] = (_layer_norm_rows(x_ref[...]) * (1.0 + scale) + shift).astype(h_ref.dtype)


def _ln_mod(x_all, mod, *, lc, tr, ctx_row, k_shift, k_scale):
    bsz, t, d = x_all.shape
    kern = functools.partial(_ln_mod_kernel, nctx_blk=lc // tr, ctx_row=ctx_row,
                             k_shift=k_shift, k_scale=k_scale)
    return pl.pallas_call(
        kern,
        grid=(bsz, t // tr),
        in_specs=[pl.BlockSpec((None, tr, d), lambda b, j: (b, j, 0)),
                  pl.BlockSpec(mod.shape, lambda b, j: (0, 0))],
        out_specs=pl.BlockSpec((None, tr, d), lambda b, j: (b, j, 0)),
        out_shape=jax.ShapeDtypeStruct((bsz, t, d), BF16),
        compiler_params=_cparams(("parallel", "parallel")),
        name="ln_modulate",
    )(x_all, mod)


def _mm_kernel(a_ref, w_ref, o_ref, *scratch, nk):
    if nk == 1:
        o_ref[...] = _dot(a_ref[...], w_ref[...]).astype(o_ref.dtype)
        return
    acc_ref, = scratch
    k = pl.program_id(2)

    @pl.when(k == 0)
    def _():
        acc_ref[...] = jnp.zeros_like(acc_ref)

    acc_ref[...] += _dot(a_ref[...], w_ref[...])

    @pl.when(k == nk - 1)
    def _():
        o_ref[...] = acc_ref[...].astype(o_ref.dtype)


def _matmul(a, w, *, out_dtype, tm, tn, tk, name):
    m, kdim = a.shape
    n = w.shape[1]
    tm, tn, tk = _pick(m, tm, SUBLANES), _pick(n, tn, LANES), _pick(kdim, tk, LANES)
    nk = kdim // tk
    scratch = [pltpu.VMEM((tm, tn), F32)] if nk > 1 else []
    return pl.pallas_call(
        functools.partial(_mm_kernel, nk=nk),
        grid=(m // tm, n // tn, nk),
        in_specs=[pl.BlockSpec((tm, tk), lambda i, j, k: (i, k)),
                  pl.BlockSpec((tk, tn), lambda i, j, k: (k, j))],
        out_specs=pl.BlockSpec((tm, tn), lambda i, j, k: (i, j)),
        out_shape=jax.ShapeDtypeStruct((m, n), out_dtype),
        scratch_shapes=scratch,
        compiler_params=_cparams(("parallel", "parallel", "arbitrary")),
        name=name,
    )(a, w)


def _glu_kernel(a_ref, w1_ref, w3_ref, o_ref):
    a = a_ref[...]
    p1 = _dot(a, w1_ref[...])
    p3 = _dot(a, w3_ref[...])
    o_ref[...] = (p1 * _sigmoid(p1) * p3).astype(o_ref.dtype)


def _glu(a, w1, w3, *, tm, tn, name):
    m, kdim = a.shape
    f = w1.shape[1]
    tm, tn = _pick(m, tm, SUBLANES), _pick(f, tn, LANES)
    w_spec = pl.BlockSpec((kdim, tn), lambda i, j: (0, j))
    return pl.pallas_call(
        _glu_kernel,
        grid=(m // tm, f // tn),
        in_specs=[pl.BlockSpec((tm, kdim), lambda i, j: (i, 0)), w_spec, w_spec],
        out_specs=pl.BlockSpec((tm, tn), lambda i, j: (i, j)),
        out_shape=jax.ShapeDtypeStruct((m, f), BF16),
        compiler_params=_cparams(("parallel", "parallel")),
        name=name,
    )(a, w1, w3)


def _pool_consts(lc, tl, grid_w, tb):
    rows = tl // grid_w
    cb, c1, inv = [], [], []
    for win in POOL_WINDOWS:
        half = win // 2
        t = np.arange(tb)
        same_row = (t[:, None] // grid_w) == (t[None, :] // grid_w)
        dc = (t[None, :] % grid_w) - (t[:, None] % grid_w)
        cb.append((same_row & (dc >= -half) & (dc < half)).astype(np.float32))
        tc = np.arange(lc)
        d1 = tc[None, :] - tc[:, None]
        c1.append(((d1 >= -half) & (d1 < half)).astype(np.float32))
        cnt1 = np.minimum(tc + half, lc) - np.maximum(tc - half, 0)
        g = np.arange(grid_w)
        cntc = np.minimum(g + half, grid_w) - np.maximum(g - half, 0)
        r = np.arange(rows)
        cntr = np.minimum(r + half, rows) - np.maximum(r - half, 0)
        cnt2 = (cntr[:, None] * cntc[None, :]).reshape(-1)
        iv = 1.0 / np.concatenate([cnt1, cnt2]).astype(np.float64)
        inv.append(np.broadcast_to(iv[:, None], (lc + tl, LANES)).astype(np.float32))
    return (jnp.asarray(np.stack(cb), BF16), jnp.asarray(np.stack(c1), BF16),
            jnp.asarray(np.stack(inv), F32))


def _pool_kernel(u_ref, cb_ref, c1_ref, inv_ref, pw_ref, ps_ref, o_ref, s1_ref, acc_ref,
                 *, lc, tb, grid_w, half):
    t = u_ref.shape[0]
    tl = t - lc
    acc_ref[0:lc, :] = _dot(c1_ref[...], u_ref[0:lc, :])
    cb = cb_ref[...]
    for blk in range(tl // tb):
        lo, hi = lc + blk * tb, lc + (blk + 1) * tb
        s1_ref[lo:hi, :] = _dot(cb, u_ref[lo:hi, :])
    acc_ref[lc:t, :] = s1_ref[lc:t, :]
    for dr in range(-half, half):
        if dr == 0:
            continue
        sh = abs(dr) * grid_w
        if sh >= tl:
            continue
        if dr > 0:
            acc_ref[lc:t - sh, :] += s1_ref[lc + sh:t, :]
        else:
            acc_ref[lc + sh:t, :] += s1_ref[lc:t - sh, :]
    n_rep = u_ref.shape[1] // LANES
    diff = acc_ref[...] * jnp.concatenate([inv_ref[...]] * n_rep, axis=1) - u_ref[...].astype(F32)
    o_ref[...] = (_dot(diff.astype(BF16), pw_ref[...]) * ps_ref[...]).astype(o_ref.dtype)


def _pool_group(u_all, consts, pool_w, pool_scale, idx, *, lc, grid_w, tb, pg):
    bsz, t, _ = u_all.shape
    cb, c1, inv = consts
    half = POOL_WINDOWS[idx] // 2
    kern = functools.partial(_pool_kernel, lc=lc, tb=tb, grid_w=grid_w, half=half)
    return pl.pallas_call(
        kern,
        grid=(bsz,),
        in_specs=[pl.BlockSpec((None, t, pg), lambda b: (b, 0, idx)),
                  pl.BlockSpec((None, tb, tb), lambda b: (idx, 0, 0)),
                  pl.BlockSpec((None, lc, lc), lambda b: (idx, 0, 0)),
                  pl.BlockSpec((None, t, LANES), lambda b: (idx, 0, 0)),
                  pl.BlockSpec((None, pg, pg), lambda b: (idx, 0, 0)),
                  pl.BlockSpec((1, pg), lambda b: (0, idx))],
        out_specs=pl.BlockSpec((None, t, pg), lambda b: (b, 0, 0)),
        out_shape=jax.ShapeDtypeStruct((bsz, t, pg), BF16),
        scratch_shapes=[pltpu.VMEM((t, pg), F32), pltpu.VMEM((t, pg), F32)],
        compiler_params=_cparams(("parallel",)),
        name="pool_mixer_%d" % idx,
    )(u_all, cb, c1, inv, pool_w, pool_scale)


def _token_shift(f_ref, prev_ref, next_ref, mu_p, mu_n, seg_first, seg_last, shift_ref=None):
    f = f_ref[...].astype(F32)
    tr = f.shape[0]
    row = lax.broadcasted_iota(jnp.int32, f.shape, 0)
    prev_row = jnp.where(seg_first, 0.0, prev_ref[...].astype(F32)[HALO - 1:HALO, :])
    next_row = jnp.where(seg_last, 0.0, next_ref[...].astype(F32)[0:1, :])
    if shift_ref is None:
        prev = jnp.where(row == 0, prev_row, pltpu.roll(f, 1, 0))
        nxt = jnp.where(row == tr - 1, next_row, pltpu.roll(f, tr - 1, 0))
    else:
        moved = _dot(shift_ref[...], f_ref[...])
        prev = jnp.where(row == 0, prev_row, moved[:tr])
        nxt = jnp.where(row == tr - 1, next_row, moved[tr:])
    return f + mu_p * (prev - f) + mu_n * (nxt - f)


def _shift_matrix(tr):
    t = np.arange(tr)
    prev = (t[:, None] - 1 == t[None, :]).astype(np.float32)
    nxt = (t[:, None] + 1 == t[None, :]).astype(np.float32)
    return jnp.asarray(np.concatenate([prev, nxt], axis=0), BF16)


def _segment_flags(j, nctx_blk, n_blk):
    seg_first = jnp.logical_or(j == 0, j == nctx_blk)
    seg_last = jnp.logical_or(j == nctx_blk - 1, j == n_blk - 1)
    return seg_first, seg_last


def _shift_specs(tr, tc, col_blk, t):
    per_blk = tr // HALO
    last = t // HALO - 1
    cur = pl.BlockSpec((None, tr, tc), lambda b, j, c: (b, j, col_blk(c)))
    prv = pl.BlockSpec((None, HALO, tc),
                       lambda b, j, c: (b, jnp.maximum(j * per_blk - 1, 0), col_blk(c)))
    nxt = pl.BlockSpec((None, HALO, tc),
                       lambda b, j, c: (b, jnp.minimum((j + 1) * per_blk, last), col_blk(c)))
    return [cur, prv, nxt]


def _lora_kernel(u_ref, up_ref, un_ref, mu_ref, o_ref, *, nctx_blk, n_blk, lr, gl):
    j = pl.program_id(1)
    seg_first, seg_last = _segment_flags(j, nctx_blk, n_blk)
    z = _token_shift(u_ref, up_ref, un_ref, mu_ref[0:1, :], mu_ref[1:2, :], seg_first, seg_last)
    col = lax.broadcasted_iota(jnp.int32, z.shape, 1)
    act = jnp.where(col < 2 * lr, jnp.tanh(z),
                    jnp.where(jnp.logical_and(col >= 4 * lr, col < 4 * lr + gl), _sigmoid(z), z))
    o_ref[...] = act.astype(o_ref.dtype)


def _lora_act(u_all, mu_lora, *, lc, tr, col0, wl, lr, gl):
    bsz, t, _ = u_all.shape
    specs = _shift_specs(tr, wl, lambda c: col0 // wl, t)
    kern = functools.partial(_lora_kernel, nctx_blk=lc // tr, n_blk=t // tr, lr=lr, gl=gl)
    return pl.pallas_call(
        kern,
        grid=(bsz, t // tr, 1),
        in_specs=specs + [pl.BlockSpec((SUBLANES, wl), lambda b, j, c: (0, 0))],
        out_specs=pl.BlockSpec((None, tr, wl), lambda b, j, c: (b, j, 0)),
        out_shape=jax.ShapeDtypeStruct((bsz, t, wl), BF16),
        compiler_params=_cparams(("parallel", "parallel", "arbitrary")),
        name="lora_act",
    )(u_all, u_all, u_all, mu_lora)


def _head_ones():
    r = lax.broadcasted_iota(jnp.int32, (LANES, LANES), 0)
    c = lax.broadcasted_iota(jnp.int32, (LANES, LANES), 1)
    return jnp.where((r ^ c) < HEAD, 1.0, 0.0).astype(BF16)


def _head_sum(x, ones, passes=2):
    outs = []
    for s in range(x.shape[1] // LANES):
        part = x[:, s * LANES:(s + 1) * LANES]
        if passes == 1:
            outs.append(_dot(part.astype(BF16), ones))
            continue
        hi, lo = _split2(part)
        outs.append(_dot(hi, ones) + _dot(lo, ones))
    return outs[0] if len(outs) == 1 else jnp.concatenate(outs, axis=1)


(_V_MPR, _V_MNR, _V_MPK, _V_MNK, _V_MPV, _V_MNV, _V_W0F, _V_W0B, _V_A0F, _V_A0B,
 _V_V0, _V_KK, _V_KA, _V_RK, _V_LG, _V_LB) = range(16)
_DECAY_SCALE = math.exp(-0.5)


def _terms_kernel(*refs, nctx_blk, n_blk, lr, gl, has_vres):
    (ur, urp, urn, uk, ukp, ukn, uv, uvp, uvn, act_ref, vec_ref,
     w2f_ref, w2b_ref, a2f_ref, a2b_ref, gup_ref, shift_ref) = refs[:17]
    pos = 17
    if has_vres:
        v2_ref, vfirst_ref = refs[pos:pos + 2]
        pos += 2
    (r_o, v_o, kk_o, lwf_o, lwb_o, kf_o, kb_o, af_o, ab_o, gate_o, bonus_o) = refs[pos:]

    j = pl.program_id(1)
    seg_first, seg_last = _segment_flags(j, nctx_blk, n_blk)
    vec = lambda i: vec_ref[i:i + 1, :]
    r = _token_shift(ur, urp, urn, vec(_V_MPR), vec(_V_MNR), seg_first, seg_last, shift_ref)
    k = _token_shift(uk, ukp, ukn, vec(_V_MPK), vec(_V_MNK), seg_first, seg_last, shift_ref)
    v = _token_shift(uv, uvp, uvn, vec(_V_MPV), vec(_V_MNV), seg_first, seg_last, shift_ref)

    act = act_ref[...]
    a_wf, a_wb = act[:, 0:lr], act[:, lr:2 * lr]
    a_af, a_ab = act[:, 2 * lr:3 * lr], act[:, 3 * lr:4 * lr]
    a_g = act[:, 4 * lr:4 * lr + gl]
    if has_vres:
        a_v = act[:, 4 * lr + gl:5 * lr + gl]
        v = v + (vfirst_ref[...].astype(F32) - v) * _sigmoid(vec(_V_V0) + _dot(a_v, v2_ref[...]))

    lwf_o[...] = -_DECAY_SCALE * _sigmoid(vec(_V_W0F) + _dot(a_wf, w2f_ref[...]))
    lwb_o[...] = -_DECAY_SCALE * _sigmoid(vec(_V_W0B) + _dot(a_wb, w2b_ref[...]))
    a_f = _sigmoid(vec(_V_A0F) + _dot(a_af, a2f_ref[...]))
    a_b = _sigmoid(vec(_V_A0B) + _dot(a_ab, a2b_ref[...]))

    ones = _head_ones()
    kkr = k * vec(_V_KK)
    kk_o[...] = (kkr * lax.rsqrt(jnp.maximum(_head_sum(kkr * kkr, ones, 1), NORM_EPS * NORM_EPS))
                 ).astype(kk_o.dtype)
    k_f = k * (1.0 + (a_f - 1.0) * vec(_V_KA))
    k_b = k * (1.0 + (a_b - 1.0) * vec(_V_KA))
    bonus_o[...] = (_head_sum(r * (k_f + k_b) * vec(_V_RK), ones) * v).astype(bonus_o.dtype)
    gate_o[...] = _dot(a_g, gup_ref[...]).astype(gate_o.dtype)
    r_o[...] = r.astype(r_o.dtype)
    v_o[...] = v.astype(v_o.dtype)
    kf_o[...] = k_f.astype(kf_o.dtype)
    kb_o[...] = k_b.astype(kb_o.dtype)
    af_o[...] = a_f.astype(af_o.dtype)
    ab_o[...] = a_b.astype(ab_o.dtype)


def _rwkv_terms(u_all, act, vec, lw, v_first, *, lc, tr, tc, pool_w, rw, lr, gl):
    bsz, t, _ = u_all.shape
    has_vres = v_first is not None
    nb = rw // tc
    specs = []
    for part in range(3):
        base = (pool_w + part * rw) // tc
        specs += _shift_specs(tr, tc, (lambda c, base=base: base + c), t)
    specs.append(pl.BlockSpec((None, tr, act.shape[-1]), lambda b, j, c: (b, j, 0)))
    specs.append(pl.BlockSpec((vec.shape[0], tc), lambda b, j, c: (0, c)))
    wspec = lambda rows: pl.BlockSpec((rows, tc), lambda b, j, c: (0, c))
    specs += [wspec(lr), wspec(lr), wspec(lr), wspec(lr), wspec(gl)]
    specs.append(pl.BlockSpec((2 * tr, tr), lambda b, j, c: (0, 0)))
    args = [u_all] * 9 + [act, vec, lw["w2_f"], lw["w2_b"], lw["a2_f"], lw["a2_b"], lw["g_up"],
                          _shift_matrix(tr)]
    if has_vres:
        specs += [wspec(lr), pl.BlockSpec((None, tr, tc), lambda b, j, c: (b, j, c))]
        args += [lw["v2"], v_first]
    out_spec = pl.BlockSpec((None, tr, tc), lambda b, j, c: (b, j, c))
    kern = functools.partial(_terms_kernel, nctx_blk=lc // tr, n_blk=t // tr, lr=lr, gl=gl,
                             has_vres=has_vres)
    names = ("r", "v", "kk", "lw_f", "lw_b", "k_f", "k_b", "a_f", "a_b", "gate", "bonus")
    dtypes = [F32 if n.startswith("lw") else BF16 for n in names]
    outs = pl.pallas_call(
        kern,
        grid=(bsz, t // tr, nb),
        in_specs=specs,
        out_specs=[out_spec] * len(names),
        out_shape=[jax.ShapeDtypeStruct((bsz, t, rw), dt) for dt in dtypes],
        compiler_params=_cparams(("parallel", "parallel", "parallel")),
        name="rwkv_terms",
    )(*args)
    return dict(zip(names, outs))


def _cumsum_rows(x, rev):
    n = x.shape[0]
    row = lax.broadcasted_iota(jnp.int32, x.shape, 0)
    s = 1
    while s < n:
        if rev:
            x = x + jnp.where(row < n - s, pltpu.roll(x, n - s, 0), 0.0)
        else:
            x = x + jnp.where(row >= s, pltpu.roll(x, s, 0), 0.0)
        s *= 2
    return x


def _wkv_chunks(chains):
    n = chains[0][0].shape[0]
    hp = 2 * n
    lane = lax.broadcasted_iota(jnp.int32, (n, LANES), 1)
    first_head = lane < HEAD
    ri = lax.broadcasted_iota(jnp.int32, (hp, hp), 0)
    ci = lax.broadcasted_iota(jnp.int32, (hp, hp), 1)
    blk = ri ^ ci
    eye = ri == ci
    tdiff = (ci & (n - 1)) - (ri & (n - 1))
    dist = {False: jnp.where(blk < n, tdiff, hp), True: jnp.where(blk < n, -tdiff, hp)}

    def stack(x):
        return jnp.concatenate([jnp.where(first_head, x, 0.0), jnp.where(first_head, 0.0, x)], axis=0)

    pre = []
    for r, v, kk, lw, k, a, state, rev in chains:
        c = _cumsum_rows(lw, rev)
        ctot = c[0:1, :] if rev else c[n - 1:n, :]
        e_pos, e_neg = jnp.exp(c), jnp.exp(-c)
        e_prev, e_rem = jnp.exp(c - lw), jnp.exp(ctot - c)
        kka = kk * a
        bt, kt = kka * e_neg, k * e_neg
        pre.append(dict(s_at=stack(-kk * e_prev), s_rt=stack(r * e_pos), s_v=stack(v), rt=r * e_pos,
                        rhs1=jnp.concatenate([bt, bt, kt, kt], axis=0).astype(BF16),
                        lhs_t=jnp.concatenate([stack(kka * e_rem), stack(k * e_rem)], axis=0),
                        wtot=jnp.exp(ctot), strict=dist[rev] < 0, incl=dist[rev] <= 0))

    s1 = [_dot_nt(jnp.concatenate([p["s_at"], p["s_rt"]], axis=0).astype(BF16), p["rhs1"]) for p in pre]
    nmat = [jnp.where(p["strict"], s[:hp, :hp], 0.0) for p, s in zip(pre, s1)]
    m_ak = [jnp.where(p["strict"], s[:hp, hp:], 0.0) for p, s in zip(pre, s1)]
    m_rbk = [jnp.concatenate([jnp.where(p["incl"], s[hp:, :hp], 0.0), jnp.where(p["incl"], s[hp:, hp:], 0.0)],
                             axis=1).astype(BF16) for p, s in zip(pre, s1)]
    z = [_mm(m, p["s_v"]) for m, p in zip(m_ak, pre)]

    n8 = [jnp.where(blk < SUBLANES, m, 0.0) for m in nmat]
    tmat = [jnp.where(eye, 1.0, m) for m in n8]
    n2 = [_mm(m, m, INV_PASSES) for m in n8]
    st = [_mm(jnp.concatenate([t, m], axis=0), m, INV_PASSES) for t, m in zip(tmat, n2)]
    tmat = [t + s[:hp] for t, s in zip(tmat, st)]
    tmat = [t + _mm(t, s[hp:], INV_PASSES) for t, s in zip(tmat, st)]
    size = SUBLANES
    while size < n:
        sel = jnp.logical_and(blk >= size, blk < 2 * size)
        pieces = [[(g * 2 * size + (0 if chain[7] else size), g * 2 * size + (size if chain[7] else 2 * size))
                   for g in range(hp // (2 * size))] for chain in chains]

        def take(mat, rows):
            return jnp.concatenate([mat[a:b] for a, b in rows], axis=0)

        def put(mat, rows, part, keep):
            out, at, prev = [], 0, 0
            for a, b in rows:
                if a > prev:
                    out.append(mat[prev:a] if keep else jnp.zeros((a - prev, mat.shape[1]), F32))
                out.append(part[at:at + b - a])
                at, prev = at + b - a, b
            if prev < mat.shape[0]:
                out.append(mat[prev:] if keep else jnp.zeros((mat.shape[0] - prev, mat.shape[1]), F32))
            return jnp.concatenate(out, axis=0)

        x = [_mm(take(jnp.where(sel, m, 0.0), rows), t, INV_PASSES) for m, t, rows in zip(nmat, tmat, pieces)]
        x = [put(t, rows, xx, keep=False) for t, rows, xx in zip(tmat, pieces, x)]
        th = [take(t, rows) for t, rows in zip(tmat, pieces)]
        th = [h + _mm(h, xx, INV_PASSES) for h, xx in zip(th, x)]
        tmat = [put(t, rows, h, keep=True) for t, rows, h in zip(tmat, pieces, th)]
        size *= 2

    gu = [_mm(t, jnp.concatenate([p["s_at"], zz], axis=1)) for t, p, zz in zip(tmat, pre, z)]
    zeros = jnp.zeros((hp, LANES), F32)
    rhs4 = [jnp.concatenate([g, jnp.concatenate([zeros, p["s_v"]], axis=1)], axis=0).astype(BF16)
            for g, p in zip(gu, pre)]
    tb = [_dot(jnp.concatenate([m, p["lhs_t"].T.astype(BF16)], axis=0), rr)
          for m, p, rr in zip(m_rbk, pre, rhs4)]
    top = [x_[:hp] for x_ in tb]
    bot = [x_[hp:] for x_ in tb]

    outs = []
    for p, tp, bt_, chain in zip(pre, top, bot, chains):
        q = p["rt"] + tp[:n, :LANES] + tp[n:, :LANES]
        amat = jnp.where(eye, jnp.broadcast_to(p["wtot"], (LANES, LANES)), 0.0) + bt_[:, :LANES]
        qa = jnp.concatenate([q, amat], axis=0).astype(BF16)
        s_hi, s_lo = _split2(chain[6])
        outs.append(_dot(qa, s_hi) + _dot(qa, s_lo))
    return [(o[:n] + tp[:n, LANES:] + tp[n:, LANES:], o[n:] + bt_[:, LANES:])
            for o, tp, bt_ in zip(outs, top, bot)]


def _scan_kernel(*refs, pairs, chunk):
    ins, (yf_ref, yb_ref, state_ref) = refs[:12], refs[12:]
    s = pl.program_id(1)
    n_sub = yf_ref.shape[0] // chunk

    @pl.when(s == 0)
    def _():
        state_ref[...] = jnp.zeros_like(state_ref)

    chains = [(d, p, slice(p * LANES, (p + 1) * LANES)) for d in range(2) for p in range(pairs)]

    def sub_step(i, carry):
        rows = [pl.ds(pl.multiple_of((n_sub - 1 - i if d else i) * chunk, chunk), chunk) for d in range(2)]
        loaded = [tuple(ref[rows[d], cols].astype(F32) for ref in ins[6 * d:6 * d + 6])
                  + (state_ref[d, p], bool(d)) for d, p, cols in chains]
        for (d, p, cols), (y, new_state) in zip(chains, _wkv_chunks(loaded)):
            y_ref = yb_ref if d else yf_ref
            y_ref[rows[d], cols] = y.astype(y_ref.dtype)
            state_ref[d, p] = new_state
        return carry

    lax.fori_loop(0, n_sub, sub_step, 0)


def _wkv_scan(tm, *, lc, chunk, pairs):
    bsz, t, rw = tm["r"].shape
    width = pairs * LANES
    ngrp = rw // width
    n_sub = math.gcd(math.gcd(lc // chunk, (t - lc) // chunk), SCAN_SUB)
    step_rows = n_sub * chunk
    nctx, ntot = lc // step_rows, t // step_rows

    def fwd(g, s):
        return (g // ngrp, s, g % ngrp)

    def bwd(g, s):
        return (g // ngrp, jnp.where(s < nctx, nctx - 1 - s, ntot - 1 - (s - nctx)), g % ngrp)

    blk = (None, step_rows, width)
    names_f = ("r", "v", "kk", "lw_f", "k_f", "a_f")
    names_b = ("r", "v", "kk", "lw_b", "k_b", "a_b")
    in_specs = [pl.BlockSpec(blk, fwd)] * 6 + [pl.BlockSpec(blk, bwd)] * 6
    args = [tm[n] for n in names_f] + [tm[n] for n in names_b]
    return pl.pallas_call(
        functools.partial(_scan_kernel, pairs=pairs, chunk=chunk),
        grid=(bsz * ngrp, ntot),
        in_specs=in_specs,
        out_specs=[pl.BlockSpec(blk, fwd), pl.BlockSpec(blk, bwd)],
        out_shape=[jax.ShapeDtypeStruct((bsz, t, rw), BF16)] * 2,
        scratch_shapes=[pltpu.VMEM((2, pairs, LANES, LANES), F32)],
        compiler_params=_cparams(("parallel", "arbitrary")),
        name="wkv7_scan",
    )(*args)


def _rwkv_out_kernel(yf_ref, yb_ref, bonus_ref, gate_ref, vec_ref, o_ref):
    y = yf_ref[...].astype(F32) + yb_ref[...].astype(F32)
    ones = _head_ones()
    inv_n = 1.0 / HEAD
    mu = _head_sum(y, ones, 1) * inv_n
    yc = y - mu
    var = _head_sum(yc * yc, ones, 1) * inv_n
    yn = yc * lax.rsqrt(var + GN_EPS)
    yn = yn * vec_ref[_V_LG:_V_LG + 1, :] + vec_ref[_V_LB:_V_LB + 1, :]
    o_ref[...] = ((yn + bonus_ref[...].astype(F32)) * gate_ref[...].astype(F32)).astype(o_ref.dtype)


def _rwkv_out(y_f, y_b, bonus, gate, vec, *, tr, tc):
    bsz, t, rw = y_f.shape
    spec = pl.BlockSpec((None, tr, tc), lambda b, j, c: (b, j, c))
    return pl.pallas_call(
        _rwkv_out_kernel,
        grid=(bsz, t // tr, rw // tc),
        in_specs=[spec] * 4 + [pl.BlockSpec((vec.shape[0], tc), lambda b, j, c: (0, c))],
        out_specs=spec,
        out_shape=jax.ShapeDtypeStruct((bsz, t, rw), BF16),
        compiler_params=_cparams(("parallel", "parallel", "parallel")),
        name="rwkv_out",
    )(y_f, y_b, bonus, gate, vec)


_R_E1, _R_E2, _R_G1, _R_G2 = range(4)

def _res_kernel(*refs, alpha, row_off_blk, nctx_blk, ctx_row, k_gate, k_shift, k_scale,
                with_mod, n_exp):
    x_ref, y_ref, mod_ref, gb_ref = refs[:4]
    pos = 4
    if n_exp:
        router_ref = refs[pos]
        pos += 1
    outs = refs[pos:]
    b, j = pl.program_id(0), pl.program_id(1)
    d = x_ref.shape[-1]
    row = jnp.where(j + row_off_blk < nctx_blk, ctx_row, b)
    gate = _mod_row(mod_ref, row, k_gate, d)
    xn = _layer_norm_rows(alpha * x_ref[...] + gate * y_ref[...]) * gb_ref[0:1, :] + gb_ref[1:2, :]
    outs[0][...] = xn
    if not with_mod:
        return
    h = _layer_norm_rows(xn) * (1.0 + _mod_row(mod_ref, row, k_scale, d)) + _mod_row(mod_ref, row, k_shift, d)
    if outs[1].dtype == jnp.uint32:
        outs[1][...] = _pack_pairs(h, min(PACK_GROUP, d))
    else:
        outs[1][...] = h.astype(outs[1].dtype)
    if not n_exp:
        return
    logits = _mm(h, router_ref[...], passes=3)
    lane = lax.broadcasted_iota(jnp.int32, logits.shape, 1).astype(F32)
    neg = -jnp.inf
    logits = jnp.where(lane < n_exp, logits, neg)
    m1 = jnp.max(logits, axis=-1, keepdims=True)
    i1 = jnp.min(jnp.where(logits == m1, lane, float(LANES)), axis=-1, keepdims=True)
    rest = jnp.where(lane == i1, neg, logits)
    m2 = jnp.max(rest, axis=-1, keepdims=True)
    i2 = jnp.min(jnp.where(rest == m2, lane, float(LANES)), axis=-1, keepdims=True)
    e2 = jnp.exp(m2 - m1)
    g1 = 1.0 / (1.0 + e2)
    g2 = e2 / (1.0 + e2)
    outs[2][...] = jnp.where(lane == _R_E1, i1, jnp.where(lane == _R_E2, i2,
                             jnp.where(lane == _R_G1, g1, jnp.where(lane == _R_G2, g2, 0.0))))


def _residual_ln(x, y, mod, gain, bias, *, alpha, tr, lc, ctx_row, k_gate, x_off=0, y_off=0,
                 rows=None, mod2=None, router=None, pack_h=False):
    bsz, _, d = x.shape
    rows = x.shape[1] if rows is None else rows
    n_exp = 0 if router is None else router.shape[1]
    gb = jnp.stack([gain, bias])
    in_specs = [pl.BlockSpec((None, tr, d), lambda b, j: (b, j + x_off // tr, 0)),
                pl.BlockSpec((None, tr, d), lambda b, j: (b, j + y_off // tr, 0)),
                pl.BlockSpec(mod.shape, lambda b, j: (0, 0)),
                pl.BlockSpec((2, d), lambda b, j: (0, 0))]
    args = [x, y, mod, gb]
    out_specs = [pl.BlockSpec((None, tr, d), lambda b, j: (b, j, 0))]
    out_shape = [jax.ShapeDtypeStruct((bsz, rows, d), F32)]
    if mod2 is not None:
        h_cols, h_dtype = (d // 2, jnp.uint32) if pack_h else (d, BF16)
        out_specs.append(pl.BlockSpec((None, tr, h_cols), lambda b, j: (b, j, 0)))
        out_shape.append(jax.ShapeDtypeStruct((bsz, rows, h_cols), h_dtype))
    if router is not None:
        rpad = jnp.zeros((d, LANES), F32).at[:, :n_exp].set(router)
        in_specs.append(pl.BlockSpec((d, LANES), lambda b, j: (0, 0)))
        args.append(rpad)
        out_specs.append(pl.BlockSpec((None, tr, LANES), lambda b, j: (b, j, 0)))
        out_shape.append(jax.ShapeDtypeStruct((bsz, rows, LANES), F32))
    k_shift, k_scale = mod2 if mod2 is not None else (0, 0)
    kern = functools.partial(_res_kernel, alpha=alpha, row_off_blk=x_off // tr, nctx_blk=lc // tr,
                             ctx_row=ctx_row, k_gate=k_gate, k_shift=k_shift, k_scale=k_scale,
                             with_mod=mod2 is not None, n_exp=n_exp)
    return pl.pallas_call(
        kern,
        grid=(bsz, rows // tr),
        in_specs=in_specs,
        out_specs=out_specs,
        out_shape=out_shape,
        compiler_params=_cparams(("parallel", "parallel")),
        name="residual_ln",
    )(*args)


def _route_plan(route, n_exp, tile):
    n = route.shape[0]
    e_flat = jnp.concatenate([route[:, _R_E1], route[:, _R_E2]]).astype(jnp.int32)
    experts = jnp.arange(n_exp, dtype=jnp.int32)
    onehot = (e_flat[:, None] == experts[None, :]).astype(jnp.int32)
    csum = jnp.cumsum(onehot, axis=0)
    rank = jnp.sum((csum - onehot) * onehot, axis=1)
    tiles_e = (csum[-1] + tile - 1) // tile
    tile_end = jnp.cumsum(tiles_e)
    tile_start = tile_end - tiles_e
    pos = jnp.sum(onehot * tile_start[None, :], axis=1) * tile + rank
    n_tiles = (2 * n) // tile + n_exp
    tidx = jnp.arange(n_tiles, dtype=jnp.int32)
    valid = tidx < tile_end[-1]
    last_e = jnp.max(jnp.where(tiles_e > 0, experts, 0))
    tile_expert = jnp.sum((tidx[:, None] >= tile_end[None, :]).astype(jnp.int32), axis=1)
    tile_expert = jnp.where(valid, tile_expert, last_e)
    token = jnp.arange(2 * n, dtype=jnp.int32) % n
    src_token = jnp.zeros((n_tiles * tile,), jnp.int32).at[pos].set(token)
    return src_token, pos, tile_expert, valid.astype(jnp.int32)


def _gather_kernel(idx_ref, src_ref, out_ref, sem):
    rows = out_ref.shape[0]

    def row_copy(j, src_row):
        return pltpu.make_async_copy(src_ref.at[pl.ds(src_row, 1)], out_ref.at[pl.ds(j, 1)], sem)

    def start_pair(i, carry):
        for prio in range(2):
            j = 2 * i + prio
            row_copy(j, idx_ref[0, j]).start(priority=prio)
        return carry

    def wait(j, carry):
        row_copy(j, 0).wait()
        return carry

    lax.fori_loop(0, rows // 2, start_pair, 0, unroll=4)
    lax.fori_loop(0, rows, wait, 0, unroll=8)


def _gather_rows(src, idx, *, rows):
    n_out = idx.shape[0]
    d = src.shape[1]
    rows = _pick(n_out, rows, SUBLANES)
    return pl.pallas_call(
        _gather_kernel,
        grid=(n_out // rows,),
        in_specs=[pl.BlockSpec((None, 1, rows), lambda i: (i, 0, 0), memory_space=pltpu.SMEM),
                  pl.BlockSpec(memory_space=pl.ANY)],
        out_specs=pl.BlockSpec((rows, d), lambda i: (i, 0)),
        out_shape=jax.ShapeDtypeStruct((n_out, d), src.dtype),
        scratch_shapes=[pltpu.SemaphoreType.DMA(())],
        compiler_params=_cparams(("arbitrary",)),
        name="row_gather",
    )(idx.reshape(n_out // rows, 1, rows), src)


def _moe_glu_kernel(te_ref, tv_ref, a_ref, w1_ref, w3_ref, o_ref):
    i = pl.program_id(1)

    @pl.when(tv_ref[i] != 0)
    def _():
        a = _unpack_pairs(a_ref[...], min(PACK_GROUP, 2 * a_ref.shape[1])).astype(BF16)
        p1 = _dot(a, w1_ref[...])
        p3 = _dot(a, w3_ref[...])
        o_ref[...] = (p1 * _sigmoid(p1) * p3).astype(o_ref.dtype)

    @pl.when(tv_ref[i] == 0)
    def _():
        o_ref[...] = jnp.zeros_like(o_ref)


def _moe_down_kernel(te_ref, tv_ref, h_ref, w2_ref, o_ref):
    i = pl.program_id(1)

    @pl.when(tv_ref[i] != 0)
    def _():
        o_ref[...] = _pack_pairs(_dot(h_ref[...], w2_ref[...]), min(PACK_GROUP, 2 * o_ref.shape[1]))

    @pl.when(tv_ref[i] == 0)
    def _():
        o_ref[...] = jnp.zeros_like(o_ref)


def _moe_experts(xs, w1, w3, w2, tile_expert, tile_valid, *, tile, tn_up, tn_down):
    r, d_half = xs.shape
    d = 2 * d_half
    _, _, f = w1.shape
    n_tiles = r // tile
    tn_up, tn_down = _pick(f, tn_up, LANES), _pick(d, tn_down, min(PACK_GROUP, d))
    up_spec = pl.BlockSpec((None, d, tn_up), lambda j, i, te, tv: (te[i], 0, j))
    hid = pl.pallas_call(
        _moe_glu_kernel,
        grid_spec=pltpu.PrefetchScalarGridSpec(
            num_scalar_prefetch=2, grid=(f // tn_up, n_tiles),
            in_specs=[pl.BlockSpec((tile, d_half), lambda j, i, te, tv: (i, 0)), up_spec, up_spec],
            out_specs=pl.BlockSpec((tile, tn_up), lambda j, i, te, tv: (i, j))),
        out_shape=jax.ShapeDtypeStruct((r, f), BF16),
        compiler_params=_cparams(("parallel", "arbitrary")),
        name="moe_glu",
    )(tile_expert, tile_valid, xs, w1, w3)
    return pl.pallas_call(
        _moe_down_kernel,
        grid_spec=pltpu.PrefetchScalarGridSpec(
            num_scalar_prefetch=2, grid=(d // tn_down, n_tiles),
            in_specs=[pl.BlockSpec((tile, f), lambda j, i, te, tv: (i, 0)),
                      pl.BlockSpec((None, f, tn_down), lambda j, i, te, tv: (te[i], 0, j))],
            out_specs=pl.BlockSpec((tile, tn_down // 2), lambda j, i, te, tv: (i, j))),
        out_shape=jax.ShapeDtypeStruct((r, d_half), jnp.uint32),
        compiler_params=_cparams(("parallel", "arbitrary")),
        name="moe_down",
    )(tile_expert, tile_valid, hid, w2)


def _moe_out_kernel(x_ref, y1_ref, y2_ref, route_ref, mod_ref, gb_ref, o_ref, *, alpha, k_gate):
    b = pl.program_id(0)
    d = x_ref.shape[-1]
    route = route_ref[...]
    group = min(PACK_GROUP, d)
    y = (route[:, _R_G1:_R_G1 + 1] * _unpack_pairs(y1_ref[...], group)
         + route[:, _R_G2:_R_G2 + 1] * _unpack_pairs(y2_ref[...], group))
    gate = _mod_row(mod_ref, b, k_gate, d)
    o_ref[...] = _layer_norm_rows(alpha * x_ref[...] + gate * y) * gb_ref[0:1, :] + gb_ref[1:2, :]


def _moe_residual_ln(x, yg, route, mod, gain, bias, *, alpha, tr, k_gate):
    bsz, rows, d = x.shape
    row_spec = pl.BlockSpec((None, tr, d), lambda b, j: (b, j, 0))
    return pl.pallas_call(
        functools.partial(_moe_out_kernel, alpha=alpha, k_gate=k_gate),
        grid=(bsz, rows // tr),
        in_specs=[row_spec,
                  pl.BlockSpec((None, None, tr, d // 2), lambda b, j: (0, b, j, 0)),
                  pl.BlockSpec((None, None, tr, d // 2), lambda b, j: (1, b, j, 0)),
                  pl.BlockSpec((None, tr, LANES), lambda b, j: (b, j, 0)),
                  pl.BlockSpec(mod.shape, lambda b, j: (0, 0)),
                  pl.BlockSpec((2, d), lambda b, j: (0, 0))],
        out_specs=row_spec,
        out_shape=jax.ShapeDtypeStruct((bsz, rows, d), F32),
        compiler_params=_cparams(("parallel", "parallel")),
        name="moe_residual_ln",
    )(x, yg, yg, route, mod, jnp.stack([gain, bias]))


def _pad_rows(w, rows):
    return jnp.zeros((rows,) + w.shape[1:], w.dtype).at[:w.shape[0]].set(w)


def _layer_layout(p, pool_w, rw):
    lr_raw = p["w2_f"].shape[0]
    gl_raw = p["g_up"].shape[0]
    vr_raw = p["v2"].shape[0] if "v2" in p else 0
    lr = -(-max(lr_raw, vr_raw, 1) // LANES) * LANES
    gl = -(-gl_raw // LANES) * LANES
    wl = 5 * lr + gl
    wl_pad = -(-wl // 1024) * 1024 if wl > 512 else wl
    core = pool_w + 3 * rw
    sizes = [lr_raw] * 4 + [gl_raw] + ([vr_raw] if vr_raw else [])
    slots = [lr] * 4 + [gl] + [lr]
    d = p["w_in"].shape[0]

    def relayout(src, rows_shape):
        dst = jnp.zeros(rows_shape + (core + wl_pad,), src.dtype)
        dst = dst.at[..., :core].set(src[..., :core])
        s_off, d_off = core, core
        for size, slot in zip(sizes, slots):
            dst = dst.at[..., d_off:d_off + size].set(src[..., s_off:s_off + size])
            s_off += size
            d_off += slot
        return dst

    w_in = relayout(p["w_in"].astype(BF16), (d,))
    zero_pool = jnp.zeros((pool_w,), F32)
    mu_p = relayout(jnp.concatenate([zero_pool, p["mu_prev"]]), ())
    mu_n = relayout(jnp.concatenate([zero_pool, p["mu_next"]]), ())
    mu_lora = jnp.zeros((SUBLANES, wl_pad), F32).at[0].set(mu_p[core:]).at[1].set(mu_n[core:])
    seg = lambda a, i: a[pool_w + i * rw:pool_w + (i + 1) * rw]
    zeros = jnp.zeros((rw,), F32)
    vec = jnp.stack([seg(mu_p, 0), seg(mu_n, 0), seg(mu_p, 1), seg(mu_n, 1), seg(mu_p, 2), seg(mu_n, 2),
                     p["w0_f"], p["w0_b"], p["a0_f"], p["a0_b"], p.get("v0", zeros),
                     p["k_k"], p["k_a"], p["r_k"].reshape(-1), p["lnx_g"], p["lnx_b"]])
    lw = {n: _pad_rows(p[n], lr).astype(BF16) for n in ("w2_f", "w2_b", "a2_f", "a2_b")}
    lw["g_up"] = _pad_rows(p["g_up"], gl).astype(BF16)
    if vr_raw:
        lw["v2"] = _pad_rows(p["v2"], lr).astype(BF16)
    return dict(w_in=w_in, mu_lora=mu_lora, vec=vec, lw=lw, lr=lr, gl=gl, wl=wl_pad, core=core)


def _mixer(x_all, h, mod, p, lay, v_first, pool_consts, *, lc, tr, tc, grid_w, tb, ctx_row):
    bsz, t, d = x_all.shape
    mix_w = p["w_out"].shape[0]
    pool_w = mix_w // 4
    pg = pool_w // len(POOL_WINDOWS)
    rw = mix_w - pool_w
    if h is None:
        h = _ln_mod(x_all, mod, lc=lc, tr=tr, ctx_row=ctx_row, k_shift=0, k_scale=1)
    u = _matmul(h.reshape(bsz * t, d), lay["w_in"], out_dtype=BF16, tm=1024, tn=1024, tk=d,
                name="in_proj").reshape(bsz, t, -1)
    pool_wts = p["pool_w"].astype(BF16)
    pool_scale = p["pool_scale"].reshape(1, pool_w)
    pools = [_pool_group(u, pool_consts, pool_wts, pool_scale, i, lc=lc, grid_w=grid_w, tb=tb, pg=pg)
             for i in range(len(POOL_WINDOWS))]
    act = _lora_act(u, lay["mu_lora"], lc=lc, tr=tr, col0=lay["core"], wl=lay["wl"], lr=lay["lr"],
                    gl=lay["gl"])
    tm = _rwkv_terms(u, act, lay["vec"], lay["lw"], v_first, lc=lc, tr=tr, tc=tc, pool_w=pool_w,
                     rw=rw, lr=lay["lr"], gl=lay["gl"])
    y_f, y_b = _wkv_scan(tm, lc=lc, chunk=CHUNK, pairs=_pick(rw // LANES, SCAN_PAIRS, 1))
    out = _rwkv_out(y_f, y_b, tm["bonus"], tm["gate"], lay["vec"], tr=tr, tc=tc)
    mix_in = jnp.concatenate(pools + [out], axis=-1)
    mix = _matmul(mix_in.reshape(bsz * t, mix_w), p["w_out"].astype(BF16), out_dtype=BF16,
                  tm=1024, tn=1024, tk=mix_w, name="out_proj").reshape(bsz, t, d)
    return mix, tm["v"]


def _forward(x, c, ctx, c_ctx, layers, grid_w):
    bsz, seq, d = x.shape
    lc = ctx.shape[1]
    depth = len(layers)
    alpha = (2 * depth) ** 0.25
    assert bsz < COND_ROWS and lc % CHUNK == 0 and seq % CHUNK == 0 and seq % grid_w == 0
    tr = _pick(math.gcd(lc, seq), 256, SUBLANES)
    tb = tr if tr % grid_w == 0 else grid_w
    assert seq % tb == 0 and tb % grid_w == 0
    ctx_row = bsz
    cond = jnp.zeros((COND_ROWS, d), F32).at[:bsz].set(c).at[ctx_row].set(c_ctx)
    x_all = jnp.concatenate([ctx, x], axis=1)
    pool_consts = _pool_consts(lc, seq, grid_w, tb)
    common = dict(tr=tr, lc=lc, ctx_row=ctx_row)
    mods = [_modulation(cond, p["w_ada"], p["b_ada"]) for p in layers]
    v_first = None
    h_next = None
    for i, p in enumerate(layers):
        last = i == depth - 1
        mix_w = p["w_out"].shape[0]
        pool_w = mix_w // 4
        rw = mix_w - pool_w
        assert pool_w % (len(POOL_WINDOWS) * LANES) == 0 and rw % LANES == 0
        tc = _pick(rw, 512, LANES)
        lay = _layer_layout(p, pool_w, rw)
        mod = mods[i]
        mix, v_cur = _mixer(x_all, h_next, mod, p, lay, v_first, pool_consts, tc=tc, grid_w=grid_w,
                            tb=tb, **common)
        if v_first is None:
            v_first = v_cur
        if not last:
            x1, h2 = _residual_ln(x_all, mix, mod, p["ln1_g"], p["ln1_b"], alpha=alpha, k_gate=2,
                                  mod2=(3, 4), **common)
            t = x_all.shape[1]
            hid = _glu(h2.reshape(bsz * t, d), p["ffn_w1"].astype(BF16), p["ffn_w3"].astype(BF16),
                       tm=1024, tn=256, name="ffn_glu")
            ffn = _matmul(hid, p["ffn_w2"].astype(BF16), out_dtype=BF16, tm=512, tn=1024, tk=5504,
                          name="ffn_down").reshape(bsz, t, d)
            mod_pair = jnp.concatenate([mod, mods[i + 1]], axis=1)
            x_all, h_next = _residual_ln(x1, ffn, mod_pair, p["ln2_g"], p["ln2_b"], alpha=alpha,
                                         k_gate=5, mod2=(N_ADA, N_ADA + 1), **common)
        else:
            x1, h2, route = _residual_ln(x_all, mix, mod, p["ln1_g"], p["ln1_b"], alpha=alpha, k_gate=2,
                                         mod2=(3, 4), router=p["router"], pack_h=True, x_off=lc,
                                         y_off=lc, rows=seq, **common)
            n_exp = p["router"].shape[1]
            ntok = bsz * seq
            tile = _pick(2 * ntok, 512, SUBLANES)
            src_token, pos, tile_expert, tile_valid = _route_plan(route.reshape(ntok, LANES), n_exp, tile)
            xs = _gather_rows(h2.reshape(ntok, d // 2), src_token, rows=256)
            ys = _moe_experts(xs, p["exp_w1"].astype(BF16), p["exp_w3"].astype(BF16),
                              p["exp_w2"].astype(BF16), tile_expert, tile_valid, tile=tile,
                              tn_up=512, tn_down=1024)
            yg = _gather_rows(ys, pos, rows=256).reshape(2, bsz, seq, d // 2)
            return _moe_residual_ln(x1, yg, route, mod, p["ln2_g"], p["ln2_b"], alpha=alpha, tr=tr,
                                    k_gate=5)
    return x_all[:, lc:]


_LAYER0 = ("w_ada", "b_ada", "w_in", "mu_prev", "mu_next", "pool_w", "pool_scale", "w0_f", "w2_f",
           "w0_b", "w2_b", "a0_f", "a2_f", "a0_b", "a2_b", "g_up", "k_k", "k_a", "r_k", "lnx_g",
           "lnx_b", "w_out", "ln1_g", "ln1_b", "ln2_g", "ln2_b", "ffn_w1", "ffn_w3", "ffn_w2")
_LAYER1 = _LAYER0[:26] + ("v0", "v2", "router", "exp_w1", "exp_w3", "exp_w2")


def kernel(x, c, ctx, c_ctx, l0_w_ada, l0_b_ada, l0_w_in, l0_mu_prev, l0_mu_next, l0_pool_w, l0_pool_scale, l0_w0_f, l0_w2_f, l0_w0_b, l0_w2_b, l0_a0_f, l0_a2_f, l0_a0_b, l0_a2_b, l0_g_up, l0_k_k, l0_k_a, l0_r_k, l0_lnx_g, l0_lnx_b, l0_w_out, l0_ln1_g, l0_ln1_b, l0_ln2_g, l0_ln2_b, l0_ffn_w1, l0_ffn_w3, l0_ffn_w2, l1_w_ada, l1_b_ada, l1_w_in, l1_mu_prev, l1_mu_next, l1_pool_w, l1_pool_scale, l1_w0_f, l1_w2_f, l1_w0_b, l1_w2_b, l1_a0_f, l1_a2_f, l1_a0_b, l1_a2_b, l1_g_up, l1_k_k, l1_k_a, l1_r_k, l1_lnx_g, l1_lnx_b, l1_w_out, l1_ln1_g, l1_ln1_b, l1_ln2_g, l1_ln2_b, l1_v0, l1_v2, l1_router, l1_exp_w1, l1_exp_w3, l1_exp_w2):
    l0 = dict(zip(_LAYER0, (l0_w_ada, l0_b_ada, l0_w_in, l0_mu_prev, l0_mu_next, l0_pool_w, l0_pool_scale, l0_w0_f, l0_w2_f, l0_w0_b, l0_w2_b, l0_a0_f, l0_a2_f, l0_a0_b, l0_a2_b, l0_g_up, l0_k_k, l0_k_a, l0_r_k, l0_lnx_g, l0_lnx_b, l0_w_out, l0_ln1_g, l0_ln1_b, l0_ln2_g, l0_ln2_b, l0_ffn_w1, l0_ffn_w3, l0_ffn_w2)))
    l1 = dict(zip(_LAYER1, (l1_w_ada, l1_b_ada, l1_w_in, l1_mu_prev, l1_mu_next, l1_pool_w, l1_pool_scale, l1_w0_f, l1_w2_f, l1_w0_b, l1_w2_b, l1_a0_f, l1_a2_f, l1_a0_b, l1_a2_b, l1_g_up, l1_k_k, l1_k_a, l1_r_k, l1_lnx_g, l1_lnx_b, l1_w_out, l1_ln1_g, l1_ln1_b, l1_ln2_g, l1_ln2_b, l1_v0, l1_v2, l1_router, l1_exp_w1, l1_exp_w3, l1_exp_w2)))
    return _forward(x, c, ctx, c_ctx, [l0, l1], GRID_W)
```
